```python
import jax, jax.numpy as jnp
from jax import lax
import numpy as np

D_MODEL = 1024
BATCH = 2
SEQ = 8192
DEPTH = 1

HEAD_DIM = 64
ATTN_GROUPS = ((128, 1), (512, 4), (2048, 16))
ATTN_HEADS_PER_GROUP = 4
ATTN_HEADS = ATTN_HEADS_PER_GROUP * len(ATTN_GROUPS)
ATTN_WIDTH = ATTN_HEADS * HEAD_DIM
ATTN_OUT = ATTN_HEADS_PER_GROUP * HEAD_DIM
ROPE_THETA = 500000.0
ROPE_DIM = HEAD_DIM // 4

RWKV_HEADS = D_MODEL // HEAD_DIM
RWKV_WIDTH = RWKV_HEADS * HEAD_DIM
DECAY_LORA = 64
ICLR_LORA = 64
GATE_LORA = 160
RWKV_PROJ = 3 * RWKV_WIDTH + DECAY_LORA + ICLR_LORA + GATE_LORA

N_BRANCH = 2
GATE_WIDTH = N_BRANCH * D_MODEL
IN_WIDTH = 3 * ATTN_WIDTH + RWKV_PROJ + GATE_WIDTH

D_FF = ((8 * D_MODEL // 3 + 127) // 128) * 128
CONV_WIDTH = 3
RMS_EPS = 1e-6
GN_EPS = 64e-5

kernel_name = "hybrid_dilated_attn_rwkv7_convffn"


def rms_norm(x, g):
    xf = x.astype(jnp.float32)
    y = xf * lax.rsqrt(jnp.mean(xf * xf, axis=-1, keepdims=True) + RMS_EPS)
    return (y * g.astype(jnp.float32)).astype(x.dtype)


def partial_rotary(t, positions):
    half = ROPE_DIM // 2
    inv_freq = jnp.power(ROPE_THETA, -jnp.arange(0, ROPE_DIM, 2, dtype=jnp.float32) / ROPE_DIM)
    ang = positions.astype(jnp.float32)[:, None] * inv_freq[None, :]
    cos = jnp.cos(ang)[None, :, None, :]
    sin = jnp.sin(ang)[None, :, None, :]
    tf = t.astype(jnp.float32)
    t1, t2 = tf[..., :half], tf[..., half:ROPE_DIM]
    out = jnp.concatenate([t1 * cos - t2 * sin, t2 * cos + t1 * sin, tf[..., ROPE_DIM:]], axis=-1)
    return out.astype(t.dtype)


def dilated_window_attention(q, k, v, window, dilation):
    B, S, H, Dh = q.shape
    span = window // dilation
    L = span
    unit = dilation * L
    S_pad = -(-S // unit) * unit
    M = S_pad // dilation
    nb = M // L

    def to_blocks(t):
        t = jnp.pad(t.astype(jnp.float32), ((0, 0), (0, S_pad - S), (0, 0), (0, 0)))
        t = t.reshape(B, M, dilation, H, Dh).transpose(0, 2, 1, 3, 4)
        return t.reshape(B, dilation, nb, L, H, Dh)

    qb, kb, vb = to_blocks(q), to_blocks(k), to_blocks(v)

    def with_prev(t):
        prev = jnp.pad(t, ((0, 0), (0, 0), (1, 0), (0, 0), (0, 0), (0, 0)))[:, :, :-1]
        return jnp.concatenate([prev, t], axis=3)

    kc, vc = with_prev(kb), with_prev(vb)
    scores = jnp.einsum('brnqhd,brnkhd->brnhqk', qb, kc) * (Dh ** -0.5)
    qi = jnp.arange(L)[:, None]
    kj = jnp.arange(2 * L)[None, :]
    rel = L + qi - kj
    band = (rel >= 0) & (rel <= span)
    has_prev = (jnp.arange(nb) > 0)[:, None, None] | (kj >= L)[None]
    valid = band[None] & has_prev
    scores = jnp.where(valid[None, None, :, None], scores, -jnp.inf)
    m = jnp.max(scores, axis=-1, keepdims=True)
    e = jnp.exp(scores - m)
    den = jnp.sum(e, axis=-1, keepdims=True)
    o = jnp.einsum('brnhqk,brnkhd->brnqhd', e / den, vc)
    lse = (m + jnp.log(den))[..., 0]
    o = o.reshape(B, dilation, M, H, Dh).transpose(0, 2, 1, 3, 4).reshape(B, S_pad, H, Dh)[:, :S]
    lse = lse.transpose(0, 1, 2, 4, 3).reshape(B, dilation, M, H).transpose(0, 2, 1, 3)
    lse = lse.reshape(B, S_pad, H)[:, :S]
    return o, lse


def rwkv7_scan(r, decay, k, v, kk, a):
    B, S, H, N = r.shape

    def step(state, inp):
        r_t, w_t, k_t, v_t, kk_t, a_t = inp
        sa = jnp.einsum('bhvk,bhk->bhv', state, -kk_t)
        state = (state * w_t[:, :, None, :]
                 + sa[..., None] * (kk_t * a_t)[:, :, None, :]
                 + v_t[..., None] * k_t[:, :, None, :])
        y_t = jnp.einsum('bhvk,bhk->bhv', state, r_t)
        return state, y_t

    xs = tuple(jnp.swapaxes(t.astype(jnp.float32), 0, 1) for t in (r, decay, k, v, kk, a))
    s0 = jnp.zeros((B, H, N, N), jnp.float32)
    _, ys = lax.scan(step, s0, xs)
    return jnp.swapaxes(ys, 0, 1)


def rwkv7_branch(p, mu_shift, w0, w_decay_up, a0, w_a_up, w_g_up, k_k, k_a, r_k, ln_x_w, ln_x_b):
    B, S, _ = p.shape
    H, N = RWKV_HEADS, HEAD_DIM
    prev = jnp.pad(p, ((0, 0), (1, 0), (0, 0)))[:, :-1]
    p = p + (prev - p) * mu_shift
    c = np.cumsum([RWKV_WIDTH, RWKV_WIDTH, RWKV_WIDTH, DECAY_LORA, ICLR_LORA])
    r, k, v, zw, za, zg = jnp.split(p, [int(i) for i in c], axis=-1)
    w_log = -jax.nn.softplus(-(w0 + jnp.tanh(zw) @ w_decay_up)) - 0.5
    decay = jnp.exp(-jnp.exp(w_log.astype(jnp.float32)))
    a = jax.nn.sigmoid(a0 + za @ w_a_up)
    g = jax.nn.sigmoid(zg) @ w_g_up
    kk = (k * k_k).astype(jnp.float32).reshape(B, S, H, N)
    kk = kk / jnp.maximum(jnp.linalg.norm(kk, axis=-1, keepdims=True), 1e-12)
    k = k * (1.0 + (a - 1.0) * k_a)
    rh = r.astype(jnp.float32).reshape(B, S, H, N)
    kh = k.astype(jnp.float32).reshape(B, S, H, N)
    vh = v.astype(jnp.float32).reshape(B, S, H, N)
    ah = a.astype(jnp.float32).reshape(B, S, H, N)
    y = rwkv7_scan(rh, decay.reshape(B, S, H, N), kh, vh, kk, ah)
    mu = jnp.mean(y, axis=-1, keepdims=True)
    var = jnp.mean(jnp.square(y - mu), axis=-1, keepdims=True)
    y = (y - mu) * lax.rsqrt(var + GN_EPS)
    y = y * ln_x_w.astype(jnp.float32).reshape(H, N) + ln_x_b.astype(jnp.float32).reshape(H, N)
    y = y + jnp.sum(rh * kh * r_k.astype(jnp.float32), axis=-1, keepdims=True) * vh
    return (y.reshape(B, S, RWKV_WIDTH) * g.astype(jnp.float32)).astype(p.dtype)


def conv_ffn(h, w_ffn_up, conv_w, conv_b, w_ffn_down):
    u = h @ w_ffn_up
    u = lax.conv_general_dilated(
        u, conv_w, window_strides=(1,), padding=[(CONV_WIDTH - 1, 0)],
        dimension_numbers=('NWC', 'WIO', 'NWC'), feature_group_count=u.shape[-1]) + conv_b
    gate, val = jnp.split(u, 2, axis=-1)
    return (jax.nn.silu(gate) * val) @ w_ffn_down


def setup_inputs(seed: int = 0) -> dict:
    key = jax.random.key(seed)
    ks = jax.random.split(key, 26)
    f32 = jnp.float32

    def nrm(k, shape, scale):
        return jax.random.normal(k, shape, f32) * scale

    Lh = DEPTH
    return {
        "x": nrm(ks[0], (BATCH, SEQ, D_MODEL), 1.0),
        "norm_mix_g": 1.0 + nrm(ks[1], (Lh, D_MODEL), 0.02),
        "w_in": nrm(ks[2], (Lh, D_MODEL, IN_WIDTH), D_MODEL ** -0.5),
        "b_gate": nrm(ks[3], (Lh, GATE_WIDTH), 0.02),
        "mu_shift": jax.random.uniform(ks[4], (Lh, RWKV_PROJ), f32),
        "w0": -2.0 + nrm(ks[5], (Lh, RWKV_WIDTH), 1.0),
        "w_decay_up": nrm(ks[6], (Lh, DECAY_LORA, RWKV_WIDTH), 0.1 * DECAY_LORA ** -0.5),
        "a0": nrm(ks[7], (Lh, RWKV_WIDTH), 0.5),
        "w_a_up": nrm(ks[8], (Lh, ICLR_LORA, RWKV_WIDTH), ICLR_LORA ** -0.5),
        "w_g_up": nrm(ks[9], (Lh, GATE_LORA, RWKV_WIDTH), GATE_LORA ** -0.5),
        "k_k": 0.85 + nrm(ks[10], (Lh, RWKV_WIDTH), 0.02),
        "k_a": 1.0 + nrm(ks[11], (Lh, RWKV_WIDTH), 0.02),
        "r_k": nrm(ks[12], (Lh, RWKV_HEADS, HEAD_DIM), 0.1),
        "ln_x_w": 1.0 + nrm(ks[13], (Lh, RWKV_WIDTH), 0.02),
        "ln_x_b": nrm(ks[14], (Lh, RWKV_WIDTH), 0.02),
        "w_branch_attn": nrm(ks[15], (Lh, ATTN_OUT, D_MODEL), ATTN_OUT ** -0.5),
        "w_branch_rwkv": nrm(ks[16], (Lh, RWKV_WIDTH, D_MODEL), RWKV_WIDTH ** -0.5),
        "w_out": nrm(ks[17], (Lh, D_MODEL, D_MODEL), D_MODEL ** -0.5),
        "norm_ffn_g": 1.0 + nrm(ks[18], (Lh, D_MODEL), 0.02),
        "w_ffn_up": nrm(ks[19], (Lh, D_MODEL, 2 * D_FF), D_MODEL ** -0.5),
        "conv_w": nrm(ks[20], (Lh, CONV_WIDTH, 1, 2 * D_FF), CONV_WIDTH ** -0.5),
        "conv_b": nrm(ks[21], (Lh, 2 * D_FF), 0.02),
        "w_ffn_down": nrm(ks[22], (Lh, D_FF, D_MODEL), D_FF ** -0.5),
        "norm_final_g": 1.0 + nrm(ks[23], (D_MODEL,), 0.02),
    }


def reference(x, norm_mix_g, w_in, b_gate, mu_shift, w0, w_decay_up, a0, w_a_up, w_g_up,
              k_k, k_a, r_k, ln_x_w, ln_x_b, w_branch_attn, w_branch_rwkv, w_out,
              norm_ffn_g, w_ffn_up, conv_w, conv_b, w_ffn_down, norm_final_g):
    B, S, _ = x.shape
    positions = jnp.arange(S, dtype=jnp.int32)
    for l in range(DEPTH):
        h = rms_norm(x, norm_mix_g[l])
        proj = h @ w_in[l]
        q = proj[..., :ATTN_WIDTH].reshape(B, S, ATTN_HEADS, HEAD_DIM)
        k = proj[..., ATTN_WIDTH:2 * ATTN_WIDTH].reshape(B, S, ATTN_HEADS, HEAD_DIM)
        v = proj[..., 2 * ATTN_WIDTH:3 * ATTN_WIDTH].reshape(B, S, ATTN_HEADS, HEAD_DIM)
        p_rwkv = proj[..., 3 * ATTN_WIDTH:3 * ATTN_WIDTH + RWKV_PROJ]
        gate_logits = proj[..., 3 * ATTN_WIDTH + RWKV_PROJ:] + b_gate[l]

        q = partial_rotary(q, positions)
        k = partial_rotary(k, positions)
        outs, lses = [], []
        for gi, (window, dilation) in enumerate(ATTN_GROUPS):
            sl = slice(gi * ATTN_HEADS_PER_GROUP, (gi + 1) * ATTN_HEADS_PER_GROUP)
            o_g, lse_g = dilated_window_attention(q[:, :, sl], k[:, :, sl], v[:, :, sl], window, dilation)
            outs.append(o_g)
            lses.append(lse_g)
        wts = jax.nn.softmax(jnp.stack(lses, axis=0), axis=0)
        y_attn = jnp.sum(wts[..., None] * jnp.stack(outs, axis=0), axis=0)
        y_attn = y_attn.reshape(B, S, ATTN_OUT).astype(x.dtype)

        y_rwkv = rwkv7_branch(p_rwkv, mu_shift[l], w0[l], w_decay_up[l], a0[l], w_a_up[l], w_g_up[l],
                              k_k[l], k_a[l], r_k[l], ln_x_w[l], ln_x_b[l])

        g_attn, g_rwkv = jnp.split(jax.nn.sigmoid(gate_logits), 2, axis=-1)
        merged = g_attn * (y_attn @ w_branch_attn[l]) + g_rwkv * (y_rwkv @ w_branch_rwkv[l])
        x = x + merged @ w_out[l]

        x = x + conv_ffn(rms_norm(x, norm_ffn_g[l]), w_ffn_up[l], conv_w[l], conv_b[l], w_ffn_down[l])
    return rms_norm(x, norm_final_g)
```

```python
import functools

import jax
import jax.numpy as jnp
from jax import lax
from jax.experimental import pallas as pl
from jax.experimental.pallas import tpu as pltpu

F32 = jnp.float32
BF16 = jnp.bfloat16

HEAD_DIM = 64
LANES = 128
ATTN_GROUPS = ((128, 1), (512, 4), (2048, 16))
ATTN_SPAN = 128
ATTN_TILE = 2048
ROPE_THETA = 500000.0
ROPE_DIM = HEAD_DIM // 4
RMS_EPS = 1e-6
GN_EPS = 64e-5
CHUNK = 64
NEG_BIG = -1e30
VMEM_LIMIT = 56 * 1024 * 1024


def _cparams(*sem):
    return pltpu.CompilerParams(dimension_semantics=sem, vmem_limit_bytes=VMEM_LIMIT)


def _mm(a, b):
    return jnp.dot(a.astype(BF16), b.astype(BF16), preferred_element_type=F32)


def _mm_nt(a, b):
    return lax.dot_general(a.astype(BF16), b.astype(BF16), (((1,), (1,)), ((), ())),
                           preferred_element_type=F32)


def _mm_tn(a, b):
    return lax.dot_general(a.astype(BF16), b.astype(BF16), (((0,), (0,)), ((), ())),
                           preferred_element_type=F32)


def _split3(a):
    hi = a.astype(BF16)
    r1 = a - hi.astype(F32)
    mid = r1.astype(BF16)
    lo = (r1 - mid.astype(F32)).astype(BF16)
    return hi, mid, lo


def _mm_exact_rhs(a, b_bf16):
    hi, mid, lo = _split3(a)
    out = jnp.dot(lo, b_bf16, preferred_element_type=F32)
    out = out + jnp.dot(mid, b_bf16, preferred_element_type=F32)
    return out + jnp.dot(hi, b_bf16, preferred_element_type=F32)


def _mm_exact_lhs(a_bf16, b):
    hi, mid, lo = _split3(b)
    out = jnp.dot(a_bf16, lo, preferred_element_type=F32)
    out = out + jnp.dot(a_bf16, mid, preferred_element_type=F32)
    return out + jnp.dot(a_bf16, hi, preferred_element_type=F32)


def _mm3(a, b):
    ah = a.astype(BF16)
    al = (a - ah.astype(F32)).astype(BF16)
    bh = b.astype(BF16)
    bl = (b - bh.astype(F32)).astype(BF16)
    out = jnp.dot(al, bh, preferred_element_type=F32)
    out = out + jnp.dot(ah, bl, preferred_element_type=F32)
    return out + jnp.dot(ah, bh, preferred_element_type=F32)


def _sigmoid(z):
    return 1.0 / (1.0 + jnp.exp(-z))


def _head_ones():
    r = jnp.where(lax.broadcasted_iota(jnp.int32, (LANES, LANES), 0) < HEAD_DIM, 0, 1)
    c = jnp.where(lax.broadcasted_iota(jnp.int32, (LANES, LANES), 1) < HEAD_DIM, 0, 1)
    return jnp.where(r == c, 1.0, 0.0).astype(BF16)


def _norm_proj_kernel(x_ref, g_ref, w_ref, o_ref, h_scr):
    @pl.when(pl.program_id(1) == 0)
    def _():
        x = x_ref[...]
        ms = jnp.mean(x * x, axis=-1, keepdims=True)
        h_scr[...] = (x * lax.rsqrt(ms + RMS_EPS) * g_ref[...]).astype(BF16)

    o_ref[...] = jnp.dot(h_scr[...], w_ref[...], preferred_element_type=F32)


def _norm_proj_rope_kernel(x_ref, g_ref, w_ref, cos_ref, sa_ref, sb_ref, o_ref, h_scr, *, rope_tiles):
    j = pl.program_id(1)

    @pl.when(j == 0)
    def _():
        x = x_ref[...]
        ms = jnp.mean(x * x, axis=-1, keepdims=True)
        h_scr[...] = (x * lax.rsqrt(ms + RMS_EPS) * g_ref[...]).astype(BF16)

    o = jnp.dot(h_scr[...], w_ref[...], preferred_element_type=F32)
    tn = o.shape[1]

    @pl.when(j < rope_tiles)
    def _():
        half = ROPE_DIM // 2
        cos, sa, sb = cos_ref[...], sa_ref[...], sb_ref[...]
        for c in range(tn // LANES):
            ob = o[:, c * LANES:(c + 1) * LANES]
            o_ref[:, c * LANES:(c + 1) * LANES] = (
                ob * cos + pltpu.roll(ob, half, 1) * sa + pltpu.roll(ob, LANES - half, 1) * sb)

    @pl.when(j >= rope_tiles)
    def _():
        o_ref[...] = o


def _norm_proj(x, g, w, tm, tn, rope=None):
    T, D = x.shape
    N = w.shape[1]
    grid = (T // tm, N // tn)
    x_spec = pl.BlockSpec((tm, D), lambda i, j: (i, 0))
    g_spec = pl.BlockSpec((1, D), lambda i, j: (0, 0))
    w_spec = pl.BlockSpec((D, tn), lambda i, j: (0, j))
    o_spec = pl.BlockSpec((tm, tn), lambda i, j: (i, j))
    if rope is None:
        return pl.pallas_call(
            _norm_proj_kernel, grid=grid, in_specs=[x_spec, g_spec, w_spec], out_specs=o_spec,
            out_shape=jax.ShapeDtypeStruct((T, N), F32),
            scratch_shapes=[pltpu.VMEM((tm, D), BF16)],
            compiler_params=_cparams("parallel", "arbitrary"), name="norm_proj")(x, g, w)
    cos, sa, sb, seq, rope_tiles = rope
    nseq = seq // tm
    t_spec = pl.BlockSpec((tm, LANES), lambda i, j: (i % nseq, 0))
    return pl.pallas_call(
        functools.partial(_norm_proj_rope_kernel, rope_tiles=rope_tiles), grid=grid,
        in_specs=[x_spec, g_spec, w_spec, t_spec, t_spec, t_spec], out_specs=o_spec,
        out_shape=jax.ShapeDtypeStruct((T, N), F32),
        scratch_shapes=[pltpu.VMEM((tm, D), BF16)],
        compiler_params=_cparams("parallel", "arbitrary"), name="norm_proj_rope")(x, g, w, cos, sa, sb)


def _attn_kernel(q0, q1, q2, k0, k1, k2, v0, v1, v2, o_ref, kbuf, vbuf, obuf, mbuf, lbuf):
    tile = pl.program_id(2)
    q_refs, k_refs, v_refs = (q0, q1, q2), (k0, k1, k2), (v0, v1, v2)
    L = ATTN_SPAN
    TL = ATTN_TILE

    lane = lax.broadcasted_iota(jnp.int32, (L, LANES), 1)
    head_a = lane < HEAD_DIM
    qi = lax.broadcasted_iota(jnp.int32, (L, 2 * L), 0)
    kj = lax.broadcasted_iota(jnp.int32, (L, 2 * L), 1)
    rel = L + qi - kj
    band = (rel >= 0) & (rel <= L)

    for g, (_, d) in enumerate(ATTN_GROUPS):
        @pl.when(tile == 0)
        def _():
            kbuf[g, 0:TL, :] = jnp.zeros((TL, LANES), F32)
            vbuf[g, 0:TL, :] = jnp.zeros((TL, LANES), F32)

        kbuf[g, TL:2 * TL, :] = k_refs[g][...]
        vbuf[g, TL:2 * TL, :] = v_refs[g][...]

        nblk = TL // L

        def block(idx, carry, g=g, d=d):
            r = idx % d
            n = idx // d
            start = r + (L * d) * n
            q = q_refs[g][pl.ds(start, L, stride=d), :] * (HEAD_DIM ** -0.5)
            kc = kbuf[g, pl.ds(TL + start - L * d, 2 * L, stride=d), :].astype(BF16)
            vc = vbuf[g, pl.ds(TL + start - L * d, 2 * L, stride=d), :].astype(BF16)
            first_key = jnp.where((tile > 0) | (n > 0), 0, L)
            valid = band & (kj >= first_key)
            o_acc = m_acc = l_acc = None
            for hd in range(2):
                sel = head_a if hd == 0 else jnp.logical_not(head_a)
                qh = jnp.where(sel, q, 0.0).astype(BF16)
                s = lax.dot_general(qh, kc, (((1,), (1,)), ((), ())), preferred_element_type=F32)
                s = jnp.where(valid, s, NEG_BIG)
                m = jnp.max(s, axis=1, keepdims=True)
                e = jnp.exp(s - m)
                l = jnp.sum(e, axis=1, keepdims=True)
                o = jnp.dot(e.astype(BF16), vc, preferred_element_type=F32)
                mb = jnp.broadcast_to(m, (L, LANES))
                lb = jnp.broadcast_to(l, (L, LANES))
                if hd == 0:
                    o_acc, m_acc, l_acc = o, mb, lb
                else:
                    o_acc = jnp.where(head_a, o_acc, o)
                    m_acc = jnp.where(head_a, m_acc, mb)
                    l_acc = jnp.where(head_a, l_acc, lb)
            obuf[g, pl.ds(start, L, stride=d), :] = o_acc
            mbuf[g, pl.ds(start, L, stride=d), :] = m_acc
            lbuf[g, pl.ds(start, L, stride=d), :] = l_acc
            return carry

        lax.fori_loop(0, nblk, block, 0)

        kbuf[g, 0:TL, :] = kbuf[g, TL:2 * TL, :]
        vbuf[g, 0:TL, :] = vbuf[g, TL:2 * TL, :]

    MR = 256

    def merge(i, carry):
        rows = pl.ds(pl.multiple_of(i * MR, MR), MR)
        m0, m1, m2 = mbuf[0, rows, :], mbuf[1, rows, :], mbuf[2, rows, :]
        mx = jnp.maximum(jnp.maximum(m0, m1), m2)
        w0, w1, w2 = jnp.exp(m0 - mx), jnp.exp(m1 - mx), jnp.exp(m2 - mx)
        num = w0 * obuf[0, rows, :] + w1 * obuf[1, rows, :] + w2 * obuf[2, rows, :]
        den = w0 * lbuf[0, rows, :] + w1 * lbuf[1, rows, :] + w2 * lbuf[2, rows, :]
        o_ref[rows, :] = (num / den).astype(o_ref.dtype)
        return carry

    lax.fori_loop(0, TL // MR, merge, 0)


def _attention(qkv, batch, seq):
    T = qkv.shape[0]
    TL = ATTN_TILE
    tiles = seq // TL
    npair = 2
    qcols = 3 * 4 * HEAD_DIM // LANES

    def spec(section, g):
        return pl.BlockSpec((TL, LANES),
                            lambda p, b, t, section=section, g=g: (b * tiles + t, section * qcols + 2 * g + p))

    in_specs = [spec(s, g) for s in range(3) for g in range(3)]
    out_spec = pl.BlockSpec((TL, LANES), lambda p, b, t: (b * tiles + t, p))
    return pl.pallas_call(
        _attn_kernel, grid=(npair, batch, tiles), in_specs=in_specs, out_specs=out_spec,
        out_shape=jax.ShapeDtypeStruct((T, npair * LANES), BF16),
        scratch_shapes=[pltpu.VMEM((3, 2 * TL, LANES), F32), pltpu.VMEM((3, 2 * TL, LANES), F32),
                        pltpu.VMEM((3, TL, LANES), F32), pltpu.VMEM((3, TL, LANES), F32),
                        pltpu.VMEM((3, TL, LANES), F32)],
        compiler_params=_cparams("arbitrary", "arbitrary", "arbitrary"), name="dilated_attn")(*([qkv] * 9))


def _stack_heads(x, head_a):
    return jnp.concatenate([jnp.where(head_a, x, 0.0), jnp.where(head_a, 0.0, x)], axis=0)


def _rwkv_chunk_kernel(r_ref, k_ref, v_ref, z_ref, rh_ref, kh_ref, vh_ref, zh_ref,
                       mur_ref, muk_ref, muv_ref, muz_ref, par_ref, wd_ref, wa_ref, wg_ref,
                       rp_ref, yp_ref, bonus_ref, gate_ref, p_ref, q_ref, *, seq):
    tm = r_ref.shape[0]
    C = CHUNK
    first = (pl.program_id(0) * tm) % seq == 0
    row = lax.broadcasted_iota(jnp.int32, (tm, 1), 0)

    def shifted(ref, halo_ref, mu_ref):
        p = ref[...]
        hal = jnp.where(first, 0.0, halo_ref[7:8, :])
        prev = jnp.where(row == 0, hal, pltpu.roll(p, 1, 0))
        return p + (prev - p) * mu_ref[...]

    z = shifted(z_ref, zh_ref, muz_ref)
    r = shifted(r_ref, rh_ref, mur_ref)
    k = shifted(k_ref, kh_ref, muk_ref)
    v = shifted(v_ref, vh_ref, muv_ref)

    w0, a0, k_k, k_a, r_k = (par_ref[i:i + 1, :] for i in range(5))
    zwa = z[:, 0:LANES]
    dpre = w0 + _mm(jnp.tanh(zwa), wd_ref[...])
    w_log = -(jnp.maximum(-dpre, 0.0) + jnp.log(1.0 + jnp.exp(-jnp.abs(dpre)))) - 0.5
    lw = -jnp.exp(w_log)
    a = _sigmoid(a0 + _mm(zwa, wa_ref[...]))
    gate_ref[...] = _mm(_sigmoid(z[:, LANES:3 * LANES]), wg_ref[...])

    ones_bd = _head_ones()
    kk = k * k_k
    kk = kk / jnp.maximum(jnp.sqrt(_mm_exact_rhs(kk * kk, ones_bd)), 1e-12)
    k2 = k * (1.0 + (a - 1.0) * k_a)
    bonus_ref[...] = _mm_exact_rhs(r * k2 * r_k, ones_bd) * v
    beta = kk * a

    ti = lax.broadcasted_iota(jnp.int32, (C, C), 0)
    si = lax.broadcasted_iota(jnp.int32, (C, C), 1)
    tril = jnp.where(si <= ti, 1.0, 0.0).astype(BF16)
    t2 = lax.broadcasted_iota(jnp.int32, (2 * C, 2 * C), 0)
    s2 = lax.broadcasted_iota(jnp.int32, (2 * C, 2 * C), 1)
    same = jnp.where(t2 < C, 0, 1) == jnp.where(s2 < C, 0, 1)
    strict = same & (s2 < t2)
    incl = same & (s2 <= t2)
    eye = jnp.where(t2 == s2, 1.0, 0.0)
    lane = lax.broadcasted_iota(jnp.int32, (C, LANES), 1)
    head_a = lane < HEAD_DIM

    for c in range(tm // C):
        sl = slice(c * C, (c + 1) * C)
        lw_c = lw[sl]
        cum = _mm_exact_lhs(tril, lw_c)
        tot = cum[C - 1:C, :]
        g_inc = jnp.exp(cum)
        g_prev = jnp.exp(cum - lw_c)
        g_inv = jnp.exp(-cum)
        g_end = jnp.exp(tot - cum)
        gam = jnp.exp(tot)

        xa = _stack_heads(-kk[sl] * g_prev, head_a)
        xr = _stack_heads(r[sl] * g_inc, head_a)
        yb = _stack_heads(beta[sl] * g_inv, head_a)
        yk = _stack_heads(k2[sl] * g_inv, head_a)
        bh = _stack_heads(beta[sl] * g_end, head_a)
        kh = _stack_heads(k2[sl] * g_end, head_a)
        vb = _stack_heads(v[sl], head_a)

        zz = _mm_nt(jnp.concatenate([xa, xr], axis=0), jnp.concatenate([yb, yk], axis=0))
        l_ab = jnp.where(strict, zz[:2 * C, :2 * C], 0.0)
        l_ak = jnp.where(strict, zz[:2 * C, 2 * C:], 0.0)
        m_rb = jnp.where(incl, zz[2 * C:, :2 * C], 0.0)
        m_rk = jnp.where(incl, zz[2 * C:, 2 * C:], 0.0)

        tinv = eye + l_ab
        lp = l_ab
        steps = 1
        while 2 * steps < C:
            lp = _mm(lp, lp)
            tinv = tinv + _mm(lp, tinv)
            steps *= 2

        wu = _mm(tinv, jnp.concatenate([xa, _mm(l_ak, vb)], axis=1))
        ry = _mm(m_rb, wu)
        rp = xr + ry[:, :LANES]
        yp = ry[:, LANES:] + _mm(m_rk, vb)
        rp_ref[sl, :] = rp[:C] + rp[C:]
        yp_ref[sl, :] = yp[:C] + yp[C:]
        pq = _mm_tn(bh, wu)
        p_ref[c, 0] = jnp.where(t2 == s2, gam, 0.0) + pq[:, :LANES]
        q_ref[c, 0] = pq[:, LANES:] + _mm_tn(kh, vb)


def _rwkv_chunks(prk, mu, params, wd, wa, wg, seq, tm):
    T = prk.shape[0]
    npair = 8
    zblk = 3 * LANES
    zcol = (3 * npair * LANES) // zblk
    hb = tm // 8

    def col(base):
        return pl.BlockSpec((tm, LANES), lambda i, j, base=base: (i, base + j))

    def halo(base):
        return pl.BlockSpec((8, LANES), lambda i, j, base=base: (jnp.maximum(i * hb - 1, 0), base + j))

    def mucol(base):
        return pl.BlockSpec((1, LANES), lambda i, j, base=base: (0, base + j))

    in_specs = [col(0), col(npair), col(2 * npair), pl.BlockSpec((tm, zblk), lambda i, j: (i, zcol)),
                halo(0), halo(npair), halo(2 * npair),
                pl.BlockSpec((8, zblk), lambda i, j: (jnp.maximum(i * hb - 1, 0), zcol)),
                mucol(0), mucol(npair), mucol(2 * npair), pl.BlockSpec((1, zblk), lambda i, j: (0, zcol)),
                pl.BlockSpec((8, LANES), lambda i, j: (0, j)),
                pl.BlockSpec((LANES, LANES), lambda i, j: (0, j)),
                pl.BlockSpec((LANES, LANES), lambda i, j: (0, j)),
                pl.BlockSpec((2 * LANES, LANES), lambda i, j: (0, j))]
    tok = pl.BlockSpec((tm, LANES), lambda i, j: (i, j))
    mat = pl.BlockSpec((tm // CHUNK, 1, LANES, LANES), lambda i, j: (i, j, 0, 0))
    tok_shape = jax.ShapeDtypeStruct((T, npair * LANES), F32)
    mat_shape = jax.ShapeDtypeStruct((T // CHUNK, npair, LANES, LANES), F32)
    return pl.pallas_call(
        functools.partial(_rwkv_chunk_kernel, seq=seq), grid=(T // tm, npair),
        in_specs=in_specs, out_specs=[tok, tok, tok, tok, mat, mat],
        out_shape=[tok_shape, tok_shape, tok_shape, tok_shape, mat_shape, mat_shape],
        compiler_params=_cparams("parallel", "parallel"), name="rwkv_chunks")(
            prk, prk, prk, prk, prk, prk, prk, prk, mu, mu, mu, mu, params, wd, wa, wg)


def _rwkv_scan_kernel(rp_ref, yp_ref, bonus_ref, gate_ref, p_ref, q_ref, lnw_ref, lnb_ref, o_ref, s_scr):
    npair = s_scr.shape[0]
    nchunk = p_ref.shape[0]
    C = CHUNK

    @pl.when(pl.program_id(1) == 0)
    def _():
        s_scr[...] = jnp.zeros(s_scr.shape, F32)

    ones_bd = _head_ones()
    inv_n = 1.0 / HEAD_DIM

    def chunk(c, carry):
        rows = pl.ds(pl.multiple_of(c * C, C), C)
        for j in range(npair):
            cols = slice(j * LANES, (j + 1) * LANES)
            s = s_scr[j]
            y = _mm(rp_ref[rows, cols], s) + yp_ref[rows, cols]
            s_scr[j] = _mm3(p_ref[c, j], s) + q_ref[c, j]
            mu = _mm_exact_rhs(y, ones_bd) * inv_n
            yc = y - mu
            var = _mm_exact_rhs(yc * yc, ones_bd) * inv_n
            yn = yc * lax.rsqrt(var + GN_EPS) * lnw_ref[:, cols] + lnb_ref[:, cols]
            o_ref[rows, cols] = ((yn + bonus_ref[rows, cols]) * gate_ref[rows, cols]).astype(o_ref.dtype)
        return carry

    lax.fori_loop(0, nchunk, chunk, 0)


def _rwkv_scan(rp, yp, bonus, gate, p, q, lnw, lnb, batch, seq, tm):
    T, W = rp.shape
    npair = W // LANES
    nblk = seq // tm
    tok = pl.BlockSpec((tm, W), lambda b, i: (b * nblk + i, 0))
    mat = pl.BlockSpec((tm // CHUNK, npair, LANES, LANES), lambda b, i: (b * nblk + i, 0, 0, 0))
    vec = pl.BlockSpec((1, W), lambda b, i: (0, 0))
    return pl.pallas_call(
        _rwkv_scan_kernel, grid=(batch, nblk),
        in_specs=[tok, tok, tok, tok, mat, mat, vec, vec], out_specs=tok,
        out_shape=jax.ShapeDtypeStruct((T, W), BF16),
        scratch_shapes=[pltpu.VMEM((npair, LANES, LANES), F32)],
        compiler_params=_cparams("arbitrary", "arbitrary"), name="rwkv_scan")(
            rp, yp, bonus, gate, p, q, lnw, lnb)


def _merge_kernel(x_ref, ya_ref, yr_ref, ga_ref, gr_ref, bga_ref, bgr_ref, wba_ref, wbr_ref, wo_ref, o_ref):
    ba = jnp.dot(ya_ref[...], wba_ref[...], preferred_element_type=F32)
    br = jnp.dot(yr_ref[...], wbr_ref[...], preferred_element_type=F32)
    merged = _sigmoid(ga_ref[...] + bga_ref[...]) * ba + _sigmoid(gr_ref[...] + bgr_ref[...]) * br
    o_ref[...] = x_ref[...] + jnp.dot(merged.astype(BF16), wo_ref[...], preferred_element_type=F32)


def _merge(x, ya, yr, gates, b_gate, wba, wbr, wo, tm):
    T, D = x.shape
    row = lambda w: pl.BlockSpec((tm, w), lambda i: (i, 0))
    full = lambda a: pl.BlockSpec(a.shape, lambda i: (0, 0))
    return pl.pallas_call(
        _merge_kernel, grid=(T // tm,),
        in_specs=[row(D), row(ya.shape[1]), row(D),
                  pl.BlockSpec((tm, D), lambda i: (i, 0)), pl.BlockSpec((tm, D), lambda i: (i, 1)),
                  pl.BlockSpec((1, D), lambda i: (0, 0)), pl.BlockSpec((1, D), lambda i: (0, 1)),
                  full(wba), full(wbr), full(wo)],
        out_specs=row(D), out_shape=jax.ShapeDtypeStruct((T, D), F32),
        compiler_params=_cparams("parallel"), name="merge_out")(
            x, ya, yr, gates, gates, b_gate, b_gate, wba, wbr, wo)


FFN_HALO = 16


def _ffn_kernel(x_ref, xh_ref, g_ref, wg_ref, wv_ref, cwg_ref, cwv_ref, cbg_ref, cbv_ref, wd_ref, gf_ref,
                o_ref, h_scr, acc_scr, *, seq):
    i, j = pl.program_id(0), pl.program_id(1)
    tm = x_ref.shape[0]
    H = FFN_HALO

    def norm(x):
        ms = jnp.mean(x * x, axis=-1, keepdims=True)
        return x * lax.rsqrt(ms + RMS_EPS) * g_ref[...]

    @pl.when(j == 0)
    def _():
        first = (i * tm) % seq == 0
        h_scr[0:H, :] = jnp.where(first, 0.0, norm(xh_ref[...])).astype(BF16)
        h_scr[H:, :] = norm(x_ref[...]).astype(BF16)
        acc_scr[...] = jnp.zeros(acc_scr.shape, F32)

    h = h_scr[...]

    def conv(u, cw_ref, cb_ref):
        c = u * cw_ref[2:3, :] + pltpu.roll(u, 1, 0) * cw_ref[1:2, :] + pltpu.roll(u, 2, 0) * cw_ref[0:1, :]
        return c[H:] + cb_ref[...]

    cg = conv(jnp.dot(h, wg_ref[...], preferred_element_type=F32), cwg_ref, cbg_ref)
    cv = conv(jnp.dot(h, wv_ref[...], preferred_element_type=F32), cwv_ref, cbv_ref)
    act = cg * _sigmoid(cg) * cv
    acc_scr[...] += jnp.dot(act.astype(BF16), wd_ref[...], preferred_element_type=F32)

    @pl.when(j == pl.num_programs(1) - 1)
    def _():
        x2 = x_ref[...] + acc_scr[...]
        ms = jnp.mean(x2 * x2, axis=-1, keepdims=True)
        o_ref[...] = x2 * lax.rsqrt(ms + RMS_EPS) * gf_ref[...]


def _conv_ffn(x, g, w_up, conv_w, conv_b, w_down, g_final, seq, tm, tf):
    T, D = x.shape
    dff = w_down.shape[0]
    nj = dff // tf
    hb = tm // FFN_HALO
    in_specs = [pl.BlockSpec((tm, D), lambda i, j: (i, 0)),
                pl.BlockSpec((FFN_HALO, D), lambda i, j: (jnp.maximum(i * hb - 1, 0), 0)),
                pl.BlockSpec((1, D), lambda i, j: (0, 0)),
                pl.BlockSpec((D, tf), lambda i, j: (0, j)),
                pl.BlockSpec((D, tf), lambda i, j: (0, nj + j)),
                pl.BlockSpec((3, tf), lambda i, j: (0, j)),
                pl.BlockSpec((3, tf), lambda i, j: (0, nj + j)),
                pl.BlockSpec((1, tf), lambda i, j: (0, j)),
                pl.BlockSpec((1, tf), lambda i, j: (0, nj + j)),
                pl.BlockSpec((tf, D), lambda i, j: (j, 0)),
                pl.BlockSpec((1, D), lambda i, j: (0, 0))]
    return pl.pallas_call(
        functools.partial(_ffn_kernel, seq=seq), grid=(T // tm, nj), in_specs=in_specs,
        out_specs=pl.BlockSpec((tm, D), lambda i, j: (i, 0)),
        out_shape=jax.ShapeDtypeStruct((T, D), F32),
        scratch_shapes=[pltpu.VMEM((tm + FFN_HALO, D), BF16), pltpu.VMEM((tm, D), F32)],
        compiler_params=_cparams("parallel", "arbitrary"), name="conv_ffn")(
            x, x, g, w_up, w_up, conv_w, conv_w, conv_b, conv_b, w_down, g_final)


def _rope_tables(seq):
    half = ROPE_DIM // 2
    inv_freq = jnp.power(ROPE_THETA, -jnp.arange(0, ROPE_DIM, 2, dtype=F32) / ROPE_DIM)
    ang = jnp.arange(seq, dtype=jnp.int32).astype(F32)[:, None] * inv_freq[None, :]
    cos, sin = jnp.cos(ang), jnp.sin(ang)
    ones = jnp.ones((seq, HEAD_DIM - ROPE_DIM), F32)
    zeros = jnp.zeros((seq, HEAD_DIM - ROPE_DIM), F32)
    zh = jnp.zeros((seq, half), F32)
    c = jnp.concatenate([cos, cos, ones], axis=1)
    sa = jnp.concatenate([zh, sin, zeros], axis=1)
    sb = jnp.concatenate([-sin, zh, zeros], axis=1)
    tile2 = lambda t: jnp.concatenate([t, t], axis=1)
    return tile2(c), tile2(sa), tile2(sb)


def kernel(x, norm_mix_g, w_in, b_gate, mu_shift, w0, w_decay_up, a0, w_a_up, w_g_up, k_k, k_a, r_k,
           ln_x_w, ln_x_b, w_branch_attn, w_branch_rwkv, w_out, norm_ffn_g, w_ffn_up, conv_w, conv_b,
           w_ffn_down, norm_final_g):
    B, S, D = x.shape
    T = B * S
    depth = norm_mix_g.shape[0]
    attn_w = 3 * 4 * HEAD_DIM
    rw = D
    n_decay, n_iclr, n_gate = w_decay_up.shape[1], w_a_up.shape[1], w_g_up.shape[1]
    lora_w = n_decay + n_iclr + n_gate
    lora_pad = 3 * LANES
    c_rwkv = 3 * attn_w
    c_lora = c_rwkv + 3 * rw
    c_gate = c_lora + lora_w

    assert depth == 1 and S % ATTN_TILE == 0 and D == 16 * HEAD_DIM
    xt = x.reshape(T, D)
    cos, sa, sb = _rope_tables(S)
    for l in range(depth):
        wi = w_in[l]
        w_qkv = wi[:, :c_rwkv].astype(BF16)
        w_rkvz = jnp.concatenate(
            [wi[:, c_rwkv:c_gate], jnp.zeros((D, lora_pad - lora_w), F32)], axis=1).astype(BF16)
        w_gt = wi[:, c_gate:].astype(BF16)
        gmix = norm_mix_g[l].reshape(1, D)

        qkv = _norm_proj(xt, gmix, w_qkv, 512, attn_w, rope=(cos, sa, sb, S, 2))
        prk = _norm_proj(xt, gmix, w_rkvz, 512, (3 * rw + lora_pad) // 3)
        gates = _norm_proj(xt, gmix, w_gt, 512, 1024)

        y_attn = _attention(qkv, B, S)

        mu = jnp.concatenate([mu_shift[l], jnp.zeros((lora_pad - lora_w,), F32)]).reshape(1, -1)
        params = jnp.concatenate(
            [jnp.stack([w0[l], a0[l], k_k[l], k_a[l], r_k[l].reshape(-1)]), jnp.zeros((3, rw), F32)], axis=0)
        wd = jnp.concatenate([w_decay_up[l], jnp.zeros((LANES - n_decay, rw), F32)], axis=0).astype(BF16)
        wa = jnp.concatenate([jnp.zeros((n_decay, rw), F32), w_a_up[l],
                              jnp.zeros((LANES - n_decay - n_iclr, rw), F32)], axis=0).astype(BF16)
        wg = jnp.concatenate([w_g_up[l], jnp.zeros((2 * LANES - n_gate, rw), F32)], axis=0).astype(BF16)
        rp, yp, bonus, gate, pm, qm = _rwkv_chunks(prk, mu, params, wd, wa, wg, S, 256)
        y_rwkv = _rwkv_scan(rp, yp, bonus, gate, pm, qm, ln_x_w[l].reshape(1, -1), ln_x_b[l].reshape(1, -1),
                            B, S, 512)

        xt = _merge(xt, y_attn, y_rwkv, gates, b_gate[l].reshape(1, -1), w_branch_attn[l].astype(BF16),
                    w_branch_rwkv[l].astype(BF16), w_out[l].astype(BF16), 512)

        xt = _conv_ffn(xt, norm_ffn_g[l].reshape(1, D), w_ffn_up[l].astype(BF16), conv_w[l].reshape(3, -1),
                       conv_b[l].reshape(1, -1), w_ffn_down[l].astype(BF16), norm_final_g.reshape(1, D),
                       S, 512, 256)
    return xt.reshape(B, S, D)
```

```python
import functools

import jax
import jax.numpy as jnp
from jax import lax
from jax.experimental import pallas as pl
from jax.experimental.pallas import tpu as pltpu

F32 = jnp.float32
BF16 = jnp.bfloat16

HEAD_DIM = 64
LANES = 128
ATTN_GROUPS = ((128, 1), (512, 4), (2048, 16))
ATTN_SPAN = 128
ATTN_TILE = 2048
ATTN_UNROLL = 4
ROPE_THETA = 500000.0
ROPE_DIM = HEAD_DIM // 4
RMS_EPS = 1e-6
GN_EPS = 64e-5
CHUNK = 64
NEG_BIG = -1e30
VMEM_LIMIT = 56 * 1024 * 1024


def _cparams(*sem):
    return pltpu.CompilerParams(dimension_semantics=sem, vmem_limit_bytes=VMEM_LIMIT)


def _mm(a, b):
    return jnp.dot(a.astype(BF16), b.astype(BF16), preferred_element_type=F32)


def _mm_nt(a, b):
    return lax.dot_general(a.astype(BF16), b.astype(BF16), (((1,), (1,)), ((), ())),
                           preferred_element_type=F32)


def _mm_tn(a, b):
    return lax.dot_general(a.astype(BF16), b.astype(BF16), (((0,), (0,)), ((), ())),
                           preferred_element_type=F32)


def _split3(a):
    hi = a.astype(BF16)
    r1 = a - hi.astype(F32)
    mid = r1.astype(BF16)
    lo = (r1 - mid.astype(F32)).astype(BF16)
    return hi, mid, lo


def _mm_exact_rhs(a, b_bf16):
    hi, mid, lo = _split3(a)
    out = jnp.dot(lo, b_bf16, preferred_element_type=F32)
    out = out + jnp.dot(mid, b_bf16, preferred_element_type=F32)
    return out + jnp.dot(hi, b_bf16, preferred_element_type=F32)


def _mm_exact_lhs(a_bf16, b):
    hi, mid, lo = _split3(b)
    out = jnp.dot(a_bf16, lo, preferred_element_type=F32)
    out = out + jnp.dot(a_bf16, mid, preferred_element_type=F32)
    return out + jnp.dot(a_bf16, hi, preferred_element_type=F32)


def _mm3(a, b):
    ah = a.astype(BF16)
    al = (a - ah.astype(F32)).astype(BF16)
    bh = b.astype(BF16)
    bl = (b - bh.astype(F32)).astype(BF16)
    out = jnp.dot(al, bh, preferred_element_type=F32)
    out = out + jnp.dot(ah, bl, preferred_element_type=F32)
    return out + jnp.dot(ah, bh, preferred_element_type=F32)


def _sigmoid(z):
    return 1.0 / (1.0 + jnp.exp(-z))


def _head_ones():
    r = jnp.where(lax.broadcasted_iota(jnp.int32, (LANES, LANES), 0) < HEAD_DIM, 0, 1)
    c = jnp.where(lax.broadcasted_iota(jnp.int32, (LANES, LANES), 1) < HEAD_DIM, 0, 1)
    return jnp.where(r == c, 1.0, 0.0).astype(BF16)


def _norm_proj_kernel(x_ref, g_ref, w_ref, o_ref, h_scr):
    @pl.when(pl.program_id(1) == 0)
    def _():
        x = x_ref[...]
        ms = jnp.mean(x * x, axis=-1, keepdims=True)
        h_scr[...] = (x * lax.rsqrt(ms + RMS_EPS) * g_ref[...]).astype(BF16)

    o_ref[...] = jnp.dot(h_scr[...], w_ref[...], preferred_element_type=F32)


def _norm_proj_rope_kernel(x_ref, g_ref, w_ref, cos_ref, sa_ref, sb_ref, o_ref, h_scr, *, rope_tiles):
    j = pl.program_id(1)

    @pl.when(j == 0)
    def _():
        x = x_ref[...]
        ms = jnp.mean(x * x, axis=-1, keepdims=True)
        h_scr[...] = (x * lax.rsqrt(ms + RMS_EPS) * g_ref[...]).astype(BF16)

    o = jnp.dot(h_scr[...], w_ref[...], preferred_element_type=F32)
    tn = o.shape[1]

    @pl.when(j < rope_tiles)
    def _():
        half = ROPE_DIM // 2
        cos, sa, sb = cos_ref[...], sa_ref[...], sb_ref[...]
        for c in range(tn // LANES):
            ob = o[:, c * LANES:(c + 1) * LANES]
            o_ref[:, c * LANES:(c + 1) * LANES] = (
                ob * cos + pltpu.roll(ob, half, 1) * sa + pltpu.roll(ob, LANES - half, 1) * sb)

    @pl.when(j >= rope_tiles)
    def _():
        o_ref[...] = o


def _norm_proj(x, g, w, tm, tn, rope=None):
    T, D = x.shape
    N = w.shape[1]
    grid = (T // tm, N // tn)
    x_spec = pl.BlockSpec((tm, D), lambda i, j: (i, 0))
    g_spec = pl.BlockSpec((1, D), lambda i, j: (0, 0))
    w_spec = pl.BlockSpec((D, tn), lambda i, j: (0, j))
    o_spec = pl.BlockSpec((tm, tn), lambda i, j: (i, j))
    if rope is None:
        return pl.pallas_call(
            _norm_proj_kernel, grid=grid, in_specs=[x_spec, g_spec, w_spec], out_specs=o_spec,
            out_shape=jax.ShapeDtypeStruct((T, N), F32),
            scratch_shapes=[pltpu.VMEM((tm, D), BF16)],
            compiler_params=_cparams("parallel", "arbitrary"), name="norm_proj")(x, g, w)
    cos, sa, sb, seq, rope_tiles = rope
    nseq = seq // tm
    t_spec = pl.BlockSpec((tm, LANES), lambda i, j: (i % nseq, 0))
    return pl.pallas_call(
        functools.partial(_norm_proj_rope_kernel, rope_tiles=rope_tiles), grid=grid,
        in_specs=[x_spec, g_spec, w_spec, t_spec, t_spec, t_spec], out_specs=o_spec,
        out_shape=jax.ShapeDtypeStruct((T, N), F32),
        scratch_shapes=[pltpu.VMEM((tm, D), BF16)],
        compiler_params=_cparams("parallel", "arbitrary"), name="norm_proj_rope")(x, g, w, cos, sa, sb)


def _attn_kernel(q0, q1, q2, k0, k1, k2, v0, v1, v2, kp0, kp1, kp2, vp0, vp1, vp2, o_ref, obuf, mbuf, lbuf):
    tile = pl.program_id(2)
    q_refs, k_refs, v_refs = (q0, q1, q2), (k0, k1, k2), (v0, v1, v2)
    kp_refs, vp_refs = (kp0, kp1, kp2), (vp0, vp1, vp2)
    L = ATTN_SPAN
    TL = ATTN_TILE
    U = ATTN_UNROLL

    lane = lax.broadcasted_iota(jnp.int32, (L, LANES), 1)
    head_a = lane < HEAD_DIM
    qi = lax.broadcasted_iota(jnp.int32, (L, 2 * L), 0)
    kj = lax.broadcasted_iota(jnp.int32, (L, 2 * L), 1)
    rel = L + qi - kj
    band = (rel >= 0) & (rel <= L)
    band_first = band & (kj >= jnp.where(tile > 0, 0, L))

    def run_blocks(g, d, blocks):
        qs, kcs, vcs, valids = [], [], [], []
        for start, from_prev in blocks:
            rows = pl.ds(start, L, stride=d)
            qs.append(q_refs[g][rows, :] * (HEAD_DIM ** -0.5))
            if from_prev:
                k1, v1 = kp_refs[g][rows, :], vp_refs[g][rows, :]
            else:
                back = pl.ds(start - L * d, L, stride=d)
                k1, v1 = k_refs[g][back, :], v_refs[g][back, :]
            kcs.append(jnp.concatenate([k1, k_refs[g][rows, :]], axis=0).astype(BF16))
            vcs.append(jnp.concatenate([v1, v_refs[g][rows, :]], axis=0).astype(BF16))
            valids.append(band_first if from_prev else band)
        nb = range(len(blocks))
        qh = [[jnp.where(head_a, qs[b], 0.0).astype(BF16), jnp.where(head_a, 0.0, qs[b]).astype(BF16)] for b in nb]
        s = [[lax.dot_general(qh[b][h], kcs[b], (((1,), (1,)), ((), ())), preferred_element_type=F32)
              for h in range(2)] for b in nb]
        s = [[jnp.where(valids[b], s[b][h], NEG_BIG) for h in range(2)] for b in nb]
        m = [[jnp.max(s[b][h], axis=1, keepdims=True) for h in range(2)] for b in nb]
        e = [[jnp.exp(s[b][h] - m[b][h]) for h in range(2)] for b in nb]
        l = [[jnp.sum(e[b][h], axis=1, keepdims=True) for h in range(2)] for b in nb]
        o = [[jnp.dot(e[b][h].astype(BF16), vcs[b], preferred_element_type=F32) for h in range(2)] for b in nb]
        for b, (start, _) in enumerate(blocks):
            rows = pl.ds(start, L, stride=d)
            obuf[g, rows, :] = jnp.where(head_a, o[b][0], o[b][1])
            mbuf[g, rows, :] = jnp.where(head_a, jnp.broadcast_to(m[b][0], (L, LANES)),
                                         jnp.broadcast_to(m[b][1], (L, LANES)))
            lbuf[g, rows, :] = jnp.where(head_a, jnp.broadcast_to(l[b][0], (L, LANES)),
                                         jnp.broadcast_to(l[b][1], (L, LANES)))

    nblk = TL // L
    for g, (_, d) in enumerate(ATTN_GROUPS):
        def cur_body(it, carry, g=g, d=d):
            idxs = [it * U + u for u in range(U)]
            run_blocks(g, d, [((idx % d) + (L * d) * (idx // d), False) for idx in idxs])
            return carry

        if d >= U:
            def prev_body(it, carry, g=g, d=d):
                run_blocks(g, d, [(it * U + u, True) for u in range(U)])
                return carry

            lax.fori_loop(0, d // U, prev_body, 0)
            if d < nblk:
                lax.fori_loop(d // U, nblk // U, cur_body, 0)
        else:
            run_blocks(g, d, [((idx % d) + (L * d) * (idx // d), idx // d == 0) for idx in range(U)])
            lax.fori_loop(1, nblk // U, cur_body, 0)

    MR = 256

    def merge(i, carry):
        rows = pl.ds(pl.multiple_of(i * MR, MR), MR)
        m0, m1, m2 = mbuf[0, rows, :], mbuf[1, rows, :], mbuf[2, rows, :]
        mx = jnp.maximum(jnp.maximum(m0, m1), m2)
        w0, w1, w2 = jnp.exp(m0 - mx), jnp.exp(m1 - mx), jnp.exp(m2 - mx)
        num = w0 * obuf[0, rows, :] + w1 * obuf[1, rows, :] + w2 * obuf[2, rows, :]
        den = w0 * lbuf[0, rows, :] + w1 * lbuf[1, rows, :] + w2 * lbuf[2, rows, :]
        o_ref[rows, :] = (num / den).astype(o_ref.dtype)
        return carry

    lax.fori_loop(0, TL // MR, merge, 0)


def _attention(qkv, batch, seq):
    T = qkv.shape[0]
    TL = ATTN_TILE
    tiles = seq // TL
    npair = 2
    qcols = 3 * 4 * HEAD_DIM // LANES

    def spec(section, g):
        return pl.BlockSpec((TL, LANES),
                            lambda p, b, t, section=section, g=g: (b * tiles + t, section * qcols + 2 * g + p))

    def prev_spec(section, g):
        rows = ATTN_SPAN * ATTN_GROUPS[g][1]
        per_tile = TL // rows
        return pl.BlockSpec(
            (rows, LANES),
            lambda p, b, t, section=section, g=g: (jnp.maximum((b * tiles + t) * per_tile - 1, 0),
                                                   section * qcols + 2 * g + p))

    in_specs = ([spec(s, g) for s in range(3) for g in range(3)]
                + [prev_spec(s, g) for s in (1, 2) for g in range(3)])
    out_spec = pl.BlockSpec((TL, LANES), lambda p, b, t: (b * tiles + t, p))
    return pl.pallas_call(
        _attn_kernel, grid=(npair, batch, tiles), in_specs=in_specs, out_specs=out_spec,
        out_shape=jax.ShapeDtypeStruct((T, npair * LANES), BF16),
        scratch_shapes=[pltpu.VMEM((3, TL, LANES), F32), pltpu.VMEM((3, TL, LANES), F32),
                        pltpu.VMEM((3, TL, LANES), F32)],
        compiler_params=_cparams("parallel", "parallel", "parallel"), name="dilated_attn")(*([qkv] * 15))


def _stack_heads(x, head_a):
    return jnp.concatenate([jnp.where(head_a, x, 0.0), jnp.where(head_a, 0.0, x)], axis=0)


def _rwkv_chunk_kernel(r_ref, k_ref, v_ref, z_ref, rh_ref, kh_ref, vh_ref, zh_ref,
                       mur_ref, muk_ref, muv_ref, muz_ref, par_ref, wd_ref, wa_ref, wg_ref,
                       rp_ref, yp_ref, bonus_ref, gate_ref, p_ref, q_ref, *, seq):
    tm = r_ref.shape[0]
    C = CHUNK
    first = (pl.program_id(0) * tm) % seq == 0
    row = lax.broadcasted_iota(jnp.int32, (tm, 1), 0)

    def shifted(ref, halo_ref, mu_ref):
        p = ref[...]
        hal = jnp.where(first, 0.0, halo_ref[7:8, :])
        prev = jnp.where(row == 0, hal, pltpu.roll(p, 1, 0))
        return p + (prev - p) * mu_ref[...]

    z = shifted(z_ref, zh_ref, muz_ref)
    r = shifted(r_ref, rh_ref, mur_ref)
    k = shifted(k_ref, kh_ref, muk_ref)
    v = shifted(v_ref, vh_ref, muv_ref)

    w0, a0, k_k, k_a, r_k = (par_ref[i:i + 1, :] for i in range(5))
    zwa = z[:, 0:LANES]
    dpre = w0 + _mm(jnp.tanh(zwa), wd_ref[...])
    w_log = -(jnp.maximum(-dpre, 0.0) + jnp.log(1.0 + jnp.exp(-jnp.abs(dpre)))) - 0.5
    lw = -jnp.exp(w_log)
    a = _sigmoid(a0 + _mm(zwa, wa_ref[...]))
    gate_ref[...] = _mm(_sigmoid(z[:, LANES:3 * LANES]), wg_ref[...])

    ones_bd = _head_ones()
    kk = k * k_k
    kk = kk / jnp.maximum(jnp.sqrt(_mm_exact_rhs(kk * kk, ones_bd)), 1e-12)
    k2 = k * (1.0 + (a - 1.0) * k_a)
    bonus_ref[...] = _mm_exact_rhs(r * k2 * r_k, ones_bd) * v
    beta = kk * a

    shift = C.bit_length() - 1
    ti = lax.broadcasted_iota(jnp.int32, (tm, tm), 0)
    si = lax.broadcasted_iota(jnp.int32, (tm, tm), 1)
    same_chunk = lax.shift_right_logical(ti, shift) == lax.shift_right_logical(si, shift)
    tril = jnp.where(same_chunk & (si <= ti), 1.0, 0.0).astype(BF16)
    cum = _mm_exact_lhs(tril, lw)
    t2 = lax.broadcasted_iota(jnp.int32, (2 * C, 2 * C), 0)
    s2 = lax.broadcasted_iota(jnp.int32, (2 * C, 2 * C), 1)
    same = jnp.where(t2 < C, 0, 1) == jnp.where(s2 < C, 0, 1)
    strict = same & (s2 < t2)
    incl = same & (s2 <= t2)
    eye = jnp.where(t2 == s2, 1.0, 0.0)
    lane = lax.broadcasted_iota(jnp.int32, (C, LANES), 1)
    head_a = lane < HEAD_DIM

    chunks = range(tm // C)
    xa, xr, yy, lhs_t, vb, gam = [], [], [], [], [], []
    for c in chunks:
        sl = slice(c * C, (c + 1) * C)
        cum_c, lw_c = cum[sl], lw[sl]
        tot = cum[(c + 1) * C - 1:(c + 1) * C, :]
        g_inc = jnp.exp(cum_c)
        g_prev = jnp.exp(cum_c - lw_c)
        g_inv = jnp.exp(-cum_c)
        g_end = jnp.exp(tot - cum_c)
        gam.append(jnp.exp(tot))
        xa.append(_stack_heads(-kk[sl] * g_prev, head_a))
        xr.append(_stack_heads(r[sl] * g_inc, head_a))
        yy.append(jnp.concatenate([_stack_heads(beta[sl] * g_inv, head_a),
                                   _stack_heads(k2[sl] * g_inv, head_a)], axis=0))
        bh = _stack_heads(beta[sl] * g_end, head_a)
        kh = _stack_heads(k2[sl] * g_end, head_a)
        lhs_t.append(jnp.concatenate([bh.T, kh.T], axis=1))
        vb.append(_stack_heads(v[sl], head_a))

    zz = [_mm_nt(jnp.concatenate([xa[c], xr[c]], axis=0), yy[c]) for c in chunks]
    l_ab = [jnp.where(strict, zz[c][:2 * C, :2 * C], 0.0) for c in chunks]
    l_ak = [jnp.where(strict, zz[c][:2 * C, 2 * C:], 0.0) for c in chunks]
    m_rbk = [jnp.concatenate([jnp.where(incl, zz[c][2 * C:, :2 * C], 0.0),
                              jnp.where(incl, zz[c][2 * C:, 2 * C:], 0.0)], axis=1) for c in chunks]
    lakv = [_mm(l_ak[c], vb[c]) for c in chunks]

    lp = [_mm(l_ab[c], l_ab[c]) for c in chunks]
    tinv = [eye + l_ab[c] for c in chunks]
    power = 2
    while 2 * power < C:
        both = [_mm(lp[c], jnp.concatenate([lp[c], tinv[c]], axis=1)) for c in chunks]
        tinv = [tinv[c] + both[c][:, 2 * C:] for c in chunks]
        lp = [both[c][:, :2 * C] for c in chunks]
        power *= 2
    tinv = [tinv[c] + _mm(lp[c], tinv[c]) for c in chunks]

    wu = [_mm(tinv[c], jnp.concatenate([xa[c], lakv[c]], axis=1)) for c in chunks]
    zero = jnp.zeros((2 * C, LANES), F32)
    for c in chunks:
        sl = slice(c * C, (c + 1) * C)
        rhs = jnp.concatenate([wu[c], jnp.concatenate([zero, vb[c]], axis=1)], axis=0)
        out = _mm(jnp.concatenate([m_rbk[c], lhs_t[c]], axis=0), rhs)
        rp = xr[c] + out[:2 * C, :LANES]
        yp = out[:2 * C, LANES:]
        rp_ref[sl, :] = rp[:C] + rp[C:]
        yp_ref[sl, :] = yp[:C] + yp[C:]
        p_ref[c, 0] = jnp.where(t2 == s2, gam[c], 0.0) + out[2 * C:, :LANES]
        q_ref[c, 0] = out[2 * C:, LANES:]


def _rwkv_chunks(prk, mu, params, wd, wa, wg, seq, tm):
    T = prk.shape[0]
    npair = 8
    zblk = 3 * LANES
    zcol = (3 * npair * LANES) // zblk
    hb = tm // 8

    def col(base):
        return pl.BlockSpec((tm, LANES), lambda i, j, base=base: (i, base + j))

    def halo(base):
        return pl.BlockSpec((8, LANES), lambda i, j, base=base: (jnp.maximum(i * hb - 1, 0), base + j))

    def mucol(base):
        return pl.BlockSpec((1, LANES), lambda i, j, base=base: (0, base + j))

    in_specs = [col(0), col(npair), col(2 * npair), pl.BlockSpec((tm, zblk), lambda i, j: (i, zcol)),
                halo(0), halo(npair), halo(2 * npair),
                pl.BlockSpec((8, zblk), lambda i, j: (jnp.maximum(i * hb - 1, 0), zcol)),
                mucol(0), mucol(npair), mucol(2 * npair), pl.BlockSpec((1, zblk), lambda i, j: (0, zcol)),
                pl.BlockSpec((8, LANES), lambda i, j: (0, j)),
                pl.BlockSpec((LANES, LANES), lambda i, j: (0, j)),
                pl.BlockSpec((LANES, LANES), lambda i, j: (0, j)),
                pl.BlockSpec((2 * LANES, LANES), lambda i, j: (0, j))]
    tok = pl.BlockSpec((tm, LANES), lambda i, j: (i, j))
    mat = pl.BlockSpec((tm // CHUNK, 1, LANES, LANES), lambda i, j: (i, j, 0, 0))
    tok_shape = jax.ShapeDtypeStruct((T, npair * LANES), F32)
    mat_shape = jax.ShapeDtypeStruct((T // CHUNK, npair, LANES, LANES), F32)
    return pl.pallas_call(
        functools.partial(_rwkv_chunk_kernel, seq=seq), grid=(T // tm, npair),
        in_specs=in_specs, out_specs=[tok, tok, tok, tok, mat, mat],
        out_shape=[tok_shape, tok_shape, tok_shape, tok_shape, mat_shape, mat_shape],
        compiler_params=_cparams("parallel", "parallel"), name="rwkv_chunks")(
            prk, prk, prk, prk, prk, prk, prk, prk, mu, mu, mu, mu, params, wd, wa, wg)


def _rwkv_scan_kernel(rp_ref, yp_ref, p_ref, q_ref, y_ref, s_scr):
    nseq, npair = s_scr.shape[0], s_scr.shape[1]
    nchunk = p_ref.shape[1]
    C = CHUNK

    @pl.when(pl.program_id(0) == 0)
    def _():
        s_scr[...] = jnp.zeros(s_scr.shape, F32)

    def chunk(c, carry):
        rows = pl.ds(pl.multiple_of(c * C, C), C)
        units = [(b, j) for b in range(nseq) for j in range(npair)]
        s = [s_scr[b, j] for b, j in units]
        for (b, j), s_u in zip(units, s):
            cols = slice(j * LANES, (j + 1) * LANES)
            y_ref[b, rows, cols] = _mm(rp_ref[b, rows, cols], s_u) + yp_ref[b, rows, cols]
        for (b, j), s_u in zip(units, s):
            s_scr[b, j] = _mm3(p_ref[b, c, j], s_u) + q_ref[b, c, j]
        return carry

    lax.fori_loop(0, nchunk, chunk, 0)


def _rwkv_scan(rp, yp, p, q, batch, seq, tm):
    T, W = rp.shape
    npair = W // LANES
    rp, yp = rp.reshape(batch, seq, W), yp.reshape(batch, seq, W)
    p = p.reshape(batch, seq // CHUNK, npair, LANES, LANES)
    q = q.reshape(batch, seq // CHUNK, npair, LANES, LANES)
    tok = pl.BlockSpec((batch, tm, W), lambda i: (0, i, 0))
    mat = pl.BlockSpec((batch, tm // CHUNK, npair, LANES, LANES), lambda i: (0, i, 0, 0, 0))
    y = pl.pallas_call(
        _rwkv_scan_kernel, grid=(seq // tm,),
        in_specs=[tok, tok, mat, mat], out_specs=tok,
        out_shape=jax.ShapeDtypeStruct((batch, seq, W), F32),
        scratch_shapes=[pltpu.VMEM((batch, npair, LANES, LANES), F32)],
        compiler_params=_cparams("arbitrary"), name="rwkv_scan")(rp, yp, p, q)
    return y.reshape(T, W)


def _merge_kernel(x_ref, ya_ref, y_ref, bonus_ref, rg_ref, lnw_ref, lnb_ref, ga_ref, gr_ref, bga_ref, bgr_ref,
                  wba_ref, wbr_ref, wo_ref, o_ref, yr_scr):
    ones_bd = _head_ones()
    inv_n = 1.0 / HEAD_DIM
    for j in range(y_ref.shape[1] // LANES):
        cols = slice(j * LANES, (j + 1) * LANES)
        y = y_ref[:, cols]
        yc = y - _mm_exact_rhs(y, ones_bd) * inv_n
        var = _mm_exact_rhs(yc * yc, ones_bd) * inv_n
        yn = yc * lax.rsqrt(var + GN_EPS) * lnw_ref[:, cols] + lnb_ref[:, cols]
        yr_scr[:, cols] = ((yn + bonus_ref[:, cols]) * rg_ref[:, cols]).astype(BF16)

    ba = jnp.dot(ya_ref[...], wba_ref[...], preferred_element_type=F32)
    br = jnp.dot(yr_scr[...], wbr_ref[...], preferred_element_type=F32)
    merged = _sigmoid(ga_ref[...] + bga_ref[...]) * ba + _sigmoid(gr_ref[...] + bgr_ref[...]) * br
    o_ref[...] = x_ref[...] + jnp.dot(merged.astype(BF16), wo_ref[...], preferred_element_type=F32)


def _merge(x, ya, y, bonus, rgate, lnw, lnb, gates, b_gate, wba, wbr, wo, tm):
    T, D = x.shape
    row = lambda w: pl.BlockSpec((tm, w), lambda i: (i, 0))
    full = lambda a: pl.BlockSpec(a.shape, lambda i: (0, 0))
    return pl.pallas_call(
        _merge_kernel, grid=(T // tm,),
        in_specs=[row(D), row(ya.shape[1]), row(D), row(D), row(D), full(lnw), full(lnb),
                  pl.BlockSpec((tm, D), lambda i: (i, 0)), pl.BlockSpec((tm, D), lambda i: (i, 1)),
                  pl.BlockSpec((1, D), lambda i: (0, 0)), pl.BlockSpec((1, D), lambda i: (0, 1)),
                  full(wba), full(wbr), full(wo)],
        out_specs=row(D), out_shape=jax.ShapeDtypeStruct((T, D), F32),
        scratch_shapes=[pltpu.VMEM((tm, D), BF16)],
        compiler_params=_cparams("parallel"), name="merge_out")(
            x, ya, y, bonus, rgate, lnw, lnb, gates, gates, b_gate, b_gate, wba, wbr, wo)


FFN_HALO = 16


def _ffn_kernel(x_ref, xh_ref, g_ref, wg_ref, wv_ref, cwg_ref, cwv_ref, cbg_ref, cbv_ref, wd_ref, gf_ref,
                o_ref, h_scr, acc_scr, *, seq):
    i, j = pl.program_id(0), pl.program_id(1)
    tm = x_ref.shape[0]
    H = FFN_HALO

    def norm(x):
        ms = jnp.mean(x * x, axis=-1, keepdims=True)
        return x * lax.rsqrt(ms + RMS_EPS) * g_ref[...]

    @pl.when(j == 0)
    def _():
        first = (i * tm) % seq == 0
        h_scr[0:H, :] = jnp.where(first, 0.0, norm(xh_ref[...])).astype(BF16)
        h_scr[H:, :] = norm(x_ref[...]).astype(BF16)
        acc_scr[...] = jnp.zeros(acc_scr.shape, F32)

    h = h_scr[...]

    def conv(u, cw_ref, cb_ref):
        c = u * cw_ref[2:3, :] + pltpu.roll(u, 1, 0) * cw_ref[1:2, :] + pltpu.roll(u, 2, 0) * cw_ref[0:1, :]
        return c[H:] + cb_ref[...]

    cg = conv(jnp.dot(h, wg_ref[...], preferred_element_type=F32), cwg_ref, cbg_ref)
    cv = conv(jnp.dot(h, wv_ref[...], preferred_element_type=F32), cwv_ref, cbv_ref)
    act = cg * _sigmoid(cg) * cv
    acc_scr[...] += jnp.dot(act.astype(BF16), wd_ref[...], preferred_element_type=F32)

    @pl.when(j == pl.num_programs(1) - 1)
    def _():
        x2 = x_ref[...] + acc_scr[...]
        ms = jnp.mean(x2 * x2, axis=-1, keepdims=True)
        o_ref[...] = x2 * lax.rsqrt(ms + RMS_EPS) * gf_ref[...]


def _conv_ffn(x, g, w_up, conv_w, conv_b, w_down, g_final, seq, tm, tf):
    T, D = x.shape
    dff = w_down.shape[0]
    nj = dff // tf
    hb = tm // FFN_HALO
    in_specs = [pl.BlockSpec((tm, D), lambda i, j: (i, 0)),
                pl.BlockSpec((FFN_HALO, D), lambda i, j: (jnp.maximum(i * hb - 1, 0), 0)),
                pl.BlockSpec((1, D), lambda i, j: (0, 0)),
                pl.BlockSpec((D, tf), lambda i, j: (0, j)),
                pl.BlockSpec((D, tf), lambda i, j: (0, nj + j)),
                pl.BlockSpec((3, tf), lambda i, j: (0, j)),
                pl.BlockSpec((3, tf), lambda i, j: (0, nj + j)),
                pl.BlockSpec((1, tf), lambda i, j: (0, j)),
                pl.BlockSpec((1, tf), lambda i, j: (0, nj + j)),
                pl.BlockSpec((tf, D), lambda i, j: (j, 0)),
                pl.BlockSpec((1, D), lambda i, j: (0, 0))]
    return pl.pallas_call(
        functools.partial(_ffn_kernel, seq=seq), grid=(T // tm, nj), in_specs=in_specs,
        out_specs=pl.BlockSpec((tm, D), lambda i, j: (i, 0)),
        out_shape=jax.ShapeDtypeStruct((T, D), F32),
        scratch_shapes=[pltpu.VMEM((tm + FFN_HALO, D), BF16), pltpu.VMEM((tm, D), F32)],
        compiler_params=_cparams("parallel", "arbitrary"), name="conv_ffn")(
            x, x, g, w_up, w_up, conv_w, conv_w, conv_b, conv_b, w_down, g_final)


def _rope_tables(seq):
    half = ROPE_DIM // 2
    inv_freq = jnp.power(ROPE_THETA, -jnp.arange(0, ROPE_DIM, 2, dtype=F32) / ROPE_DIM)
    ang = jnp.arange(seq, dtype=jnp.int32).astype(F32)[:, None] * inv_freq[None, :]
    cos, sin = jnp.cos(ang), jnp.sin(ang)
    ones = jnp.ones((seq, HEAD_DIM - ROPE_DIM), F32)
    zeros = jnp.zeros((seq, HEAD_DIM - ROPE_DIM), F32)
    zh = jnp.zeros((seq, half), F32)
    c = jnp.concatenate([cos, cos, ones], axis=1)
    sa = jnp.concatenate([zh, sin, zeros], axis=1)
    sb = jnp.concatenate([-sin, zh, zeros], axis=1)
    tile2 = lambda t: jnp.concatenate([t, t], axis=1)
    return tile2(c), tile2(sa), tile2(sb)


def kernel(x, norm_mix_g, w_in, b_gate, mu_shift, w0, w_decay_up, a0, w_a_up, w_g_up, k_k, k_a, r_k,
           ln_x_w, ln_x_b, w_branch_attn, w_branch_rwkv, w_out, norm_ffn_g, w_ffn_up, conv_w, conv_b,
           w_ffn_down, norm_final_g):
    B, S, D = x.shape
    T = B * S
    depth = norm_mix_g.shape[0]
    attn_w = 3 * 4 * HEAD_DIM
    rw = D
    n_decay, n_iclr, n_gate = w_decay_up.shape[1], w_a_up.shape[1], w_g_up.shape[1]
    lora_w = n_decay + n_iclr + n_gate
    lora_pad = 3 * LANES
    c_rwkv = 3 * attn_w
    c_lora = c_rwkv + 3 * rw
    c_gate = c_lora + lora_w

    assert depth == 1 and S % ATTN_TILE == 0 and D == 16 * HEAD_DIM
    xt = x.reshape(T, D)
    cos, sa, sb = _rope_tables(S)
    for l in range(depth):
        wi = w_in[l]
        w_qkv = wi[:, :c_rwkv].astype(BF16)
        w_rkvz = jnp.concatenate(
            [wi[:, c_rwkv:c_gate], jnp.zeros((D, lora_pad - lora_w), F32)], axis=1).astype(BF16)
        w_gt = wi[:, c_gate:].astype(BF16)
        gmix = norm_mix_g[l].reshape(1, D)

        qkv = _norm_proj(xt, gmix, w_qkv, 512, attn_w, rope=(cos, sa, sb, S, 2))
        prk = _norm_proj(xt, gmix, w_rkvz, 512, (3 * rw + lora_pad) // 3)
        gates = _norm_proj(xt, gmix, w_gt, 512, 1024)

        y_attn = _attention(qkv, B, S)

        mu = jnp.concatenate([mu_shift[l], jnp.zeros((lora_pad - lora_w,), F32)]).reshape(1, -1)
        params = jnp.concatenate(
            [jnp.stack([w0[l], a0[l], k_k[l], k_a[l], r_k[l].reshape(-1)]), jnp.zeros((3, rw), F32)], axis=0)
        wd = jnp.concatenate([w_decay_up[l], jnp.zeros((LANES - n_decay, rw), F32)], axis=0).astype(BF16)
        wa = jnp.concatenate([jnp.zeros((n_decay, rw), F32), w_a_up[l],
                              jnp.zeros((LANES - n_decay - n_iclr, rw), F32)], axis=0).astype(BF16)
        wg = jnp.concatenate([w_g_up[l], jnp.zeros((2 * LANES - n_gate, rw), F32)], axis=0).astype(BF16)
        rp, yp, bonus, gate, pm, qm = _rwkv_chunks(prk, mu, params, wd, wa, wg, S, 512)
        y_scan = _rwkv_scan(rp, yp, pm, qm, B, S, 256)

        xt = _merge(xt, y_attn, y_scan, bonus, gate, ln_x_w[l].reshape(1, -1), ln_x_b[l].reshape(1, -1),
                    gates, b_gate[l].reshape(1, -1), w_branch_attn[l].astype(BF16),
                    w_branch_rwkv[l].astype(BF16), w_out[l].astype(BF16), 512)

        xt = _conv_ffn(xt, norm_ffn_g[l].reshape(1, D), w_ffn_up[l].astype(BF16), conv_w[l].reshape(3, -1),
                       conv_b[l].reshape(1, -1), w_ffn_down[l].astype(BF16), norm_final_g.reshape(1, D),
                       S, 512, 256)
    return xt.reshape(B, S, D)
```

```python
import functools

import jax
import jax.numpy as jnp
from jax import lax
from jax.experimental import pallas as pl
from jax.experimental.pallas import tpu as pltpu

F32 = jnp.float32
BF16 = jnp.bfloat16

HEAD_DIM = 64
LANES = 128
ATTN_GROUPS = ((128, 1), (512, 4), (2048, 16))
ATTN_SPAN = 128
ATTN_TILE = 2048
ATTN_UNROLL = 4
ROPE_THETA = 500000.0
ROPE_DIM = HEAD_DIM // 4
RMS_EPS = 1e-6
GN_EPS = 64e-5
CHUNK = 64
RWKV_HALO = 16
NEG_BIG = -1e30
VMEM_LIMIT = 56 * 1024 * 1024


def _cparams(*sem):
    return pltpu.CompilerParams(dimension_semantics=sem, vmem_limit_bytes=VMEM_LIMIT)


def _mm(a, b):
    return jnp.dot(a.astype(BF16), b.astype(BF16), preferred_element_type=F32)


def _mm_nt(a, b):
    return lax.dot_general(a.astype(BF16), b.astype(BF16), (((1,), (1,)), ((), ())),
                           preferred_element_type=F32)


def _mm_tn(a, b):
    return lax.dot_general(a.astype(BF16), b.astype(BF16), (((0,), (0,)), ((), ())),
                           preferred_element_type=F32)


def _split3(a):
    hi = a.astype(BF16)
    r1 = a - hi.astype(F32)
    mid = r1.astype(BF16)
    lo = (r1 - mid.astype(F32)).astype(BF16)
    return hi, mid, lo


def _mm_exact_rhs(a, b_bf16):
    hi, mid, lo = _split3(a)
    out = jnp.dot(lo, b_bf16, preferred_element_type=F32)
    out = out + jnp.dot(mid, b_bf16, preferred_element_type=F32)
    return out + jnp.dot(hi, b_bf16, preferred_element_type=F32)


def _mm_exact_lhs(a_bf16, b):
    hi, mid, lo = _split3(b)
    out = jnp.dot(a_bf16, lo, preferred_element_type=F32)
    out = out + jnp.dot(a_bf16, mid, preferred_element_type=F32)
    return out + jnp.dot(a_bf16, hi, preferred_element_type=F32)


def _mm3(a, b):
    ah = a.astype(BF16)
    al = (a - ah.astype(F32)).astype(BF16)
    bh = b.astype(BF16)
    bl = (b - bh.astype(F32)).astype(BF16)
    out = jnp.dot(al, bh, preferred_element_type=F32)
    out = out + jnp.dot(ah, bl, preferred_element_type=F32)
    return out + jnp.dot(ah, bh, preferred_element_type=F32)


def _sigmoid(z):
    return 1.0 / (1.0 + jnp.exp(-z))


def _head_ones():
    r = jnp.where(lax.broadcasted_iota(jnp.int32, (LANES, LANES), 0) < HEAD_DIM, 0, 1)
    c = jnp.where(lax.broadcasted_iota(jnp.int32, (LANES, LANES), 1) < HEAD_DIM, 0, 1)
    return jnp.where(r == c, 1.0, 0.0).astype(BF16)


def _norm_proj_kernel(x_ref, g_ref, w_ref, b_ref, o_ref, h_scr, *, gate):
    @pl.when(pl.program_id(1) == 0)
    def _():
        x = x_ref[...]
        ms = jnp.mean(x * x, axis=-1, keepdims=True)
        h_scr[...] = (x * lax.rsqrt(ms + RMS_EPS) * g_ref[...]).astype(BF16)

    o = jnp.dot(h_scr[...], w_ref[...], preferred_element_type=F32)
    if gate:
        o = _sigmoid(o + b_ref[...])
    o_ref[...] = o.astype(o_ref.dtype)


def _norm_proj_rope_kernel(x_ref, g_ref, w_ref, cos_ref, sa_ref, sb_ref, o_ref, h_scr, *, rope_tiles):
    j = pl.program_id(1)

    @pl.when(j == 0)
    def _():
        x = x_ref[...]
        ms = jnp.mean(x * x, axis=-1, keepdims=True)
        h_scr[...] = (x * lax.rsqrt(ms + RMS_EPS) * g_ref[...]).astype(BF16)

    o = jnp.dot(h_scr[...], w_ref[...], preferred_element_type=F32)
    tn = o.shape[1]

    @pl.when(j < rope_tiles)
    def _():
        half = ROPE_DIM // 2
        cos, sa, sb = cos_ref[...], sa_ref[...], sb_ref[...]
        for c in range(tn // LANES):
            ob = o[:, c * LANES:(c + 1) * LANES]
            o_ref[:, c * LANES:(c + 1) * LANES] = (
                ob * cos + pltpu.roll(ob, half, 1) * sa + pltpu.roll(ob, LANES - half, 1) * sb)

    @pl.when(j >= rope_tiles)
    def _():
        o_ref[...] = o


def _norm_proj(x, g, w, tm, tn, rope=None, gate_bias=None):
    T, D = x.shape
    N = w.shape[1]
    grid = (T // tm, N // tn)
    x_spec = pl.BlockSpec((tm, D), lambda i, j: (i, 0))
    g_spec = pl.BlockSpec((1, D), lambda i, j: (0, 0))
    w_spec = pl.BlockSpec((D, tn), lambda i, j: (0, j))
    o_spec = pl.BlockSpec((tm, tn), lambda i, j: (i, j))
    if rope is None:
        bias = jnp.zeros((1, N), F32) if gate_bias is None else gate_bias
        return pl.pallas_call(
            functools.partial(_norm_proj_kernel, gate=gate_bias is not None), grid=grid,
            in_specs=[x_spec, g_spec, w_spec, pl.BlockSpec((1, tn), lambda i, j: (0, j))], out_specs=o_spec,
            out_shape=jax.ShapeDtypeStruct((T, N), BF16),
            scratch_shapes=[pltpu.VMEM((tm, D), BF16)],
            compiler_params=_cparams("parallel", "arbitrary"), name="norm_proj")(x, g, w, bias)
    cos, sa, sb, seq, rope_tiles = rope
    nseq = seq // tm
    t_spec = pl.BlockSpec((tm, LANES), lambda i, j: (i % nseq, 0))
    return pl.pallas_call(
        functools.partial(_norm_proj_rope_kernel, rope_tiles=rope_tiles), grid=grid,
        in_specs=[x_spec, g_spec, w_spec, t_spec, t_spec, t_spec], out_specs=o_spec,
        out_shape=jax.ShapeDtypeStruct((T, N), F32),
        scratch_shapes=[pltpu.VMEM((tm, D), BF16)],
        compiler_params=_cparams("parallel", "arbitrary"), name="norm_proj_rope")(x, g, w, cos, sa, sb)


def _attn_kernel(q0, q1, q2, k0, k1, k2, v0, v1, v2, kp0, kp1, kp2, vp0, vp1, vp2, o_ref, obuf, mbuf, lbuf):
    tile = pl.program_id(2)
    q_refs, k_refs, v_refs = (q0, q1, q2), (k0, k1, k2), (v0, v1, v2)
    kp_refs, vp_refs = (kp0, kp1, kp2), (vp0, vp1, vp2)
    L = ATTN_SPAN
    TL = ATTN_TILE
    U = ATTN_UNROLL

    lane = lax.broadcasted_iota(jnp.int32, (L, LANES), 1)
    head_a = lane < HEAD_DIM
    qi = lax.broadcasted_iota(jnp.int32, (L, 2 * L), 0)
    kj = lax.broadcasted_iota(jnp.int32, (L, 2 * L), 1)
    rel = L + qi - kj
    band = (rel >= 0) & (rel <= L)
    band_first = band & (kj >= jnp.where(tile > 0, 0, L))

    def run_blocks(g, d, blocks):
        qs, kcs, vcs, valids = [], [], [], []
        for start, from_prev in blocks:
            rows = pl.ds(start, L, stride=d)
            qs.append(q_refs[g][rows, :] * (HEAD_DIM ** -0.5))
            if from_prev:
                k1, v1 = kp_refs[g][rows, :], vp_refs[g][rows, :]
            else:
                back = pl.ds(start - L * d, L, stride=d)
                k1, v1 = k_refs[g][back, :], v_refs[g][back, :]
            kcs.append(jnp.concatenate([k1, k_refs[g][rows, :]], axis=0).astype(BF16))
            vcs.append(jnp.concatenate([v1, v_refs[g][rows, :]], axis=0).astype(BF16))
            valids.append(band_first if from_prev else band)
        nb = range(len(blocks))
        qh = [[jnp.where(head_a, qs[b], 0.0).astype(BF16), jnp.where(head_a, 0.0, qs[b]).astype(BF16)] for b in nb]
        s = [[lax.dot_general(qh[b][h], kcs[b], (((1,), (1,)), ((), ())), preferred_element_type=F32)
              for h in range(2)] for b in nb]
        s = [[jnp.where(valids[b], s[b][h], NEG_BIG) for h in range(2)] for b in nb]
        m = [[jnp.max(s[b][h], axis=1, keepdims=True) for h in range(2)] for b in nb]
        e = [[jnp.exp(s[b][h] - m[b][h]) for h in range(2)] for b in nb]
        l = [[jnp.sum(e[b][h], axis=1, keepdims=True) for h in range(2)] for b in nb]
        o = [[jnp.dot(e[b][h].astype(BF16), vcs[b], preferred_element_type=F32) for h in range(2)] for b in nb]
        for b, (start, _) in enumerate(blocks):
            rows = pl.ds(start, L, stride=d)
            obuf[g, rows, :] = jnp.where(head_a, o[b][0], o[b][1])
            mbuf[g, rows, :] = jnp.where(head_a, jnp.broadcast_to(m[b][0], (L, LANES)),
                                         jnp.broadcast_to(m[b][1], (L, LANES)))
            lbuf[g, rows, :] = jnp.where(head_a, jnp.broadcast_to(l[b][0], (L, LANES)),
                                         jnp.broadcast_to(l[b][1], (L, LANES)))

    nblk = TL // L
    for g, (_, d) in enumerate(ATTN_GROUPS):
        def cur_body(it, carry, g=g, d=d):
            idxs = [it * U + u for u in range(U)]
            run_blocks(g, d, [((idx % d) + (L * d) * (idx // d), False) for idx in idxs])
            return carry

        if d >= U:
            def prev_body(it, carry, g=g, d=d):
                run_blocks(g, d, [(it * U + u, True) for u in range(U)])
                return carry

            lax.fori_loop(0, d // U, prev_body, 0)
            if d < nblk:
                lax.fori_loop(d // U, nblk // U, cur_body, 0)
        else:
            run_blocks(g, d, [((idx % d) + (L * d) * (idx // d), idx // d == 0) for idx in range(U)])
            lax.fori_loop(1, nblk // U, cur_body, 0)

    MR = 256

    def merge(i, carry):
        rows = pl.ds(pl.multiple_of(i * MR, MR), MR)
        m0, m1, m2 = mbuf[0, rows, :], mbuf[1, rows, :], mbuf[2, rows, :]
        mx = jnp.maximum(jnp.maximum(m0, m1), m2)
        w0, w1, w2 = jnp.exp(m0 - mx), jnp.exp(m1 - mx), jnp.exp(m2 - mx)
        num = w0 * obuf[0, rows, :] + w1 * obuf[1, rows, :] + w2 * obuf[2, rows, :]
        den = w0 * lbuf[0, rows, :] + w1 * lbuf[1, rows, :] + w2 * lbuf[2, rows, :]
        o_ref[rows, :] = (num / den).astype(o_ref.dtype)
        return carry

    lax.fori_loop(0, TL // MR, merge, 0)


def _attention(qkv, batch, seq):
    T = qkv.shape[0]
    TL = ATTN_TILE
    tiles = seq // TL
    npair = 2
    qcols = 3 * 4 * HEAD_DIM // LANES

    def spec(section, g):
        return pl.BlockSpec((TL, LANES),
                            lambda p, b, t, section=section, g=g: (b * tiles + t, section * qcols + 2 * g + p))

    def prev_spec(section, g):
        rows = ATTN_SPAN * ATTN_GROUPS[g][1]
        per_tile = TL // rows
        return pl.BlockSpec(
            (rows, LANES),
            lambda p, b, t, section=section, g=g: (jnp.maximum((b * tiles + t) * per_tile - 1, 0),
                                                   section * qcols + 2 * g + p))

    in_specs = ([spec(s, g) for s in range(3) for g in range(3)]
                + [prev_spec(s, g) for s in (1, 2) for g in range(3)])
    out_spec = pl.BlockSpec((TL, LANES), lambda p, b, t: (b * tiles + t, p))
    return pl.pallas_call(
        _attn_kernel, grid=(npair, batch, tiles), in_specs=in_specs, out_specs=out_spec,
        out_shape=jax.ShapeDtypeStruct((T, npair * LANES), BF16),
        scratch_shapes=[pltpu.VMEM((3, TL, LANES), F32), pltpu.VMEM((3, TL, LANES), F32),
                        pltpu.VMEM((3, TL, LANES), F32)],
        compiler_params=_cparams("parallel", "parallel", "parallel"), name="dilated_attn")(*([qkv] * 15))


def _stack_heads(x, head_a):
    return jnp.concatenate([jnp.where(head_a, x, 0.0), jnp.where(head_a, 0.0, x)], axis=0)


def _rwkv_chunk_kernel(r_ref, k_ref, v_ref, z_ref, rh_ref, kh_ref, vh_ref, zh_ref,
                       mur_ref, muk_ref, muv_ref, muz_ref, par_ref, wd_ref, wa_ref, wg_ref,
                       rp_ref, yp_ref, bonus_ref, gate_ref, p_ref, q_ref, *, seq):
    tm = r_ref.shape[0]
    C = CHUNK
    first = (pl.program_id(0) * tm) % seq == 0
    row = lax.broadcasted_iota(jnp.int32, (tm, 1), 0)

    def shifted(ref, halo_ref, mu_ref):
        p = ref[...].astype(F32)
        hal = jnp.where(first, 0.0, halo_ref[RWKV_HALO - 1:RWKV_HALO, :].astype(F32))
        prev = jnp.where(row == 0, hal, pltpu.roll(p, 1, 0))
        return p + (prev - p) * mu_ref[...]

    z = shifted(z_ref, zh_ref, muz_ref)
    r = shifted(r_ref, rh_ref, mur_ref)
    k = shifted(k_ref, kh_ref, muk_ref)
    v = shifted(v_ref, vh_ref, muv_ref)

    w0, a0, k_k, k_a, r_k = (par_ref[i:i + 1, :] for i in range(5))
    zwa = z[:, 0:LANES]
    dpre = w0 + _mm(jnp.tanh(zwa), wd_ref[...])
    w_log = -(jnp.maximum(-dpre, 0.0) + jnp.log(1.0 + jnp.exp(-jnp.abs(dpre)))) - 0.5
    lw = -jnp.exp(w_log)
    a = _sigmoid(a0 + _mm(zwa, wa_ref[...]))
    gate_ref[...] = _mm(_sigmoid(z[:, LANES:3 * LANES]), wg_ref[...]).astype(gate_ref.dtype)

    ones_bd = _head_ones()
    kk = k * k_k
    kk = kk / jnp.maximum(jnp.sqrt(_mm(kk * kk, ones_bd)), 1e-12)
    k2 = k * (1.0 + (a - 1.0) * k_a)
    bonus_ref[...] = (_mm(r * k2 * r_k, ones_bd) * v).astype(bonus_ref.dtype)
    beta = kk * a

    shift = C.bit_length() - 1
    ti = lax.broadcasted_iota(jnp.int32, (tm, tm), 0)
    si = lax.broadcasted_iota(jnp.int32, (tm, tm), 1)
    same_chunk = lax.shift_right_logical(ti, shift) == lax.shift_right_logical(si, shift)
    tril = jnp.where(same_chunk & (si <= ti), 1.0, 0.0).astype(BF16)
    cum = _mm_exact_lhs(tril, lw)
    t2 = lax.broadcasted_iota(jnp.int32, (2 * C, 2 * C), 0)
    s2 = lax.broadcasted_iota(jnp.int32, (2 * C, 2 * C), 1)
    same = jnp.where(t2 < C, 0, 1) == jnp.where(s2 < C, 0, 1)
    strict = same & (s2 < t2)
    incl = same & (s2 <= t2)
    eye = jnp.where(t2 == s2, 1.0, 0.0)
    lane = lax.broadcasted_iota(jnp.int32, (C, LANES), 1)
    head_a = lane < HEAD_DIM

    chunks = range(tm // C)
    xa, xr, yy, lhs_t, vb, gam = [], [], [], [], [], []
    for c in chunks:
        sl = slice(c * C, (c + 1) * C)
        cum_c, lw_c = cum[sl], lw[sl]
        tot = cum[(c + 1) * C - 1:(c + 1) * C, :]
        g_inc = jnp.exp(cum_c)
        g_prev = jnp.exp(cum_c - lw_c)
        g_inv = jnp.exp(-cum_c)
        g_end = jnp.exp(tot - cum_c)
        gam.append(jnp.exp(tot))
        xa.append(_stack_heads(-kk[sl] * g_prev, head_a))
        xr.append(_stack_heads(r[sl] * g_inc, head_a))
        yy.append(jnp.concatenate([_stack_heads(beta[sl] * g_inv, head_a),
                                   _stack_heads(k2[sl] * g_inv, head_a)], axis=0))
        bh = _stack_heads(beta[sl] * g_end, head_a)
        kh = _stack_heads(k2[sl] * g_end, head_a)
        lhs_t.append(jnp.concatenate([bh.T, kh.T], axis=1))
        vb.append(_stack_heads(v[sl], head_a))

    zz = [_mm_nt(jnp.concatenate([xa[c], xr[c]], axis=0), yy[c]) for c in chunks]
    l_ab = [jnp.where(strict, zz[c][:2 * C, :2 * C], 0.0) for c in chunks]
    l_ak = [jnp.where(strict, zz[c][:2 * C, 2 * C:], 0.0) for c in chunks]
    m_rbk = [jnp.concatenate([jnp.where(incl, zz[c][2 * C:, :2 * C], 0.0),
                              jnp.where(incl, zz[c][2 * C:, 2 * C:], 0.0)], axis=1) for c in chunks]
    lakv = [_mm(l_ak[c], vb[c]) for c in chunks]

    lp = [_mm(l_ab[c], l_ab[c]) for c in chunks]
    tinv = [eye + l_ab[c] for c in chunks]
    power = 2
    while 2 * power < C:
        both = [_mm(lp[c], jnp.concatenate([lp[c], tinv[c]], axis=1)) for c in chunks]
        tinv = [tinv[c] + both[c][:, 2 * C:] for c in chunks]
        lp = [both[c][:, :2 * C] for c in chunks]
        power *= 2
    tinv = [tinv[c] + _mm(lp[c], tinv[c]) for c in chunks]

    wu = [_mm(tinv[c], jnp.concatenate([xa[c], lakv[c]], axis=1)) for c in chunks]
    zero = jnp.zeros((2 * C, LANES), F32)
    for c in chunks:
        sl = slice(c * C, (c + 1) * C)
        rhs = jnp.concatenate([wu[c], jnp.concatenate([zero, vb[c]], axis=1)], axis=0)
        out = _mm(jnp.concatenate([m_rbk[c], lhs_t[c]], axis=0), rhs)
        rp = xr[c] + out[:2 * C, :LANES]
        yp = out[:2 * C, LANES:]
        rp_ref[sl, :] = rp[:C] + rp[C:]
        yp_ref[sl, :] = yp[:C] + yp[C:]
        pm = jnp.where(t2 == s2, gam[c], 0.0) + out[2 * C:, :LANES]
        qm = out[2 * C:, LANES:]
        p_ref[c, 0] = pm[:C] + pm[C:]
        q_ref[c, 0] = qm[:C] + qm[C:]


def _rwkv_chunks(prk, mu, params, wd, wa, wg, seq, tm):
    T = prk.shape[0]
    npair = 8
    zblk = 3 * LANES
    zcol = (3 * npair * LANES) // zblk
    hb = tm // RWKV_HALO

    def col(base):
        return pl.BlockSpec((tm, LANES), lambda i, j, base=base: (i, base + j))

    def halo(base):
        return pl.BlockSpec((RWKV_HALO, LANES), lambda i, j, base=base: (jnp.maximum(i * hb - 1, 0), base + j))

    def mucol(base):
        return pl.BlockSpec((1, LANES), lambda i, j, base=base: (0, base + j))

    in_specs = [col(0), col(npair), col(2 * npair), pl.BlockSpec((tm, zblk), lambda i, j: (i, zcol)),
                halo(0), halo(npair), halo(2 * npair),
                pl.BlockSpec((RWKV_HALO, zblk), lambda i, j: (jnp.maximum(i * hb - 1, 0), zcol)),
                mucol(0), mucol(npair), mucol(2 * npair), pl.BlockSpec((1, zblk), lambda i, j: (0, zcol)),
                pl.BlockSpec((8, LANES), lambda i, j: (0, j)),
                pl.BlockSpec((LANES, LANES), lambda i, j: (0, j)),
                pl.BlockSpec((LANES, LANES), lambda i, j: (0, j)),
                pl.BlockSpec((2 * LANES, LANES), lambda i, j: (0, j))]
    tok = pl.BlockSpec((tm, LANES), lambda i, j: (i, j))
    mat = pl.BlockSpec((tm // CHUNK, 1, CHUNK, LANES), lambda i, j: (i, j, 0, 0))
    tok_shape = jax.ShapeDtypeStruct((T, npair * LANES), F32)
    tok_bf16 = jax.ShapeDtypeStruct((T, npair * LANES), BF16)
    mat_shape = jax.ShapeDtypeStruct((T // CHUNK, npair, CHUNK, LANES), F32)
    return pl.pallas_call(
        functools.partial(_rwkv_chunk_kernel, seq=seq), grid=(T // tm, npair),
        in_specs=in_specs, out_specs=[tok, tok, tok, tok, mat, mat],
        out_shape=[tok_shape, tok_shape, tok_bf16, tok_bf16, mat_shape, mat_shape],
        compiler_params=_cparams("parallel", "parallel"), name="rwkv_chunks")(
            prk, prk, prk, prk, prk, prk, prk, prk, mu, mu, mu, mu, params, wd, wa, wg)


def _rwkv_scan_kernel(rp_ref, yp_ref, p_ref, q_ref, y_ref, s_scr):
    nseq, npair = s_scr.shape[0], s_scr.shape[1]
    nchunk = p_ref.shape[1]
    C = CHUNK

    @pl.when(pl.program_id(0) == 0)
    def _():
        s_scr[...] = jnp.zeros(s_scr.shape, F32)

    head_a = lax.broadcasted_iota(jnp.int32, (C, LANES), 1) < HEAD_DIM

    def chunk(c, carry):
        rows = pl.ds(pl.multiple_of(c * C, C), C)
        units = [(b, j) for b in range(nseq) for j in range(npair)]
        s = [s_scr[b, j] for b, j in units]
        for (b, j), s_u in zip(units, s):
            cols = slice(j * LANES, (j + 1) * LANES)
            y_ref[b, rows, cols] = _mm(rp_ref[b, rows, cols], s_u) + yp_ref[b, rows, cols]
        for (b, j), s_u in zip(units, s):
            s_scr[b, j] = _mm3(_stack_heads(p_ref[b, c, j], head_a), s_u) + _stack_heads(q_ref[b, c, j], head_a)
        return carry

    lax.fori_loop(0, nchunk, chunk, 0)


def _rwkv_scan(rp, yp, p, q, batch, seq, tm):
    T, W = rp.shape
    npair = W // LANES
    rp, yp = rp.reshape(batch, seq, W), yp.reshape(batch, seq, W)
    p = p.reshape(batch, seq // CHUNK, npair, CHUNK, LANES)
    q = q.reshape(batch, seq // CHUNK, npair, CHUNK, LANES)
    tok = pl.BlockSpec((batch, tm, W), lambda i: (0, i, 0))
    mat = pl.BlockSpec((batch, tm // CHUNK, npair, CHUNK, LANES), lambda i: (0, i, 0, 0, 0))
    y = pl.pallas_call(
        _rwkv_scan_kernel, grid=(seq // tm,),
        in_specs=[tok, tok, mat, mat], out_specs=tok,
        out_shape=jax.ShapeDtypeStruct((batch, seq, W), F32),
        scratch_shapes=[pltpu.VMEM((batch, npair, LANES, LANES), F32)],
        compiler_params=_cparams("arbitrary"), name="rwkv_scan")(rp, yp, p, q)
    return y.reshape(T, W)


def _merge_kernel(x_ref, ya_ref, y_ref, bonus_ref, rg_ref, lnw_ref, lnb_ref, ga_ref, gr_ref,
                  wba_ref, wbr_ref, wo_ref, o_ref, yr_scr):
    ones_bd = _head_ones()
    inv_n = 1.0 / HEAD_DIM
    for j in range(y_ref.shape[1] // LANES):
        cols = slice(j * LANES, (j + 1) * LANES)
        y = y_ref[:, cols]
        yc = y - _mm(y, ones_bd) * inv_n
        var = _mm(yc * yc, ones_bd) * inv_n
        yn = yc * lax.rsqrt(var + GN_EPS) * lnw_ref[:, cols] + lnb_ref[:, cols]
        yr_scr[:, cols] = ((yn + bonus_ref[:, cols]) * rg_ref[:, cols]).astype(BF16)

    ba = jnp.dot(ya_ref[...], wba_ref[...], preferred_element_type=F32)
    br = jnp.dot(yr_scr[...], wbr_ref[...], preferred_element_type=F32)
    merged = ga_ref[...] * ba + gr_ref[...] * br
    o_ref[...] = x_ref[...] + jnp.dot(merged.astype(BF16), wo_ref[...], preferred_element_type=F32)


def _merge(x, ya, y, bonus, rgate, lnw, lnb, gates, wba, wbr, wo, tm):
    T, D = x.shape
    row = lambda w: pl.BlockSpec((tm, w), lambda i: (i, 0))
    full = lambda a: pl.BlockSpec(a.shape, lambda i: (0, 0))
    return pl.pallas_call(
        _merge_kernel, grid=(T // tm,),
        in_specs=[row(D), row(ya.shape[1]), row(D), row(D), row(D), full(lnw), full(lnb),
                  pl.BlockSpec((tm, D), lambda i: (i, 0)), pl.BlockSpec((tm, D), lambda i: (i, 1)),
                  full(wba), full(wbr), full(wo)],
        out_specs=row(D), out_shape=jax.ShapeDtypeStruct((T, D), F32),
        scratch_shapes=[pltpu.VMEM((tm, D), BF16)],
        compiler_params=_cparams("parallel"), name="merge_out")(
            x, ya, y, bonus, rgate, lnw, lnb, gates, gates, wba, wbr, wo)


FFN_HALO = 16


def _ffn_kernel(x_ref, xh_ref, g_ref, wg_ref, wv_ref, cwg_ref, cwv_ref, cbg_ref, cbv_ref, wd_ref, gf_ref,
                o_ref, h_scr, acc_scr, *, seq):
    i, j = pl.program_id(0), pl.program_id(1)
    tm = x_ref.shape[0]
    H = FFN_HALO

    def norm(x):
        ms = jnp.mean(x * x, axis=-1, keepdims=True)
        return x * lax.rsqrt(ms + RMS_EPS) * g_ref[...]

    @pl.when(j == 0)
    def _():
        first = (i * tm) % seq == 0
        h_scr[0:H, :] = jnp.where(first, 0.0, norm(xh_ref[...])).astype(BF16)
        h_scr[H:, :] = norm(x_ref[...]).astype(BF16)
        acc_scr[...] = jnp.zeros(acc_scr.shape, F32)

    h = h_scr[...]

    def conv(u, cw_ref, cb_ref):
        c = u * cw_ref[2:3, :] + pltpu.roll(u, 1, 0) * cw_ref[1:2, :] + pltpu.roll(u, 2, 0) * cw_ref[0:1, :]
        return c[H:] + cb_ref[...]

    cg = conv(jnp.dot(h, wg_ref[...], preferred_element_type=F32), cwg_ref, cbg_ref)
    cv = conv(jnp.dot(h, wv_ref[...], preferred_element_type=F32), cwv_ref, cbv_ref)
    act = cg * _sigmoid(cg) * cv
    acc_scr[...] += jnp.dot(act.astype(BF16), wd_ref[...], preferred_element_type=F32)

    @pl.when(j == pl.num_programs(1) - 1)
    def _():
        x2 = x_ref[...] + acc_scr[...]
        ms = jnp.mean(x2 * x2, axis=-1, keepdims=True)
        o_ref[...] = x2 * lax.rsqrt(ms + RMS_EPS) * gf_ref[...]


def _conv_ffn(x, g, w_up, conv_w, conv_b, w_down, g_final, seq, tm, tf):
    T, D = x.shape
    dff = w_down.shape[0]
    nj = dff // tf
    hb = tm // FFN_HALO
    in_specs = [pl.BlockSpec((tm, D), lambda i, j: (i, 0)),
                pl.BlockSpec((FFN_HALO, D), lambda i, j: (jnp.maximum(i * hb - 1, 0), 0)),
                pl.BlockSpec((1, D), lambda i, j: (0, 0)),
                pl.BlockSpec((D, tf), lambda i, j: (0, j)),
                pl.BlockSpec((D, tf), lambda i, j: (0, nj + j)),
                pl.BlockSpec((3, tf), lambda i, j: (0, j)),
                pl.BlockSpec((3, tf), lambda i, j: (0, nj + j)),
                pl.BlockSpec((1, tf), lambda i, j: (0, j)),
                pl.BlockSpec((1, tf), lambda i, j: (0, nj + j)),
                pl.BlockSpec((tf, D), lambda i, j: (j, 0)),
                pl.BlockSpec((1, D), lambda i, j: (0, 0))]
    return pl.pallas_call(
        functools.partial(_ffn_kernel, seq=seq), grid=(T // tm, nj), in_specs=in_specs,
        out_specs=pl.BlockSpec((tm, D), lambda i, j: (i, 0)),
        out_shape=jax.ShapeDtypeStruct((T, D), F32),
        scratch_shapes=[pltpu.VMEM((tm + FFN_HALO, D), BF16), pltpu.VMEM((tm, D), F32)],
        compiler_params=_cparams("parallel", "arbitrary"), name="conv_ffn")(
            x, x, g, w_up, w_up, conv_w, conv_w, conv_b, conv_b, w_down, g_final)


def _rope_tables(seq):
    half = ROPE_DIM // 2
    inv_freq = jnp.power(ROPE_THETA, -jnp.arange(0, ROPE_DIM, 2, dtype=F32) / ROPE_DIM)
    ang = jnp.arange(seq, dtype=jnp.int32).astype(F32)[:, None] * inv_freq[None, :]
    cos, sin = jnp.cos(ang), jnp.sin(ang)
    ones = jnp.ones((seq, HEAD_DIM - ROPE_DIM), F32)
    zeros = jnp.zeros((seq, HEAD_DIM - ROPE_DIM), F32)
    zh = jnp.zeros((seq, half), F32)
    c = jnp.concatenate([cos, cos, ones], axis=1)
    sa = jnp.concatenate([zh, sin, zeros], axis=1)
    sb = jnp.concatenate([-sin, zh, zeros], axis=1)
    tile2 = lambda t: jnp.concatenate([t, t], axis=1)
    return tile2(c), tile2(sa), tile2(sb)


def kernel(x, norm_mix_g, w_in, b_gate, mu_shift, w0, w_decay_up, a0, w_a_up, w_g_up, k_k, k_a, r_k,
           ln_x_w, ln_x_b, w_branch_attn, w_branch_rwkv, w_out, norm_ffn_g, w_ffn_up, conv_w, conv_b,
           w_ffn_down, norm_final_g):
    B, S, D = x.shape
    T = B * S
    depth = norm_mix_g.shape[0]
    attn_w = 3 * 4 * HEAD_DIM
    rw = D
    n_decay, n_iclr, n_gate = w_decay_up.shape[1], w_a_up.shape[1], w_g_up.shape[1]
    lora_w = n_decay + n_iclr + n_gate
    lora_pad = 3 * LANES
    c_rwkv = 3 * attn_w
    c_lora = c_rwkv + 3 * rw
    c_gate = c_lora + lora_w

    assert depth == 1 and S % ATTN_TILE == 0 and D == 16 * HEAD_DIM
    xt = x.reshape(T, D)
    cos, sa, sb = _rope_tables(S)
    for l in range(depth):
        wi = w_in[l]
        w_qkv = wi[:, :c_rwkv].astype(BF16)
        w_rkvz = wi[:, c_rwkv:c_lora + lora_pad].astype(BF16)
        w_gt = wi[:, c_gate:].astype(BF16)
        gmix = norm_mix_g[l].reshape(1, D)

        qkv = _norm_proj(xt, gmix, w_qkv, 1024, attn_w, rope=(cos, sa, sb, S, 2))
        prk = _norm_proj(xt, gmix, w_rkvz, 1024, (3 * rw + lora_pad) // 3)
        gates = _norm_proj(xt, gmix, w_gt, 1024, 1024, gate_bias=b_gate[l].reshape(1, -1))

        y_attn = _attention(qkv, B, S)

        mu = jnp.concatenate([mu_shift[l], jnp.zeros((lora_pad - lora_w,), F32)]).reshape(1, -1)
        params = jnp.concatenate(
            [jnp.stack([w0[l], a0[l], k_k[l], k_a[l], r_k[l].reshape(-1)]), jnp.zeros((3, rw), F32)], axis=0)
        wd = jnp.concatenate([w_decay_up[l], jnp.zeros((LANES - n_decay, rw), F32)], axis=0).astype(BF16)
        wa = jnp.concatenate([jnp.zeros((n_decay, rw), F32), w_a_up[l],
                              jnp.zeros((LANES - n_decay - n_iclr, rw), F32)], axis=0).astype(BF16)
        wg = jnp.concatenate([w_g_up[l], jnp.zeros((2 * LANES - n_gate, rw), F32)], axis=0).astype(BF16)
        rp, yp, bonus, gate, pm, qm = _rwkv_chunks(prk, mu, params, wd, wa, wg, S, 512)
        y_scan = _rwkv_scan(rp, yp, pm, qm, B, S, 256)

        xt = _merge(xt, y_attn, y_scan, bonus, gate, ln_x_w[l].reshape(1, -1), ln_x_b[l].reshape(1, -1),
                    gates, w_branch_attn[l].astype(BF16),
                    w_branch_rwkv[l].astype(BF16), w_out[l].astype(BF16), 512)

        xt = _conv_ffn(xt, norm_ffn_g[l].reshape(1, D), w_ffn_up[l].astype(BF16), conv_w[l].reshape(3, -1),
                       conv_b[l].reshape(1, -1), w_ffn_down[l].astype(BF16), norm_final_g.reshape(1, D),
                       S, 512, w_ffn_down.shape[1] // 2)
    return xt.reshape(B, S, D)
```

```python
import functools

import jax
import jax.numpy as jnp
from jax import lax
from jax.experimental import pallas as pl
from jax.experimental.pallas import tpu as pltpu

F32 = jnp.float32
BF16 = jnp.bfloat16

HEAD_DIM = 64
LANES = 128
ATTN_GROUPS = ((128, 1), (512, 4), (2048, 16))
ATTN_SPAN = 128
ATTN_TILE = 2048
ATTN_UNROLL = 4
ROPE_THETA = 500000.0
ROPE_DIM = HEAD_DIM // 4
RMS_EPS = 1e-6
GN_EPS = 64e-5
CHUNK = 64
RWKV_HALO = 16
NEG_BIG = -1e30
VMEM_LIMIT = 56 * 1024 * 1024


def _cparams(*sem):
    return pltpu.CompilerParams(dimension_semantics=sem, vmem_limit_bytes=VMEM_LIMIT)


def _mm(a, b):
    return jnp.dot(a.astype(BF16), b.astype(BF16), preferred_element_type=F32)


def _mm_nt(a, b):
    return lax.dot_general(a.astype(BF16), b.astype(BF16), (((1,), (1,)), ((), ())),
                           preferred_element_type=F32)


def _mm_tn(a, b):
    return lax.dot_general(a.astype(BF16), b.astype(BF16), (((0,), (0,)), ((), ())),
                           preferred_element_type=F32)


def _split3(a):
    hi = a.astype(BF16)
    r1 = a - hi.astype(F32)
    mid = r1.astype(BF16)
    lo = (r1 - mid.astype(F32)).astype(BF16)
    return hi, mid, lo


def _mm_exact_rhs(a, b_bf16):
    hi, mid, lo = _split3(a)
    out = jnp.dot(lo, b_bf16, preferred_element_type=F32)
    out = out + jnp.dot(mid, b_bf16, preferred_element_type=F32)
    return out + jnp.dot(hi, b_bf16, preferred_element_type=F32)


def _mm_exact_lhs(a_bf16, b):
    hi, mid, lo = _split3(b)
    out = jnp.dot(a_bf16, lo, preferred_element_type=F32)
    out = out + jnp.dot(a_bf16, mid, preferred_element_type=F32)
    return out + jnp.dot(a_bf16, hi, preferred_element_type=F32)


def _mm3(a, b):
    ah = a.astype(BF16)
    al = (a - ah.astype(F32)).astype(BF16)
    bh = b.astype(BF16)
    bl = (b - bh.astype(F32)).astype(BF16)
    out = jnp.dot(al, bh, preferred_element_type=F32)
    out = out + jnp.dot(ah, bl, preferred_element_type=F32)
    return out + jnp.dot(ah, bh, preferred_element_type=F32)


def _sigmoid(z):
    return 1.0 / (1.0 + jnp.exp(-z))


def _head_ones():
    r = jnp.where(lax.broadcasted_iota(jnp.int32, (LANES, LANES), 0) < HEAD_DIM, 0, 1)
    c = jnp.where(lax.broadcasted_iota(jnp.int32, (LANES, LANES), 1) < HEAD_DIM, 0, 1)
    return jnp.where(r == c, 1.0, 0.0).astype(BF16)


def _in_proj_kernel(x_ref, g_ref, wq_ref, wr_ref, wg_ref, bg_ref, cos_ref, sa_ref, sb_ref,
                    oq_ref, or_ref, og_ref, h_scr, *, nq, nr, rope_tiles):
    j = pl.program_id(1)

    @pl.when(j == 0)
    def _():
        x = x_ref[...]
        ms = jnp.mean(x * x, axis=-1, keepdims=True)
        h_scr[...] = (x * lax.rsqrt(ms + RMS_EPS) * g_ref[...]).astype(BF16)

    @pl.when(j < rope_tiles)
    def _():
        o = jnp.dot(h_scr[...], wq_ref[...], preferred_element_type=F32)
        half = ROPE_DIM // 2
        cos, sa, sb = cos_ref[...], sa_ref[...], sb_ref[...]
        for c in range(o.shape[1] // LANES):
            ob = o[:, c * LANES:(c + 1) * LANES]
            oq_ref[:, c * LANES:(c + 1) * LANES] = (
                ob * cos + pltpu.roll(ob, half, 1) * sa + pltpu.roll(ob, LANES - half, 1) * sb)

    @pl.when((j >= rope_tiles) & (j < nq))
    def _():
        oq_ref[...] = jnp.dot(h_scr[...], wq_ref[...], preferred_element_type=F32)

    @pl.when((j >= nq) & (j < nq + nr))
    def _():
        or_ref[...] = jnp.dot(h_scr[...], wr_ref[...], preferred_element_type=F32).astype(or_ref.dtype)

    @pl.when(j >= nq + nr)
    def _():
        o = jnp.dot(h_scr[...], wg_ref[...], preferred_element_type=F32)
        og_ref[...] = _sigmoid(o + bg_ref[...]).astype(og_ref.dtype)


def _in_proj(x, g, w_qkv, w_rkvz, w_gate, b_gate, rope, seq, tm, tq, tr, tg, rope_tiles):
    T, D = x.shape
    nq, nr, ng = w_qkv.shape[1] // tq, w_rkvz.shape[1] // tr, w_gate.shape[1] // tg
    cos, sa, sb = rope
    nseq = seq // tm

    def clamp(j, lo, n):
        return jnp.minimum(jnp.maximum(j - lo, 0), n - 1)

    const = lambda shape: pl.BlockSpec(shape, lambda i, j: (0, 0))
    t_spec = pl.BlockSpec((tm, LANES), lambda i, j: (i % nseq, 0))
    in_specs = [pl.BlockSpec((tm, D), lambda i, j: (i, 0)), const((1, D)),
                pl.BlockSpec((D, tq), lambda i, j: (0, clamp(j, 0, nq))),
                pl.BlockSpec((D, tr), lambda i, j: (0, clamp(j, nq, nr))),
                pl.BlockSpec((D, tg), lambda i, j: (0, clamp(j, nq + nr, ng))),
                pl.BlockSpec((1, tg), lambda i, j: (0, clamp(j, nq + nr, ng))),
                t_spec, t_spec, t_spec]
    out_specs = [pl.BlockSpec((tm, tq), lambda i, j: (i, clamp(j, 0, nq))),
                 pl.BlockSpec((tm, tr), lambda i, j: (i, clamp(j, nq, nr))),
                 pl.BlockSpec((tm, tg), lambda i, j: (i, clamp(j, nq + nr, ng)))]
    out_shape = [jax.ShapeDtypeStruct((T, w_qkv.shape[1]), F32),
                 jax.ShapeDtypeStruct((T, w_rkvz.shape[1]), BF16),
                 jax.ShapeDtypeStruct((T, w_gate.shape[1]), BF16)]
    return pl.pallas_call(
        functools.partial(_in_proj_kernel, nq=nq, nr=nr, rope_tiles=rope_tiles),
        grid=(T // tm, nq + nr + ng), in_specs=in_specs, out_specs=out_specs, out_shape=out_shape,
        scratch_shapes=[pltpu.VMEM((tm, D), BF16)],
        compiler_params=_cparams("parallel", "arbitrary"), name="in_proj")(
            x, g, w_qkv, w_rkvz, w_gate, b_gate, cos, sa, sb)


def _attn_kernel(q0, q1, q2, k0, k1, k2, v0, v1, v2, kp0, kp1, kp2, vp0, vp1, vp2, o_ref, obuf, mbuf, lbuf):
    tile = pl.program_id(2)
    q_refs, k_refs, v_refs = (q0, q1, q2), (k0, k1, k2), (v0, v1, v2)
    kp_refs, vp_refs = (kp0, kp1, kp2), (vp0, vp1, vp2)
    L = ATTN_SPAN
    TL = ATTN_TILE
    U = ATTN_UNROLL

    lane = lax.broadcasted_iota(jnp.int32, (L, LANES), 1)
    head_a = lane < HEAD_DIM
    qi = lax.broadcasted_iota(jnp.int32, (L, 2 * L), 0)
    kj = lax.broadcasted_iota(jnp.int32, (L, 2 * L), 1)
    rel = L + qi - kj
    band = (rel >= 0) & (rel <= L)
    band_first = band & (kj >= jnp.where(tile > 0, 0, L))

    def run_blocks(g, d, blocks):
        qs, kcs, vcs, valids = [], [], [], []
        for start, from_prev in blocks:
            rows = pl.ds(start, L, stride=d)
            qs.append(q_refs[g][rows, :] * (HEAD_DIM ** -0.5))
            if from_prev:
                k1, v1 = kp_refs[g][rows, :], vp_refs[g][rows, :]
            else:
                back = pl.ds(start - L * d, L, stride=d)
                k1, v1 = k_refs[g][back, :], v_refs[g][back, :]
            kcs.append(jnp.concatenate([k1, k_refs[g][rows, :]], axis=0).astype(BF16))
            vcs.append(jnp.concatenate([v1, v_refs[g][rows, :]], axis=0).astype(BF16))
            valids.append(band_first if from_prev else band)
        nb = range(len(blocks))
        qh = [[jnp.where(head_a, qs[b], 0.0).astype(BF16), jnp.where(head_a, 0.0, qs[b]).astype(BF16)] for b in nb]
        s = [[lax.dot_general(qh[b][h], kcs[b], (((1,), (1,)), ((), ())), preferred_element_type=F32)
              for h in range(2)] for b in nb]
        s = [[jnp.where(valids[b], s[b][h], NEG_BIG) for h in range(2)] for b in nb]
        m = [[jnp.max(s[b][h], axis=1, keepdims=True) for h in range(2)] for b in nb]
        e = [[jnp.exp(s[b][h] - m[b][h]) for h in range(2)] for b in nb]
        l = [[jnp.sum(e[b][h], axis=1, keepdims=True) for h in range(2)] for b in nb]
        o = [[jnp.dot(e[b][h].astype(BF16), vcs[b], preferred_element_type=F32) for h in range(2)] for b in nb]
        for b, (start, _) in enumerate(blocks):
            rows = pl.ds(start, L, stride=d)
            obuf[g, rows, :] = jnp.where(head_a, o[b][0], o[b][1])
            mbuf[g, rows, :] = jnp.where(head_a, jnp.broadcast_to(m[b][0], (L, LANES)),
                                         jnp.broadcast_to(m[b][1], (L, LANES)))
            lbuf[g, rows, :] = jnp.where(head_a, jnp.broadcast_to(l[b][0], (L, LANES)),
                                         jnp.broadcast_to(l[b][1], (L, LANES)))

    nblk = TL // L
    for g, (_, d) in enumerate(ATTN_GROUPS):
        def cur_body(it, carry, g=g, d=d):
            idxs = [it * U + u for u in range(U)]
            run_blocks(g, d, [((idx % d) + (L * d) * (idx // d), False) for idx in idxs])
            return carry

        if d >= U:
            def prev_body(it, carry, g=g, d=d):
                run_blocks(g, d, [(it * U + u, True) for u in range(U)])
                return carry

            lax.fori_loop(0, d // U, prev_body, 0)
            if d < nblk:
                lax.fori_loop(d // U, nblk // U, cur_body, 0)
        else:
            run_blocks(g, d, [((idx % d) + (L * d) * (idx // d), idx // d == 0) for idx in range(U)])
            lax.fori_loop(1, nblk // U, cur_body, 0)

    MR = 256

    def merge(i, carry):
        rows = pl.ds(pl.multiple_of(i * MR, MR), MR)
        m0, m1, m2 = mbuf[0, rows, :], mbuf[1, rows, :], mbuf[2, rows, :]
        mx = jnp.maximum(jnp.maximum(m0, m1), m2)
        w0, w1, w2 = jnp.exp(m0 - mx), jnp.exp(m1 - mx), jnp.exp(m2 - mx)
        num = w0 * obuf[0, rows, :] + w1 * obuf[1, rows, :] + w2 * obuf[2, rows, :]
        den = w0 * lbuf[0, rows, :] + w1 * lbuf[1, rows, :] + w2 * lbuf[2, rows, :]
        o_ref[rows, :] = (num / den).astype(o_ref.dtype)
        return carry

    lax.fori_loop(0, TL // MR, merge, 0)


def _attention(qkv, batch, seq):
    T = qkv.shape[0]
    TL = ATTN_TILE
    tiles = seq // TL
    npair = 2
    qcols = 3 * 4 * HEAD_DIM // LANES

    def spec(section, g):
        return pl.BlockSpec((TL, LANES),
                            lambda p, b, t, section=section, g=g: (b * tiles + t, section * qcols + 2 * g + p))

    def prev_spec(section, g):
        rows = ATTN_SPAN * ATTN_GROUPS[g][1]
        per_tile = TL // rows
        return pl.BlockSpec(
            (rows, LANES),
            lambda p, b, t, section=section, g=g: (jnp.maximum((b * tiles + t) * per_tile - 1, 0),
                                                   section * qcols + 2 * g + p))

    in_specs = ([spec(s, g) for s in range(3) for g in range(3)]
                + [prev_spec(s, g) for s in (1, 2) for g in range(3)])
    out_spec = pl.BlockSpec((TL, LANES), lambda p, b, t: (b * tiles + t, p))
    return pl.pallas_call(
        _attn_kernel, grid=(npair, batch, tiles), in_specs=in_specs, out_specs=out_spec,
        out_shape=jax.ShapeDtypeStruct((T, npair * LANES), BF16),
        scratch_shapes=[pltpu.VMEM((3, TL, LANES), F32), pltpu.VMEM((3, TL, LANES), F32),
                        pltpu.VMEM((3, TL, LANES), F32)],
        compiler_params=_cparams("parallel", "parallel", "parallel"), name="dilated_attn")(*([qkv] * 15))


def _stack_heads(x, head_a):
    return jnp.concatenate([jnp.where(head_a, x, 0.0), jnp.where(head_a, 0.0, x)], axis=0)


def _rwkv_chunk_kernel(r_ref, k_ref, v_ref, z_ref, rh_ref, kh_ref, vh_ref, zh_ref,
                       mur_ref, muk_ref, muv_ref, muz_ref, par_ref, wd_ref, wa_ref, wg_ref,
                       rp_ref, yp_ref, bonus_ref, gate_ref, p_ref, q_ref, *, seq):
    tm = r_ref.shape[0]
    C = CHUNK
    first = (pl.program_id(0) * tm) % seq == 0
    row = lax.broadcasted_iota(jnp.int32, (tm, 1), 0)

    def shifted(ref, halo_ref, mu_ref):
        p = ref[...].astype(F32)
        hal = jnp.where(first, 0.0, halo_ref[RWKV_HALO - 1:RWKV_HALO, :].astype(F32))
        prev = jnp.where(row == 0, hal, pltpu.roll(p, 1, 0))
        return p + (prev - p) * mu_ref[...]

    z = shifted(z_ref, zh_ref, muz_ref)
    r = shifted(r_ref, rh_ref, mur_ref)
    k = shifted(k_ref, kh_ref, muk_ref)
    v = shifted(v_ref, vh_ref, muv_ref)

    w0, a0, k_k, k_a, r_k = (par_ref[i:i + 1, :] for i in range(5))
    zwa = z[:, 0:LANES]
    dpre = w0 + _mm(jnp.tanh(zwa), wd_ref[...])
    w_log = -(jnp.maximum(-dpre, 0.0) + jnp.log(1.0 + jnp.exp(-jnp.abs(dpre)))) - 0.5
    lw = -jnp.exp(w_log)
    a = _sigmoid(a0 + _mm(zwa, wa_ref[...]))
    gate_ref[...] = _mm(_sigmoid(z[:, LANES:3 * LANES]), wg_ref[...]).astype(gate_ref.dtype)

    ones_bd = _head_ones()
    kk = k * k_k
    kk = kk / jnp.maximum(jnp.sqrt(_mm(kk * kk, ones_bd)), 1e-12)
    k2 = k * (1.0 + (a - 1.0) * k_a)
    bonus_ref[...] = (_mm(r * k2 * r_k, ones_bd) * v).astype(bonus_ref.dtype)
    beta = kk * a

    ti = lax.broadcasted_iota(jnp.int32, (C, C), 0)
    si = lax.broadcasted_iota(jnp.int32, (C, C), 1)
    tril = jnp.where(si <= ti, 1.0, 0.0).astype(BF16)
    t2 = lax.broadcasted_iota(jnp.int32, (2 * C, 2 * C), 0)
    s2 = lax.broadcasted_iota(jnp.int32, (2 * C, 2 * C), 1)
    same = jnp.where(t2 < C, 0, 1) == jnp.where(s2 < C, 0, 1)
    strict = same & (s2 < t2)
    incl = same & (s2 <= t2)
    eye = jnp.where(t2 == s2, 1.0, 0.0)
    lane = lax.broadcasted_iota(jnp.int32, (C, LANES), 1)
    head_a = lane < HEAD_DIM
    krow = lax.broadcasted_iota(jnp.int32, (C, LANES), 0)
    diag = (lane == krow) | (lane == krow + HEAD_DIM)

    chunks = range(tm // C)
    xa, xr, yy, lhs_t, vb, gam, zz = [], [], [], [], [], [], []
    for c in chunks:
        sl = slice(c * C, (c + 1) * C)
        lw_c = lw[sl]
        cum_c = _mm_exact_lhs(tril, lw_c)
        tot = cum_c[C - 1:C, :]
        g_inc = jnp.exp(cum_c)
        g_prev = jnp.exp(cum_c - lw_c)
        g_inv = jnp.exp(-cum_c)
        g_end = jnp.exp(tot - cum_c)
        gam.append(jnp.exp(tot))
        xa.append(_stack_heads(-kk[sl] * g_prev, head_a))
        xr.append(_stack_heads(r[sl] * g_inc, head_a))
        yb, yk = beta[sl] * g_inv, k2[sl] * g_inv
        yy.append(jnp.concatenate([yb, yb, yk, yk], axis=0))
        bh, kh = beta[sl] * g_end, k2[sl] * g_end
        lhs_t.append(jnp.concatenate([bh, bh, kh, kh], axis=0).T)
        vb.append(_stack_heads(v[sl], head_a))
        zz.append(_mm_nt(jnp.concatenate([xa[c], xr[c]], axis=0), yy[c]))

    l_ab = [jnp.where(strict, zz[c][:2 * C, :2 * C], 0.0) for c in chunks]
    l_ak = [jnp.where(strict, zz[c][:2 * C, 2 * C:], 0.0) for c in chunks]
    m_rbk = [jnp.concatenate([jnp.where(incl, zz[c][2 * C:, :2 * C], 0.0),
                              jnp.where(incl, zz[c][2 * C:, 2 * C:], 0.0)], axis=1) for c in chunks]
    lakv = [_mm(l_ak[c], vb[c]) for c in chunks]

    lp = [_mm(l_ab[c], l_ab[c]) for c in chunks]
    tinv = [eye + l_ab[c] for c in chunks]
    power = 2
    while 2 * power < C:
        both = [_mm(lp[c], jnp.concatenate([lp[c], tinv[c]], axis=1)) for c in chunks]
        tinv = [tinv[c] + both[c][:, 2 * C:] for c in chunks]
        lp = [both[c][:, :2 * C] for c in chunks]
        power *= 2
    tinv = [tinv[c] + _mm(lp[c], tinv[c]) for c in chunks]

    wu = [_mm(tinv[c], jnp.concatenate([xa[c], lakv[c]], axis=1)) for c in chunks]
    zero = jnp.zeros((2 * C, LANES), F32)
    for c in chunks:
        sl = slice(c * C, (c + 1) * C)
        rhs = jnp.concatenate([wu[c], jnp.concatenate([zero, vb[c]], axis=1)], axis=0)
        out = _mm(jnp.concatenate([m_rbk[c], lhs_t[c]], axis=0), rhs)
        rp = xr[c] + out[:2 * C, :LANES]
        yp = out[:2 * C, LANES:]
        rp_ref[sl, :] = (rp[:C] + rp[C:]).astype(rp_ref.dtype)
        yp_ref[sl, :] = yp[:C] + yp[C:]
        pm, qm = out[2 * C:, :LANES], out[2 * C:, LANES:]
        p_ref[c, 0] = jnp.where(head_a, pm[:C], pm[C:]) + jnp.where(diag, gam[c], 0.0)
        q_ref[c, 0] = jnp.where(head_a, qm[:C], qm[C:])


def _rwkv_chunks(prk, mu, params, wd, wa, wg, seq, tm):
    T = prk.shape[0]
    npair = 8
    zblk = 3 * LANES
    zcol = (3 * npair * LANES) // zblk
    hb = tm // RWKV_HALO

    def col(base):
        return pl.BlockSpec((tm, LANES), lambda i, j, base=base: (i, base + j))

    def halo(base):
        return pl.BlockSpec((RWKV_HALO, LANES), lambda i, j, base=base: (jnp.maximum(i * hb - 1, 0), base + j))

    def mucol(base):
        return pl.BlockSpec((1, LANES), lambda i, j, base=base: (0, base + j))

    in_specs = [col(0), col(npair), col(2 * npair), pl.BlockSpec((tm, zblk), lambda i, j: (i, zcol)),
                halo(0), halo(npair), halo(2 * npair),
                pl.BlockSpec((RWKV_HALO, zblk), lambda i, j: (jnp.maximum(i * hb - 1, 0), zcol)),
                mucol(0), mucol(npair), mucol(2 * npair), pl.BlockSpec((1, zblk), lambda i, j: (0, zcol)),
                pl.BlockSpec((8, LANES), lambda i, j: (0, j)),
                pl.BlockSpec((LANES, LANES), lambda i, j: (0, j)),
                pl.BlockSpec((LANES, LANES), lambda i, j: (0, j)),
                pl.BlockSpec((2 * LANES, LANES), lambda i, j: (0, j))]
    tok = pl.BlockSpec((tm, LANES), lambda i, j: (i, j))
    mat = pl.BlockSpec((tm // CHUNK, 1, CHUNK, LANES), lambda i, j: (i, j, 0, 0))
    tok_shape = jax.ShapeDtypeStruct((T, npair * LANES), F32)
    tok_bf16 = jax.ShapeDtypeStruct((T, npair * LANES), BF16)
    mat_shape = jax.ShapeDtypeStruct((T // CHUNK, npair, CHUNK, LANES), F32)
    return pl.pallas_call(
        functools.partial(_rwkv_chunk_kernel, seq=seq), grid=(T // tm, npair),
        in_specs=in_specs, out_specs=[tok, tok, tok, tok, mat, mat],
        out_shape=[tok_bf16, tok_shape, tok_bf16, tok_bf16, mat_shape, mat_shape],
        compiler_params=_cparams("parallel", "parallel"), name="rwkv_chunks")(
            prk, prk, prk, prk, prk, prk, prk, prk, mu, mu, mu, mu, params, wd, wa, wg)


def _rwkv_scan_kernel(rp_ref, yp_ref, p_ref, q_ref, y_ref, s_scr):
    nseq, npair = s_scr.shape[0], s_scr.shape[1]
    nchunk = p_ref.shape[1]
    C = CHUNK

    @pl.when(pl.program_id(0) == 0)
    def _():
        s_scr[...] = jnp.zeros(s_scr.shape, F32)

    head_a = lax.broadcasted_iota(jnp.int32, (C, LANES), 1) < HEAD_DIM

    def chunk(c, carry):
        rows = pl.ds(pl.multiple_of(c * C, C), C)
        units = [(b, j) for b in range(nseq) for j in range(npair)]
        s = [s_scr[b, j] for b, j in units]
        for (b, j), s_u in zip(units, s):
            cols = slice(j * LANES, (j + 1) * LANES)
            y_ref[b, rows, cols] = _mm(rp_ref[b, rows, cols], s_u) + yp_ref[b, rows, cols]
        for (b, j), s_u in zip(units, s):
            s_scr[b, j] = _mm3(_stack_heads(p_ref[b, c, j], head_a), s_u) + _stack_heads(q_ref[b, c, j], head_a)
        return carry

    lax.fori_loop(0, nchunk, chunk, 0)


def _rwkv_scan(rp, yp, p, q, batch, seq, tm):
    T, W = rp.shape
    npair = W // LANES
    rp, yp = rp.reshape(batch, seq, W), yp.reshape(batch, seq, W)
    p = p.reshape(batch, seq // CHUNK, npair, CHUNK, LANES)
    q = q.reshape(batch, seq // CHUNK, npair, CHUNK, LANES)
    tok = pl.BlockSpec((batch, tm, W), lambda i: (0, i, 0))
    mat = pl.BlockSpec((batch, tm // CHUNK, npair, CHUNK, LANES), lambda i: (0, i, 0, 0, 0))
    y = pl.pallas_call(
        _rwkv_scan_kernel, grid=(seq // tm,),
        in_specs=[tok, tok, mat, mat], out_specs=tok,
        out_shape=jax.ShapeDtypeStruct((batch, seq, W), F32),
        scratch_shapes=[pltpu.VMEM((batch, npair, LANES, LANES), F32)],
        compiler_params=_cparams("arbitrary"), name="rwkv_scan")(rp, yp, p, q)
    return y.reshape(T, W)


def _merge_kernel(x_ref, ya_ref, y_ref, bonus_ref, rg_ref, lnw_ref, lnb_ref, ga_ref, gr_ref,
                  wba_ref, wbr_ref, wo_ref, o_ref, yr_scr):
    ones_bd = _head_ones()
    inv_n = 1.0 / HEAD_DIM
    for j in range(y_ref.shape[1] // LANES):
        cols = slice(j * LANES, (j + 1) * LANES)
        y = y_ref[:, cols]
        yc = y - _mm(y, ones_bd) * inv_n
        var = _mm(yc * yc, ones_bd) * inv_n
        yn = yc * lax.rsqrt(var + GN_EPS) * lnw_ref[:, cols] + lnb_ref[:, cols]
        yr_scr[:, cols] = ((yn + bonus_ref[:, cols]) * rg_ref[:, cols]).astype(BF16)

    ba = jnp.dot(ya_ref[...], wba_ref[...], preferred_element_type=F32)
    br = jnp.dot(yr_scr[...], wbr_ref[...], preferred_element_type=F32)
    merged = ga_ref[...] * ba + gr_ref[...] * br
    o_ref[...] = x_ref[...] + jnp.dot(merged.astype(BF16), wo_ref[...], preferred_element_type=F32)


def _merge(x, ya, y, bonus, rgate, lnw, lnb, gates, wba, wbr, wo, tm):
    T, D = x.shape
    row = lambda w: pl.BlockSpec((tm, w), lambda i: (i, 0))
    full = lambda a: pl.BlockSpec(a.shape, lambda i: (0, 0))
    return pl.pallas_call(
        _merge_kernel, grid=(T // tm,),
        in_specs=[row(D), row(ya.shape[1]), row(D), row(D), row(D), full(lnw), full(lnb),
                  pl.BlockSpec((tm, D), lambda i: (i, 0)), pl.BlockSpec((tm, D), lambda i: (i, 1)),
                  full(wba), full(wbr), full(wo)],
        out_specs=row(D), out_shape=jax.ShapeDtypeStruct((T, D), F32),
        scratch_shapes=[pltpu.VMEM((tm, D), BF16)],
        compiler_params=_cparams("parallel"), name="merge_out")(
            x, ya, y, bonus, rgate, lnw, lnb, gates, gates, wba, wbr, wo)


FFN_HALO = 16


def _ffn_kernel(x_ref, xh_ref, g_ref, wg_ref, wv_ref, cwg_ref, cwv_ref, cbg_ref, cbv_ref, wd_ref, gf_ref,
                o_ref, h_scr, acc_scr, *, seq):
    i, j = pl.program_id(0), pl.program_id(1)
    tm = x_ref.shape[0]
    H = FFN_HALO

    def norm(x):
        ms = jnp.mean(x * x, axis=-1, keepdims=True)
        return x * lax.rsqrt(ms + RMS_EPS) * g_ref[...]

    @pl.when(j == 0)
    def _():
        first = (i * tm) % seq == 0
        h_scr[0:H, :] = jnp.where(first, 0.0, norm(xh_ref[...])).astype(BF16)
        h_scr[H:, :] = norm(x_ref[...]).astype(BF16)
        acc_scr[...] = jnp.zeros(acc_scr.shape, F32)

    h = h_scr[...]

    def conv(u, cw_ref, cb_ref):
        c = u * cw_ref[2:3, :] + pltpu.roll(u, 1, 0) * cw_ref[1:2, :] + pltpu.roll(u, 2, 0) * cw_ref[0:1, :]
        return c[H:] + cb_ref[...]

    cg = conv(jnp.dot(h, wg_ref[...], preferred_element_type=F32), cwg_ref, cbg_ref)
    cv = conv(jnp.dot(h, wv_ref[...], preferred_element_type=F32), cwv_ref, cbv_ref)
    act = cg * _sigmoid(cg) * cv
    acc_scr[...] += jnp.dot(act.astype(BF16), wd_ref[...], preferred_element_type=F32)

    @pl.when(j == pl.num_programs(1) - 1)
    def _():
        x2 = x_ref[...] + acc_scr[...]
        ms = jnp.mean(x2 * x2, axis=-1, keepdims=True)
        o_ref[...] = x2 * lax.rsqrt(ms + RMS_EPS) * gf_ref[...]


def _conv_ffn(x, g, w_up, conv_w, conv_b, w_down, g_final, seq, tm, tf):
    T, D = x.shape
    dff = w_down.shape[0]
    nj = dff // tf
    hb = tm // FFN_HALO
    in_specs = [pl.BlockSpec((tm, D), lambda i, j: (i, 0)),
                pl.BlockSpec((FFN_HALO, D), lambda i, j: (jnp.maximum(i * hb - 1, 0), 0)),
                pl.BlockSpec((1, D), lambda i, j: (0, 0)),
                pl.BlockSpec((D, tf), lambda i, j: (0, j)),
                pl.BlockSpec((D, tf), lambda i, j: (0, nj + j)),
                pl.BlockSpec((3, tf), lambda i, j: (0, j)),
                pl.BlockSpec((3, tf), lambda i, j: (0, nj + j)),
                pl.BlockSpec((1, tf), lambda i, j: (0, j)),
                pl.BlockSpec((1, tf), lambda i, j: (0, nj + j)),
                pl.BlockSpec((tf, D), lambda i, j: (j, 0)),
                pl.BlockSpec((1, D), lambda i, j: (0, 0))]
    return pl.pallas_call(
        functools.partial(_ffn_kernel, seq=seq), grid=(T // tm, nj), in_specs=in_specs,
        out_specs=pl.BlockSpec((tm, D), lambda i, j: (i, 0)),
        out_shape=jax.ShapeDtypeStruct((T, D), F32),
        scratch_shapes=[pltpu.VMEM((tm + FFN_HALO, D), BF16), pltpu.VMEM((tm, D), F32)],
        compiler_params=_cparams("parallel", "arbitrary"), name="conv_ffn")(
            x, x, g, w_up, w_up, conv_w, conv_w, conv_b, conv_b, w_down, g_final)


def _rope_tables(seq):
    half = ROPE_DIM // 2
    lane = jnp.arange(LANES, dtype=jnp.int32) % HEAD_DIM
    expo = (lane % half).astype(F32) * (2.0 / ROPE_DIM)
    freq = jnp.where(lane < ROPE_DIM, jnp.power(ROPE_THETA, -expo), 0.0)
    lo_n = 128
    hi = (jnp.arange(seq // lo_n, dtype=jnp.int32) * lo_n).astype(F32)[:, None] * freq[None, :]
    lo = jnp.arange(lo_n, dtype=jnp.int32).astype(F32)[:, None] * freq[None, :]
    ch, sh, cl, sl = jnp.cos(hi)[:, None, :], jnp.sin(hi)[:, None, :], jnp.cos(lo)[None], jnp.sin(lo)[None]
    cos = (ch * cl - sh * sl).reshape(seq, LANES)
    sin = (sh * cl + ch * sl).reshape(seq, LANES)
    sa = jnp.where((lane >= half) & (lane < ROPE_DIM), sin, 0.0)
    sb = jnp.where(lane < half, -sin, 0.0)
    return cos, sa, sb


def kernel(x, norm_mix_g, w_in, b_gate, mu_shift, w0, w_decay_up, a0, w_a_up, w_g_up, k_k, k_a, r_k,
           ln_x_w, ln_x_b, w_branch_attn, w_branch_rwkv, w_out, norm_ffn_g, w_ffn_up, conv_w, conv_b,
           w_ffn_down, norm_final_g):
    B, S, D = x.shape
    T = B * S
    depth = norm_mix_g.shape[0]
    attn_w = 3 * 4 * HEAD_DIM
    rw = D
    n_decay, n_iclr, n_gate = w_decay_up.shape[1], w_a_up.shape[1], w_g_up.shape[1]
    lora_w = n_decay + n_iclr + n_gate
    lora_pad = 3 * LANES
    c_rwkv = 3 * attn_w
    c_lora = c_rwkv + 3 * rw
    c_gate = c_lora + lora_w

    assert depth == 1 and S % ATTN_TILE == 0 and D == 16 * HEAD_DIM
    xt = x.reshape(T, D)
    cos, sa, sb = _rope_tables(S)
    for l in range(depth):
        wi = w_in[l]
        w_qkv = wi[:, :c_rwkv].astype(BF16)
        w_rkvz = wi[:, c_rwkv:c_lora + lora_pad].astype(BF16)
        w_gt = wi[:, c_gate:].astype(BF16)
        gmix = norm_mix_g[l].reshape(1, D)

        qkv, prk, gates = _in_proj(xt, gmix, w_qkv, w_rkvz, w_gt, b_gate[l].reshape(1, -1), (cos, sa, sb), S,
                                   1024, attn_w, (3 * rw + lora_pad) // 3, D, rope_tiles=2)

        y_attn = _attention(qkv, B, S)

        mu = jnp.concatenate([mu_shift[l], jnp.zeros((lora_pad - lora_w,), F32)]).reshape(1, -1)
        params = jnp.concatenate(
            [jnp.stack([w0[l], a0[l], k_k[l], k_a[l], r_k[l].reshape(-1)]), jnp.zeros((3, rw), F32)], axis=0)
        wd = jnp.concatenate([w_decay_up[l], jnp.zeros((LANES - n_decay, rw), F32)], axis=0).astype(BF16)
        wa = jnp.concatenate([jnp.zeros((n_decay, rw), F32), w_a_up[l],
                              jnp.zeros((LANES - n_decay - n_iclr, rw), F32)], axis=0).astype(BF16)
        wg = jnp.concatenate([w_g_up[l], jnp.zeros((2 * LANES - n_gate, rw), F32)], axis=0).astype(BF16)
        rp, yp, bonus, gate, pm, qm = _rwkv_chunks(prk, mu, params, wd, wa, wg, S, 1024)
        y_scan = _rwkv_scan(rp, yp, pm, qm, B, S, 256)

        xt = _merge(xt, y_attn, y_scan, bonus, gate, ln_x_w[l].reshape(1, -1), ln_x_b[l].reshape(1, -1),
                    gates, w_branch_attn[l].astype(BF16),
                    w_branch_rwkv[l].astype(BF16), w_out[l].astype(BF16), 512)

        xt = _conv_ffn(xt, norm_ffn_g[l].reshape(1, D), w_ffn_up[l].astype(BF16), conv_w[l].reshape(3, -1),
                       conv_b[l].reshape(1, -1), w_ffn_down[l].astype(BF16), norm_final_g.reshape(1, D),
                       S, 512, w_ffn_down.shape[1] // 2)
    return xt.reshape(B, S, D)
```

```python
import functools

import jax
import jax.numpy as jnp
from jax import lax
from jax.experimental import pallas as pl
from jax.experimental.pallas import tpu as pltpu

F32 = jnp.float32
BF16 = jnp.bfloat16

HEAD_DIM = 64
LANES = 128
MXU_TILE = 256
ATTN_GROUPS = ((128, 1), (512, 4), (2048, 16))
ATTN_SPAN = 128
ATTN_TILE = 2048
ATTN_UNROLL = 4
ROPE_THETA = 500000.0
ROPE_DIM = HEAD_DIM // 4
RMS_EPS = 1e-6
GN_EPS = 64e-5
CHUNK = 64
RWKV_HALO = 16
NEG_BIG = -1e30
VMEM_LIMIT = 56 * 1024 * 1024


def _cparams(*sem):
    return pltpu.CompilerParams(dimension_semantics=sem, vmem_limit_bytes=VMEM_LIMIT)


def _mm(a, b):
    return jnp.dot(a.astype(BF16), b.astype(BF16), preferred_element_type=F32)


def _mm_nt(a, b):
    return lax.dot_general(a.astype(BF16), b.astype(BF16), (((1,), (1,)), ((), ())),
                           preferred_element_type=F32)


def _mm_tn(a, b):
    return lax.dot_general(a.astype(BF16), b.astype(BF16), (((0,), (0,)), ((), ())),
                           preferred_element_type=F32)


def _split3(a):
    hi = a.astype(BF16)
    r1 = a - hi.astype(F32)
    mid = r1.astype(BF16)
    lo = (r1 - mid.astype(F32)).astype(BF16)
    return hi, mid, lo


def _mm_exact_rhs(a, b_bf16):
    hi, mid, lo = _split3(a)
    out = jnp.dot(lo, b_bf16, preferred_element_type=F32)
    out = out + jnp.dot(mid, b_bf16, preferred_element_type=F32)
    return out + jnp.dot(hi, b_bf16, preferred_element_type=F32)


def _mm_exact_lhs(a_bf16, b):
    hi, mid, lo = _split3(b)
    out = jnp.dot(a_bf16, lo, preferred_element_type=F32)
    out = out + jnp.dot(a_bf16, mid, preferred_element_type=F32)
    return out + jnp.dot(a_bf16, hi, preferred_element_type=F32)


def _mm3(a, b):
    ah = a.astype(BF16)
    al = (a - ah.astype(F32)).astype(BF16)
    bh = b.astype(BF16)
    bl = (b - bh.astype(F32)).astype(BF16)
    out = jnp.dot(al, bh, preferred_element_type=F32)
    out = out + jnp.dot(ah, bl, preferred_element_type=F32)
    return out + jnp.dot(ah, bh, preferred_element_type=F32)


def _sigmoid(z):
    return 1.0 / (1.0 + jnp.exp(-z))


def _head_ones():
    r = jnp.where(lax.broadcasted_iota(jnp.int32, (LANES, LANES), 0) < HEAD_DIM, 0, 1)
    c = jnp.where(lax.broadcasted_iota(jnp.int32, (LANES, LANES), 1) < HEAD_DIM, 0, 1)
    return jnp.where(r == c, 1.0, 0.0).astype(BF16)


def _in_proj_kernel(x_ref, g_ref, wq_ref, wr_ref, wg_ref, bg_ref, cos_ref, sa_ref, sb_ref,
                    oq_ref, or_ref, og_ref, *, tq, tr, tg, rope_tiles):
    x = x_ref[...]
    ms = jnp.mean(x * x, axis=-1, keepdims=True)
    h = (x * lax.rsqrt(ms + RMS_EPS) * g_ref[...]).astype(BF16)

    half = ROPE_DIM // 2
    cos, sa, sb = cos_ref[...], sa_ref[...], sb_ref[...]
    for t in range(wq_ref.shape[1] // tq):
        cols = slice(t * tq, (t + 1) * tq)
        o = jnp.dot(h, wq_ref[:, cols], preferred_element_type=F32)
        if t >= rope_tiles:
            oq_ref[:, cols] = o
            continue
        for c in range(tq // LANES):
            ob = o[:, c * LANES:(c + 1) * LANES]
            oq_ref[:, t * tq + c * LANES:t * tq + (c + 1) * LANES] = (
                ob * cos + pltpu.roll(ob, half, 1) * sa + pltpu.roll(ob, LANES - half, 1) * sb)

    for t in range(wr_ref.shape[1] // tr):
        cols = slice(t * tr, (t + 1) * tr)
        or_ref[:, cols] = jnp.dot(h, wr_ref[:, cols], preferred_element_type=F32).astype(or_ref.dtype)

    for t in range(wg_ref.shape[1] // tg):
        cols = slice(t * tg, (t + 1) * tg)
        o = jnp.dot(h, wg_ref[:, cols], preferred_element_type=F32)
        og_ref[:, cols] = _sigmoid(o + bg_ref[:, cols]).astype(og_ref.dtype)


def _in_proj(x, g, w_qkv, w_rkvz, w_gate, b_gate, rope, seq, tm, tq, tr, tg, rope_tiles):
    T, D = x.shape
    cos, sa, sb = rope
    nseq = seq // tm
    resident = lambda a: pl.BlockSpec(a.shape, lambda i: (0, 0), pipeline_mode=pl.Buffered(1))
    row = lambda w: pl.BlockSpec((tm, w), lambda i: (i, 0))
    t_spec = pl.BlockSpec((tm, LANES), lambda i: (i % nseq, 0))
    in_specs = [row(D), resident(g), resident(w_qkv), resident(w_rkvz), resident(w_gate), resident(b_gate),
                t_spec, t_spec, t_spec]
    outs = [(w_qkv.shape[1], F32), (w_rkvz.shape[1], BF16), (w_gate.shape[1], BF16)]
    return pl.pallas_call(
        functools.partial(_in_proj_kernel, tq=tq, tr=tr, tg=tg, rope_tiles=rope_tiles),
        grid=(T // tm,), in_specs=in_specs, out_specs=[row(w) for w, _ in outs],
        out_shape=[jax.ShapeDtypeStruct((T, w), dt) for w, dt in outs],
        compiler_params=_cparams("parallel"), name="in_proj")(
            x, g, w_qkv, w_rkvz, w_gate, b_gate, cos, sa, sb)


def _attn_kernel(q0, q1, q2, k0, k1, k2, v0, v1, v2, kp0, kp1, kp2, vp0, vp1, vp2, o_ref, obuf, mbuf, lbuf):
    tile = pl.program_id(2)
    q_refs, k_refs, v_refs = (q0, q1, q2), (k0, k1, k2), (v0, v1, v2)
    kp_refs, vp_refs = (kp0, kp1, kp2), (vp0, vp1, vp2)
    L = ATTN_SPAN
    TL = ATTN_TILE
    U = ATTN_UNROLL

    lane = lax.broadcasted_iota(jnp.int32, (L, LANES), 1)
    head_a = lane < HEAD_DIM
    qi = lax.broadcasted_iota(jnp.int32, (L, 2 * L), 0)
    kj = lax.broadcasted_iota(jnp.int32, (L, 2 * L), 1)
    rel = L + qi - kj
    band = (rel >= 0) & (rel <= L)
    band_first = band & (kj >= jnp.where(tile > 0, 0, L))

    def run_blocks(g, d, blocks):
        qs, kcs, vcs, valids = [], [], [], []
        for start, from_prev in blocks:
            rows = pl.ds(start, L, stride=d)
            qs.append(q_refs[g][rows, :] * (HEAD_DIM ** -0.5))
            if from_prev:
                k1, v1 = kp_refs[g][rows, :], vp_refs[g][rows, :]
            else:
                back = pl.ds(start - L * d, L, stride=d)
                k1, v1 = k_refs[g][back, :], v_refs[g][back, :]
            kcs.append(jnp.concatenate([k1, k_refs[g][rows, :]], axis=0).astype(BF16))
            vcs.append(jnp.concatenate([v1, v_refs[g][rows, :]], axis=0).astype(BF16))
            valids.append(band_first if from_prev else band)
        nb = range(len(blocks))
        qh = [[jnp.where(head_a, qs[b], 0.0).astype(BF16), jnp.where(head_a, 0.0, qs[b]).astype(BF16)] for b in nb]
        s = [[lax.dot_general(qh[b][h], kcs[b], (((1,), (1,)), ((), ())), preferred_element_type=F32)
              for h in range(2)] for b in nb]
        s = [[jnp.where(valids[b], s[b][h], NEG_BIG) for h in range(2)] for b in nb]
        m = [[jnp.max(s[b][h], axis=1, keepdims=True) for h in range(2)] for b in nb]
        e = [[jnp.exp(s[b][h] - m[b][h]) for h in range(2)] for b in nb]
        l = [[jnp.sum(e[b][h], axis=1, keepdims=True) for h in range(2)] for b in nb]
        o = [[jnp.dot(e[b][h].astype(BF16), vcs[b], preferred_element_type=F32) for h in range(2)] for b in nb]
        for b, (start, _) in enumerate(blocks):
            rows = pl.ds(start, L, stride=d)
            obuf[g, rows, :] = jnp.where(head_a, o[b][0], o[b][1])
            mbuf[g, rows, :] = jnp.where(head_a, jnp.broadcast_to(m[b][0], (L, LANES)),
                                         jnp.broadcast_to(m[b][1], (L, LANES)))
            lbuf[g, rows, :] = jnp.where(head_a, jnp.broadcast_to(l[b][0], (L, LANES)),
                                         jnp.broadcast_to(l[b][1], (L, LANES)))

    nblk = TL // L
    for g, (_, d) in enumerate(ATTN_GROUPS):
        def cur_body(it, carry, g=g, d=d):
            idxs = [it * U + u for u in range(U)]
            run_blocks(g, d, [((idx % d) + (L * d) * (idx // d), False) for idx in idxs])
            return carry

        if d >= U:
            def prev_body(it, carry, g=g, d=d):
                run_blocks(g, d, [(it * U + u, True) for u in range(U)])
                return carry

            lax.fori_loop(0, d // U, prev_body, 0)
            if d < nblk:
                lax.fori_loop(d // U, nblk // U, cur_body, 0)
        else:
            run_blocks(g, d, [((idx % d) + (L * d) * (idx // d), idx // d == 0) for idx in range(U)])
            lax.fori_loop(1, nblk // U, cur_body, 0)

    MR = 256

    def merge(i, carry):
        rows = pl.ds(pl.multiple_of(i * MR, MR), MR)
        m0, m1, m2 = mbuf[0, rows, :], mbuf[1, rows, :], mbuf[2, rows, :]
        mx = jnp.maximum(jnp.maximum(m0, m1), m2)
        w0, w1, w2 = jnp.exp(m0 - mx), jnp.exp(m1 - mx), jnp.exp(m2 - mx)
        num = w0 * obuf[0, rows, :] + w1 * obuf[1, rows, :] + w2 * obuf[2, rows, :]
        den = w0 * lbuf[0, rows, :] + w1 * lbuf[1, rows, :] + w2 * lbuf[2, rows, :]
        o_ref[rows, :] = (num / den).astype(o_ref.dtype)
        return carry

    lax.fori_loop(0, TL // MR, merge, 0)


def _attention(qkv, batch, seq):
    T = qkv.shape[0]
    TL = ATTN_TILE
    tiles = seq // TL
    npair = 2
    qcols = 3 * 4 * HEAD_DIM // LANES

    def spec(section, g):
        return pl.BlockSpec((TL, LANES),
                            lambda p, b, t, section=section, g=g: (b * tiles + t, section * qcols + 2 * g + p))

    def prev_spec(section, g):
        rows = ATTN_SPAN * ATTN_GROUPS[g][1]
        per_tile = TL // rows
        return pl.BlockSpec(
            (rows, LANES),
            lambda p, b, t, section=section, g=g: (jnp.maximum((b * tiles + t) * per_tile - 1, 0),
                                                   section * qcols + 2 * g + p))

    in_specs = ([spec(s, g) for s in range(3) for g in range(3)]
                + [prev_spec(s, g) for s in (1, 2) for g in range(3)])
    out_spec = pl.BlockSpec((TL, LANES), lambda p, b, t: (b * tiles + t, p))
    return pl.pallas_call(
        _attn_kernel, grid=(npair, batch, tiles), in_specs=in_specs, out_specs=out_spec,
        out_shape=jax.ShapeDtypeStruct((T, npair * LANES), BF16),
        scratch_shapes=[pltpu.VMEM((3, TL, LANES), F32), pltpu.VMEM((3, TL, LANES), F32),
                        pltpu.VMEM((3, TL, LANES), F32)],
        compiler_params=_cparams("parallel", "parallel", "parallel"), name="dilated_attn")(*([qkv] * 15))


def _stack_heads(x, head_a):
    return jnp.concatenate([jnp.where(head_a, x, 0.0), jnp.where(head_a, 0.0, x)], axis=0)


def _rwkv_chunk_kernel(r_ref, k_ref, v_ref, z_ref, rh_ref, kh_ref, vh_ref, zh_ref,
                       mur_ref, muk_ref, muv_ref, muz_ref, par_ref, wd_ref, wa_ref, wg_ref,
                       rp_ref, yp_ref, bonus_ref, gate_ref, p_ref, q_ref, *, seq):
    tm = r_ref.shape[0]
    C = CHUNK
    first = (pl.program_id(0) * tm) % seq == 0
    row = lax.broadcasted_iota(jnp.int32, (tm, 1), 0)

    def shifted(ref, halo_ref, mu_ref):
        p = ref[...].astype(F32)
        hal = jnp.where(first, 0.0, halo_ref[RWKV_HALO - 1:RWKV_HALO, :].astype(F32))
        prev = jnp.where(row == 0, hal, pltpu.roll(p, 1, 0))
        return p + (prev - p) * mu_ref[...]

    z = shifted(z_ref, zh_ref, muz_ref)
    r = shifted(r_ref, rh_ref, mur_ref)
    k = shifted(k_ref, kh_ref, muk_ref)
    v = shifted(v_ref, vh_ref, muv_ref)

    w0, a0, k_k, k_a, r_k = (par_ref[i:i + 1, :] for i in range(5))
    zwa = z[:, 0:LANES]
    dpre = w0 + _mm(jnp.tanh(zwa), wd_ref[...])
    w_log = -(jnp.maximum(-dpre, 0.0) + jnp.log(1.0 + jnp.exp(-jnp.abs(dpre)))) - 0.5
    lw = -jnp.exp(w_log)
    a = _sigmoid(a0 + _mm(zwa, wa_ref[...]))
    gate_ref[...] = _mm(_sigmoid(z[:, LANES:3 * LANES]), wg_ref[...]).astype(gate_ref.dtype)

    ones_bd = _head_ones()
    kk = k * k_k
    kk = kk / jnp.maximum(jnp.sqrt(_mm(kk * kk, ones_bd)), 1e-12)
    k2 = k * (1.0 + (a - 1.0) * k_a)
    bonus_ref[...] = (_mm(r * k2 * r_k, ones_bd) * v).astype(bonus_ref.dtype)
    beta = kk * a

    ti = lax.broadcasted_iota(jnp.int32, (C, C), 0)
    si = lax.broadcasted_iota(jnp.int32, (C, C), 1)
    tril = jnp.where(si <= ti, 1.0, 0.0).astype(BF16)
    t2 = lax.broadcasted_iota(jnp.int32, (2 * C, 2 * C), 0)
    s2 = lax.broadcasted_iota(jnp.int32, (2 * C, 2 * C), 1)
    same = jnp.where(t2 < C, 0, 1) == jnp.where(s2 < C, 0, 1)
    strict = same & (s2 < t2)
    incl = same & (s2 <= t2)
    eye = jnp.where(t2 == s2, 1.0, 0.0)
    lane = lax.broadcasted_iota(jnp.int32, (C, LANES), 1)
    head_a = lane < HEAD_DIM
    krow = lax.broadcasted_iota(jnp.int32, (C, LANES), 0)
    diag = (lane == krow) | (lane == krow + HEAD_DIM)

    chunks = range(tm // C)
    xa, xr, yy, lhs_t, vb, gam, zz = [], [], [], [], [], [], []
    for c in chunks:
        sl = slice(c * C, (c + 1) * C)
        lw_c = lw[sl]
        cum_c = _mm_exact_lhs(tril, lw_c)
        tot = cum_c[C - 1:C, :]
        g_inc = jnp.exp(cum_c)
        g_prev = jnp.exp(cum_c - lw_c)
        g_inv = jnp.exp(-cum_c)
        g_end = jnp.exp(tot - cum_c)
        gam.append(jnp.exp(tot))
        xa.append(_stack_heads(-kk[sl] * g_prev, head_a))
        xr.append(_stack_heads(r[sl] * g_inc, head_a))
        yb, yk = beta[sl] * g_inv, k2[sl] * g_inv
        yy.append(jnp.concatenate([yb, yb, yk, yk], axis=0))
        bh, kh = beta[sl] * g_end, k2[sl] * g_end
        lhs_t.append(jnp.concatenate([bh, bh, kh, kh], axis=0).T)
        vb.append(_stack_heads(v[sl], head_a))
        zz.append(_mm_nt(jnp.concatenate([xa[c], xr[c]], axis=0), yy[c]))

    l_ab = [jnp.where(strict, zz[c][:2 * C, :2 * C], 0.0) for c in chunks]
    l_ak = [jnp.where(strict, zz[c][:2 * C, 2 * C:], 0.0) for c in chunks]
    m_rbk = [jnp.concatenate([jnp.where(incl, zz[c][2 * C:, :2 * C], 0.0),
                              jnp.where(incl, zz[c][2 * C:, 2 * C:], 0.0)], axis=1) for c in chunks]
    lakv = [_mm(l_ak[c], vb[c]) for c in chunks]

    lp = [_mm(l_ab[c], l_ab[c]) for c in chunks]
    tinv = [eye + l_ab[c] for c in chunks]
    power = 2
    while 2 * power < C:
        both = [_mm(lp[c], jnp.concatenate([lp[c], tinv[c]], axis=1)) for c in chunks]
        tinv = [tinv[c] + both[c][:, 2 * C:] for c in chunks]
        lp = [both[c][:, :2 * C] for c in chunks]
        power *= 2
    tinv = [tinv[c] + _mm(lp[c], tinv[c]) for c in chunks]

    wu = [_mm(tinv[c], jnp.concatenate([xa[c], lakv[c]], axis=1)) for c in chunks]
    zero = jnp.zeros((2 * C, LANES), F32)
    for c in chunks:
        sl = slice(c * C, (c + 1) * C)
        rhs = jnp.concatenate([wu[c], jnp.concatenate([zero, vb[c]], axis=1)], axis=0)
        out = _mm(jnp.concatenate([m_rbk[c], lhs_t[c]], axis=0), rhs)
        rp = xr[c] + out[:2 * C, :LANES]
        yp = out[:2 * C, LANES:]
        rp_ref[sl, :] = (rp[:C] + rp[C:]).astype(rp_ref.dtype)
        yp_ref[sl, :] = yp[:C] + yp[C:]
        pm, qm = out[2 * C:, :LANES], out[2 * C:, LANES:]
        p_ref[c, 0] = jnp.where(head_a, pm[:C], pm[C:]) + jnp.where(diag, gam[c], 0.0)
        q_ref[c, 0] = jnp.where(head_a, qm[:C], qm[C:])


def _rwkv_chunks(prk, mu, params, wd, wa, wg, seq, tm):
    T = prk.shape[0]
    npair = 8
    zblk = 3 * LANES
    zcol = (3 * npair * LANES) // zblk
    hb = tm // RWKV_HALO

    def col(base):
        return pl.BlockSpec((tm, LANES), lambda i, j, base=base: (i, base + j))

    def halo(base):
        return pl.BlockSpec((RWKV_HALO, LANES), lambda i, j, base=base: (jnp.maximum(i * hb - 1, 0), base + j))

    def mucol(base):
        return pl.BlockSpec((1, LANES), lambda i, j, base=base: (0, base + j))

    in_specs = [col(0), col(npair), col(2 * npair), pl.BlockSpec((tm, zblk), lambda i, j: (i, zcol)),
                halo(0), halo(npair), halo(2 * npair),
                pl.BlockSpec((RWKV_HALO, zblk), lambda i, j: (jnp.maximum(i * hb - 1, 0), zcol)),
                mucol(0), mucol(npair), mucol(2 * npair), pl.BlockSpec((1, zblk), lambda i, j: (0, zcol)),
                pl.BlockSpec((8, LANES), lambda i, j: (0, j)),
                pl.BlockSpec((LANES, LANES), lambda i, j: (0, j)),
                pl.BlockSpec((LANES, LANES), lambda i, j: (0, j)),
                pl.BlockSpec((2 * LANES, LANES), lambda i, j: (0, j))]
    tok = pl.BlockSpec((tm, LANES), lambda i, j: (i, j))
    mat = pl.BlockSpec((tm // CHUNK, 1, CHUNK, LANES), lambda i, j: (i, j, 0, 0))
    tok_shape = jax.ShapeDtypeStruct((T, npair * LANES), F32)
    tok_bf16 = jax.ShapeDtypeStruct((T, npair * LANES), BF16)
    mat_shape = jax.ShapeDtypeStruct((T // CHUNK, npair, CHUNK, LANES), F32)
    return pl.pallas_call(
        functools.partial(_rwkv_chunk_kernel, seq=seq), grid=(T // tm, npair),
        in_specs=in_specs, out_specs=[tok, tok, tok, tok, mat, mat],
        out_shape=[tok_bf16, tok_shape, tok_bf16, tok_bf16, mat_shape, mat_shape],
        compiler_params=_cparams("parallel", "parallel"), name="rwkv_chunks")(
            prk, prk, prk, prk, prk, prk, prk, prk, mu, mu, mu, mu, params, wd, wa, wg)


def _rwkv_scan_kernel(rp_ref, yp_ref, p_ref, q_ref, y_ref, s_scr):
    nseq, npair = s_scr.shape[0], s_scr.shape[1]
    nchunk = p_ref.shape[1]
    C = CHUNK

    @pl.when(pl.program_id(0) == 0)
    def _():
        s_scr[...] = jnp.zeros(s_scr.shape, F32)

    head_a = lax.broadcasted_iota(jnp.int32, (C, LANES), 1) < HEAD_DIM

    def chunk(c, carry):
        rows = pl.ds(pl.multiple_of(c * C, C), C)
        units = [(b, j) for b in range(nseq) for j in range(npair)]
        s = [s_scr[b, j] for b, j in units]
        for (b, j), s_u in zip(units, s):
            cols = slice(j * LANES, (j + 1) * LANES)
            y_ref[b, rows, cols] = _mm(rp_ref[b, rows, cols], s_u) + yp_ref[b, rows, cols]
        for (b, j), s_u in zip(units, s):
            s_scr[b, j] = _mm3(_stack_heads(p_ref[b, c, j], head_a), s_u) + _stack_heads(q_ref[b, c, j], head_a)
        return carry

    lax.fori_loop(0, nchunk, chunk, 0)


def _rwkv_scan(rp, yp, p, q, batch, seq, tm):
    T, W = rp.shape
    npair = W // LANES
    rp, yp = rp.reshape(batch, seq, W), yp.reshape(batch, seq, W)
    p = p.reshape(batch, seq // CHUNK, npair, CHUNK, LANES)
    q = q.reshape(batch, seq // CHUNK, npair, CHUNK, LANES)
    tok = pl.BlockSpec((batch, tm, W), lambda i: (0, i, 0))
    mat = pl.BlockSpec((batch, tm // CHUNK, npair, CHUNK, LANES), lambda i: (0, i, 0, 0, 0))
    y = pl.pallas_call(
        _rwkv_scan_kernel, grid=(seq // tm,),
        in_specs=[tok, tok, mat, mat], out_specs=tok,
        out_shape=jax.ShapeDtypeStruct((batch, seq, W), F32),
        scratch_shapes=[pltpu.VMEM((batch, npair, LANES, LANES), F32)],
        compiler_params=_cparams("arbitrary"), name="rwkv_scan")(rp, yp, p, q)
    return y.reshape(T, W)


def _merge_kernel(x_ref, ya_ref, y_ref, bonus_ref, rg_ref, lnw_ref, lnb_ref, ga_ref, gr_ref,
                  wba_ref, wbr_ref, wo_ref, o_ref, yr_scr):
    ones_bd = _head_ones()
    inv_n = 1.0 / HEAD_DIM
    for j in range(y_ref.shape[1] // LANES):
        cols = slice(j * LANES, (j + 1) * LANES)
        y = y_ref[:, cols]
        yc = y - _mm(y, ones_bd) * inv_n
        var = _mm(yc * yc, ones_bd) * inv_n
        yn = yc * lax.rsqrt(var + GN_EPS) * lnw_ref[:, cols] + lnb_ref[:, cols]
        yr_scr[:, cols] = ((yn + bonus_ref[:, cols]) * rg_ref[:, cols]).astype(BF16)

    ba = jnp.dot(ya_ref[...], wba_ref[...], preferred_element_type=F32)
    br = jnp.dot(yr_scr[...], wbr_ref[...], preferred_element_type=F32)
    merged = ga_ref[...] * ba + gr_ref[...] * br
    o_ref[...] = x_ref[...] + jnp.dot(merged.astype(BF16), wo_ref[...], preferred_element_type=F32)


def _merge(x, ya, y, bonus, rgate, lnw, lnb, gates, wba, wbr, wo, tm):
    T, D = x.shape
    row = lambda w: pl.BlockSpec((tm, w), lambda i: (i, 0))
    full = lambda a: pl.BlockSpec(a.shape, lambda i: (0, 0))
    return pl.pallas_call(
        _merge_kernel, grid=(T // tm,),
        in_specs=[row(D), row(ya.shape[1]), row(D), row(D), row(D), full(lnw), full(lnb),
                  pl.BlockSpec((tm, D), lambda i: (i, 0)), pl.BlockSpec((tm, D), lambda i: (i, 1)),
                  full(wba), full(wbr), full(wo)],
        out_specs=row(D), out_shape=jax.ShapeDtypeStruct((T, D), F32),
        scratch_shapes=[pltpu.VMEM((tm, D), BF16)],
        compiler_params=_cparams("parallel"), name="merge_out")(
            x, ya, y, bonus, rgate, lnw, lnb, gates, gates, wba, wbr, wo)


FFN_HALO = 16


def _ffn_kernel(x_ref, xh_ref, g_ref, wu_ref, cw_ref, cb_ref, wd_ref, gf_ref, o_ref, act_scr, *, seq, pieces):
    tm = x_ref.shape[0]
    H = FFN_HALO
    dff = wd_ref.shape[0]

    def norm(x):
        ms = jnp.mean(x * x, axis=-1, keepdims=True)
        return x * lax.rsqrt(ms + RMS_EPS) * g_ref[...]

    first = (pl.program_id(0) * tm) % seq == 0
    h = jnp.concatenate([jnp.where(first, 0.0, norm(xh_ref[...])).astype(BF16),
                         norm(x_ref[...]).astype(BF16)], axis=0)

    def conv(cols):
        u = jnp.dot(h, wu_ref[:, cols], preferred_element_type=F32)
        c = (u * cw_ref[2:3, cols] + pltpu.roll(u, 1, 0) * cw_ref[1:2, cols]
             + pltpu.roll(u, 2, 0) * cw_ref[0:1, cols])
        return c[H:] + cb_ref[:, cols]

    for lo, hi in pieces:
        cg = conv(slice(lo, hi))
        cv = conv(slice(dff + lo, dff + hi))
        act_scr[:, lo:hi] = (cg * _sigmoid(cg) * cv).astype(BF16)

    x2 = x_ref[...] + jnp.dot(act_scr[...], wd_ref[...], preferred_element_type=F32)
    ms = jnp.mean(x2 * x2, axis=-1, keepdims=True)
    o_ref[...] = x2 * lax.rsqrt(ms + RMS_EPS) * gf_ref[...]


def _conv_ffn(x, g, w_up, conv_w, conv_b, w_down, g_final, seq, tm):
    T, D = x.shape
    dff = w_down.shape[0]
    hb = tm // FFN_HALO
    mid = (dff // 2 + MXU_TILE - 1) // MXU_TILE * MXU_TILE
    pieces = ((0, mid), (mid, dff))
    resident = lambda a: pl.BlockSpec(a.shape, lambda i: (0, 0), pipeline_mode=pl.Buffered(1))
    in_specs = [pl.BlockSpec((tm, D), lambda i: (i, 0)),
                pl.BlockSpec((FFN_HALO, D), lambda i: (jnp.maximum(i * hb - 1, 0), 0)),
                resident(g), resident(w_up), resident(conv_w), resident(conv_b), resident(w_down),
                resident(g_final)]
    return pl.pallas_call(
        functools.partial(_ffn_kernel, seq=seq, pieces=pieces), grid=(T // tm,), in_specs=in_specs,
        out_specs=pl.BlockSpec((tm, D), lambda i: (i, 0)),
        out_shape=jax.ShapeDtypeStruct((T, D), F32),
        scratch_shapes=[pltpu.VMEM((tm, dff), BF16)],
        compiler_params=_cparams("parallel"), name="conv_ffn")(
            x, x, g, w_up, conv_w, conv_b, w_down, g_final)


def _rope_tables(seq):
    half = ROPE_DIM // 2
    lane = jnp.arange(LANES, dtype=jnp.int32) % HEAD_DIM
    expo = (lane % half).astype(F32) * (2.0 / ROPE_DIM)
    freq = jnp.where(lane < ROPE_DIM, jnp.power(ROPE_THETA, -expo), 0.0)
    lo_n = 128
    hi = (jnp.arange(seq // lo_n, dtype=jnp.int32) * lo_n).astype(F32)[:, None] * freq[None, :]
    lo = jnp.arange(lo_n, dtype=jnp.int32).astype(F32)[:, None] * freq[None, :]
    ch, sh, cl, sl = jnp.cos(hi)[:, None, :], jnp.sin(hi)[:, None, :], jnp.cos(lo)[None], jnp.sin(lo)[None]
    cos = (ch * cl - sh * sl).reshape(seq, LANES)
    sin = (sh * cl + ch * sl).reshape(seq, LANES)
    sa = jnp.where((lane >= half) & (lane < ROPE_DIM), sin, 0.0)
    sb = jnp.where(lane < half, -sin, 0.0)
    return cos, sa, sb


def kernel(x, norm_mix_g, w_in, b_gate, mu_shift, w0, w_decay_up, a0, w_a_up, w_g_up, k_k, k_a, r_k,
           ln_x_w, ln_x_b, w_branch_attn, w_branch_rwkv, w_out, norm_ffn_g, w_ffn_up, conv_w, conv_b,
           w_ffn_down, norm_final_g):
    B, S, D = x.shape
    T = B * S
    depth = norm_mix_g.shape[0]
    attn_w = 3 * 4 * HEAD_DIM
    rw = D
    n_decay, n_iclr, n_gate = w_decay_up.shape[1], w_a_up.shape[1], w_g_up.shape[1]
    lora_w = n_decay + n_iclr + n_gate
    lora_pad = 3 * LANES
    c_rwkv = 3 * attn_w
    c_lora = c_rwkv + 3 * rw
    c_gate = c_lora + lora_w

    assert depth == 1 and S % ATTN_TILE == 0 and D == 16 * HEAD_DIM
    xt = x.reshape(T, D)
    cos, sa, sb = _rope_tables(S)
    for l in range(depth):
        wi = w_in[l]
        w_qkv = wi[:, :c_rwkv].astype(BF16)
        w_rkvz = wi[:, c_rwkv:c_lora + lora_pad].astype(BF16)
        w_gt = wi[:, c_gate:].astype(BF16)
        gmix = norm_mix_g[l].reshape(1, D)

        qkv, prk, gates = _in_proj(xt, gmix, w_qkv, w_rkvz, w_gt, b_gate[l].reshape(1, -1), (cos, sa, sb), S,
                                   512, attn_w, 3 * rw + lora_pad, D, rope_tiles=2)

        y_attn = _attention(qkv, B, S)

        mu = jnp.concatenate([mu_shift[l], jnp.zeros((lora_pad - lora_w,), F32)]).reshape(1, -1)
        params = jnp.concatenate(
            [jnp.stack([w0[l], a0[l], k_k[l], k_a[l], r_k[l].reshape(-1)]), jnp.zeros((3, rw), F32)], axis=0)
        wd = jnp.concatenate([w_decay_up[l], jnp.zeros((LANES - n_decay, rw), F32)], axis=0).astype(BF16)
        wa = jnp.concatenate([jnp.zeros((n_decay, rw), F32), w_a_up[l],
                              jnp.zeros((LANES - n_decay - n_iclr, rw), F32)], axis=0).astype(BF16)
        wg = jnp.concatenate([w_g_up[l], jnp.zeros((2 * LANES - n_gate, rw), F32)], axis=0).astype(BF16)
        rp, yp, bonus, gate, pm, qm = _rwkv_chunks(prk, mu, params, wd, wa, wg, S, 1024)
        y_scan = _rwkv_scan(rp, yp, pm, qm, B, S, 256)

        xt = _merge(xt, y_attn, y_scan, bonus, gate, ln_x_w[l].reshape(1, -1), ln_x_b[l].reshape(1, -1),
                    gates, w_branch_attn[l].astype(BF16),
                    w_branch_rwkv[l].astype(BF16), w_out[l].astype(BF16), 512)

        xt = _conv_ffn(xt, norm_ffn_g[l].reshape(1, D), w_ffn_up[l].astype(BF16), conv_w[l].reshape(3, -1),
                       conv_b[l].reshape(1, -1), w_ffn_down[l].astype(BF16), norm_final_g.reshape(1, D),
                       S, 512)
    return xt.reshape(B, S, D)
```

```python
import functools

import jax
import jax.numpy as jnp
from jax import lax
from jax.experimental import pallas as pl
from jax.experimental.pallas import tpu as pltpu

F32 = jnp.float32
BF16 = jnp.bfloat16

HEAD_DIM = 64
LANES = 128
MXU_TILE = 256
ATTN_GROUPS = ((128, 1), (512, 4), (2048, 16))
ATTN_SPAN = 128
ATTN_TILE = 2048
ATTN_UNROLL = 4
ROPE_THETA = 500000.0
ROPE_DIM = HEAD_DIM // 4
RMS_EPS = 1e-6
GN_EPS = 64e-5
CHUNK = 64
IN_HALO = 16
NEG_BIG = -1e30
VMEM_LIMIT = 56 * 1024 * 1024


def _cparams(*sem):
    return pltpu.CompilerParams(dimension_semantics=sem, vmem_limit_bytes=VMEM_LIMIT)


def _mm(a, b):
    return jnp.dot(a.astype(BF16), b.astype(BF16), preferred_element_type=F32)


def _mm_nt(a, b):
    return lax.dot_general(a.astype(BF16), b.astype(BF16), (((1,), (1,)), ((), ())),
                           preferred_element_type=F32)


def _mm_tn(a, b):
    return lax.dot_general(a.astype(BF16), b.astype(BF16), (((0,), (0,)), ((), ())),
                           preferred_element_type=F32)


def _split3(a):
    hi = a.astype(BF16)
    r1 = a - hi.astype(F32)
    mid = r1.astype(BF16)
    lo = (r1 - mid.astype(F32)).astype(BF16)
    return hi, mid, lo


def _mm_exact_rhs(a, b_bf16):
    hi, mid, lo = _split3(a)
    out = jnp.dot(lo, b_bf16, preferred_element_type=F32)
    out = out + jnp.dot(mid, b_bf16, preferred_element_type=F32)
    return out + jnp.dot(hi, b_bf16, preferred_element_type=F32)


def _mm_exact_lhs(a_bf16, b):
    hi, mid, lo = _split3(b)
    out = jnp.dot(a_bf16, lo, preferred_element_type=F32)
    out = out + jnp.dot(a_bf16, mid, preferred_element_type=F32)
    return out + jnp.dot(a_bf16, hi, preferred_element_type=F32)


def _mm3(a, b):
    ah = a.astype(BF16)
    al = (a - ah.astype(F32)).astype(BF16)
    bh = b.astype(BF16)
    bl = (b - bh.astype(F32)).astype(BF16)
    out = jnp.dot(al, bh, preferred_element_type=F32)
    out = out + jnp.dot(ah, bl, preferred_element_type=F32)
    return out + jnp.dot(ah, bh, preferred_element_type=F32)


def _sigmoid(z):
    return 1.0 / (1.0 + jnp.exp(-z))


def _head_ones():
    r = jnp.where(lax.broadcasted_iota(jnp.int32, (LANES, LANES), 0) < HEAD_DIM, 0, 1)
    c = jnp.where(lax.broadcasted_iota(jnp.int32, (LANES, LANES), 1) < HEAD_DIM, 0, 1)
    return jnp.where(r == c, 1.0, 0.0).astype(BF16)


def _in_proj_kernel(x_ref, xh_ref, g_ref, wq_ref, wr_ref, wg_ref, bg_ref, mu_ref, cos_ref, sa_ref, sb_ref,
                    oq_ref, or_ref, og_ref, *, seq, tq, tg, rope_tiles, n_decay, lora_col):
    tm = x_ref.shape[0]

    def norm(x):
        ms = jnp.mean(x * x, axis=-1, keepdims=True)
        return (x * lax.rsqrt(ms + RMS_EPS) * g_ref[...]).astype(BF16)

    h = norm(x_ref[...])

    first = (pl.program_id(0) * tm) % seq == 0
    o_all = jnp.dot(jnp.concatenate([norm(xh_ref[...]), h], axis=0), wr_ref[...], preferred_element_type=F32)
    row = lax.broadcasted_iota(jnp.int32, (tm, 1), 0)
    lane = lax.broadcasted_iota(jnp.int32, (tm, LANES), 1)
    starts_sequence = (row == 0) & first
    for c in range(wr_ref.shape[1] // LANES):
        cols = slice(c * LANES, (c + 1) * LANES)
        ob_all = o_all[:, cols]
        ob = ob_all[IN_HALO:]
        prev = jnp.where(starts_sequence, 0.0, pltpu.roll(ob_all, 1, 0)[IN_HALO:])
        s = ob + (prev - ob) * mu_ref[:, cols]
        if c * LANES == lora_col:
            s = jnp.where(lane < n_decay, jnp.tanh(s), s)
        elif c * LANES > lora_col:
            s = _sigmoid(s)
        or_ref[:, cols] = s.astype(or_ref.dtype)

    half = ROPE_DIM // 2
    cos, sa, sb = cos_ref[...], sa_ref[...], sb_ref[...]
    for t in range(wq_ref.shape[1] // tq):
        cols = slice(t * tq, (t + 1) * tq)
        o = jnp.dot(h, wq_ref[:, cols], preferred_element_type=F32)
        if t >= rope_tiles:
            oq_ref[:, cols] = o
            continue
        for c in range(tq // LANES):
            ob = o[:, c * LANES:(c + 1) * LANES]
            oq_ref[:, t * tq + c * LANES:t * tq + (c + 1) * LANES] = (
                ob * cos + pltpu.roll(ob, half, 1) * sa + pltpu.roll(ob, LANES - half, 1) * sb)

    for t in range(wg_ref.shape[1] // tg):
        cols = slice(t * tg, (t + 1) * tg)
        o = jnp.dot(h, wg_ref[:, cols], preferred_element_type=F32)
        og_ref[:, cols] = _sigmoid(o + bg_ref[:, cols]).astype(og_ref.dtype)


def _in_proj(x, g, w_qkv, w_rkvz, w_gate, b_gate, mu, rope, seq, tm, tq, tg, rope_tiles, n_decay, lora_col):
    T, D = x.shape
    cos, sa, sb = rope
    nseq = seq // tm
    hb = tm // IN_HALO
    resident = lambda a: pl.BlockSpec(a.shape, lambda i: (0, 0), pipeline_mode=pl.Buffered(1))
    row = lambda w: pl.BlockSpec((tm, w), lambda i: (i, 0))
    t_spec = pl.BlockSpec((tm, LANES), lambda i: (i % nseq, 0))
    in_specs = [row(D), pl.BlockSpec((IN_HALO, D), lambda i: (jnp.maximum(i * hb - 1, 0), 0)),
                resident(g), resident(w_qkv), resident(w_rkvz), resident(w_gate), resident(b_gate), resident(mu),
                t_spec, t_spec, t_spec]
    outs = [(w_qkv.shape[1], F32), (w_rkvz.shape[1], BF16), (w_gate.shape[1], BF16)]
    return pl.pallas_call(
        functools.partial(_in_proj_kernel, seq=seq, tq=tq, tg=tg, rope_tiles=rope_tiles, n_decay=n_decay,
                          lora_col=lora_col),
        grid=(T // tm,), in_specs=in_specs, out_specs=[row(w) for w, _ in outs],
        out_shape=[jax.ShapeDtypeStruct((T, w), dt) for w, dt in outs],
        compiler_params=_cparams("parallel"), name="in_proj")(
            x, x, g, w_qkv, w_rkvz, w_gate, b_gate, mu, cos, sa, sb)


def _attn_kernel(q0, q1, q2, k0, k1, k2, v0, v1, v2, kp0, kp1, kp2, vp0, vp1, vp2, o_ref, obuf, mbuf, lbuf):
    tile = pl.program_id(2)
    q_refs, k_refs, v_refs = (q0, q1, q2), (k0, k1, k2), (v0, v1, v2)
    kp_refs, vp_refs = (kp0, kp1, kp2), (vp0, vp1, vp2)
    L = ATTN_SPAN
    TL = ATTN_TILE
    U = ATTN_UNROLL

    lane = lax.broadcasted_iota(jnp.int32, (L, LANES), 1)
    head_a = lane < HEAD_DIM
    qi = lax.broadcasted_iota(jnp.int32, (L, 2 * L), 0)
    kj = lax.broadcasted_iota(jnp.int32, (L, 2 * L), 1)
    rel = L + qi - kj
    band = (rel >= 0) & (rel <= L)
    band_first = band & (kj >= jnp.where(tile > 0, 0, L))

    def run_blocks(g, d, blocks):
        qs, kcs, vcs, valids = [], [], [], []
        for start, from_prev in blocks:
            rows = pl.ds(start, L, stride=d)
            qs.append(q_refs[g][rows, :] * (HEAD_DIM ** -0.5))
            if from_prev:
                k1, v1 = kp_refs[g][rows, :], vp_refs[g][rows, :]
            else:
                back = pl.ds(start - L * d, L, stride=d)
                k1, v1 = k_refs[g][back, :], v_refs[g][back, :]
            kcs.append(jnp.concatenate([k1, k_refs[g][rows, :]], axis=0).astype(BF16))
            vcs.append(jnp.concatenate([v1, v_refs[g][rows, :]], axis=0).astype(BF16))
            valids.append(band_first if from_prev else band)
        nb = range(len(blocks))
        qh = [[jnp.where(head_a, qs[b], 0.0).astype(BF16), jnp.where(head_a, 0.0, qs[b]).astype(BF16)] for b in nb]
        s = [[lax.dot_general(qh[b][h], kcs[b], (((1,), (1,)), ((), ())), preferred_element_type=F32)
              for h in range(2)] for b in nb]
        s = [[jnp.where(valids[b], s[b][h], NEG_BIG) for h in range(2)] for b in nb]
        m = [[jnp.max(s[b][h], axis=1, keepdims=True) for h in range(2)] for b in nb]
        e = [[jnp.exp(s[b][h] - m[b][h]) for h in range(2)] for b in nb]
        l = [[jnp.sum(e[b][h], axis=1, keepdims=True) for h in range(2)] for b in nb]
        o = [[jnp.dot(e[b][h].astype(BF16), vcs[b], preferred_element_type=F32) for h in range(2)] for b in nb]
        for b, (start, _) in enumerate(blocks):
            rows = pl.ds(start, L, stride=d)
            obuf[g, rows, :] = jnp.where(head_a, o[b][0], o[b][1])
            mbuf[g, rows, :] = jnp.where(head_a, jnp.broadcast_to(m[b][0], (L, LANES)),
                                         jnp.broadcast_to(m[b][1], (L, LANES)))
            lbuf[g, rows, :] = jnp.where(head_a, jnp.broadcast_to(l[b][0], (L, LANES)),
                                         jnp.broadcast_to(l[b][1], (L, LANES)))

    nblk = TL // L
    for g, (_, d) in enumerate(ATTN_GROUPS):
        def cur_body(it, carry, g=g, d=d):
            idxs = [it * U + u for u in range(U)]
            run_blocks(g, d, [((idx % d) + (L * d) * (idx // d), False) for idx in idxs])
            return carry

        if d >= U:
            def prev_body(it, carry, g=g, d=d):
                run_blocks(g, d, [(it * U + u, True) for u in range(U)])
                return carry

            lax.fori_loop(0, d // U, prev_body, 0)
            if d < nblk:
                lax.fori_loop(d // U, nblk // U, cur_body, 0)
        else:
            run_blocks(g, d, [((idx % d) + (L * d) * (idx // d), idx // d == 0) for idx in range(U)])
            lax.fori_loop(1, nblk // U, cur_body, 0)

    MR = 256

    def merge(i, carry):
        rows = pl.ds(pl.multiple_of(i * MR, MR), MR)
        m0, m1, m2 = mbuf[0, rows, :], mbuf[1, rows, :], mbuf[2, rows, :]
        mx = jnp.maximum(jnp.maximum(m0, m1), m2)
        w0, w1, w2 = jnp.exp(m0 - mx), jnp.exp(m1 - mx), jnp.exp(m2 - mx)
        num = w0 * obuf[0, rows, :] + w1 * obuf[1, rows, :] + w2 * obuf[2, rows, :]
        den = w0 * lbuf[0, rows, :] + w1 * lbuf[1, rows, :] + w2 * lbuf[2, rows, :]
        o_ref[rows, :] = (num / den).astype(o_ref.dtype)
        return carry

    lax.fori_loop(0, TL // MR, merge, 0)


def _attention(qkv, batch, seq):
    T = qkv.shape[0]
    TL = ATTN_TILE
    tiles = seq // TL
    npair = 2
    qcols = 3 * 4 * HEAD_DIM // LANES

    def spec(section, g):
        return pl.BlockSpec((TL, LANES),
                            lambda p, b, t, section=section, g=g: (b * tiles + t, section * qcols + 2 * g + p))

    def prev_spec(section, g):
        rows = ATTN_SPAN * ATTN_GROUPS[g][1]
        per_tile = TL // rows
        return pl.BlockSpec(
            (rows, LANES),
            lambda p, b, t, section=section, g=g: (jnp.maximum((b * tiles + t) * per_tile - 1, 0),
                                                   section * qcols + 2 * g + p))

    in_specs = ([spec(s, g) for s in range(3) for g in range(3)]
                + [prev_spec(s, g) for s in (1, 2) for g in range(3)])
    out_spec = pl.BlockSpec((TL, LANES), lambda p, b, t: (b * tiles + t, p))
    return pl.pallas_call(
        _attn_kernel, grid=(npair, batch, tiles), in_specs=in_specs, out_specs=out_spec,
        out_shape=jax.ShapeDtypeStruct((T, npair * LANES), BF16),
        scratch_shapes=[pltpu.VMEM((3, TL, LANES), F32), pltpu.VMEM((3, TL, LANES), F32),
                        pltpu.VMEM((3, TL, LANES), F32)],
        compiler_params=_cparams("parallel", "parallel", "parallel"), name="dilated_attn")(*([qkv] * 15))


def _stack_heads(x, head_a):
    return jnp.concatenate([jnp.where(head_a, x, 0.0), jnp.where(head_a, 0.0, x)], axis=0)


def _rwkv_chunk_kernel(r_ref, k_ref, v_ref, z_ref, par_ref, wd_ref, wa_ref, wg_ref,
                       rp_ref, yp_ref, bonus_ref, gate_ref, p_ref, q_ref):
    tm = r_ref.shape[0]
    C = CHUNK
    r = r_ref[...].astype(F32)
    k = k_ref[...].astype(F32)
    v = v_ref[...].astype(F32)

    w0, a0, k_k, k_a, r_k = (par_ref[i:i + 1, :] for i in range(5))
    zwa = z_ref[:, 0:LANES]
    dpre = w0 + jnp.dot(zwa, wd_ref[...], preferred_element_type=F32)
    w_log = -(jnp.maximum(-dpre, 0.0) + jnp.log(1.0 + jnp.exp(-jnp.abs(dpre)))) - 0.5
    lw = -jnp.exp(w_log)
    a = _sigmoid(a0 + jnp.dot(zwa, wa_ref[...], preferred_element_type=F32))
    gate_ref[...] = jnp.dot(z_ref[:, LANES:3 * LANES], wg_ref[...],
                            preferred_element_type=F32).astype(gate_ref.dtype)

    ones_bd = _head_ones()
    kk = k * k_k
    kk = kk / jnp.maximum(jnp.sqrt(_mm(kk * kk, ones_bd)), 1e-12)
    k2 = k * (1.0 + (a - 1.0) * k_a)
    bonus_ref[...] = (_mm(r * k2 * r_k, ones_bd) * v).astype(bonus_ref.dtype)
    beta = kk * a

    ti = lax.broadcasted_iota(jnp.int32, (C, C), 0)
    si = lax.broadcasted_iota(jnp.int32, (C, C), 1)
    tril = jnp.where(si <= ti, 1.0, 0.0).astype(BF16)
    t2 = lax.broadcasted_iota(jnp.int32, (2 * C, 2 * C), 0)
    s2 = lax.broadcasted_iota(jnp.int32, (2 * C, 2 * C), 1)
    same = jnp.where(t2 < C, 0, 1) == jnp.where(s2 < C, 0, 1)
    strict = same & (s2 < t2)
    incl = same & (s2 <= t2)
    eye = jnp.where(t2 == s2, 1.0, 0.0)
    lane = lax.broadcasted_iota(jnp.int32, (C, LANES), 1)
    head_a = lane < HEAD_DIM
    krow = lax.broadcasted_iota(jnp.int32, (C, LANES), 0)
    diag = (lane == krow) | (lane == krow + HEAD_DIM)

    chunks = range(tm // C)
    xa, xr, yy, lhs_t, vb, gam, zz = [], [], [], [], [], [], []
    for c in chunks:
        sl = slice(c * C, (c + 1) * C)
        lw_c = lw[sl]
        cum_c = _mm_exact_lhs(tril, lw_c)
        tot = cum_c[C - 1:C, :]
        g_inc = jnp.exp(cum_c)
        g_prev = jnp.exp(cum_c - lw_c)
        g_inv = jnp.exp(-cum_c)
        g_end = jnp.exp(tot - cum_c)
        gam.append(jnp.exp(tot))
        xa.append(_stack_heads(-kk[sl] * g_prev, head_a))
        xr.append(_stack_heads(r[sl] * g_inc, head_a))
        yb, yk = beta[sl] * g_inv, k2[sl] * g_inv
        yy.append(jnp.concatenate([yb, yb, yk, yk], axis=0))
        bh, kh = beta[sl] * g_end, k2[sl] * g_end
        lhs_t.append(jnp.concatenate([bh, bh, kh, kh], axis=0).T)
        vb.append(_stack_heads(v[sl], head_a))
        zz.append(_mm_nt(jnp.concatenate([xa[c], xr[c]], axis=0), yy[c]))

    l_ab = [jnp.where(strict, zz[c][:2 * C, :2 * C], 0.0) for c in chunks]
    l_ak = [jnp.where(strict, zz[c][:2 * C, 2 * C:], 0.0) for c in chunks]
    m_rbk = [jnp.concatenate([jnp.where(incl, zz[c][2 * C:, :2 * C], 0.0),
                              jnp.where(incl, zz[c][2 * C:, 2 * C:], 0.0)], axis=1) for c in chunks]
    lakv = [_mm(l_ak[c], vb[c]) for c in chunks]

    lp = [_mm(l_ab[c], l_ab[c]) for c in chunks]
    tinv = [eye + l_ab[c] for c in chunks]
    power = 2
    while 2 * power < C:
        both = [_mm(lp[c], jnp.concatenate([lp[c], tinv[c]], axis=1)) for c in chunks]
        tinv = [tinv[c] + both[c][:, 2 * C:] for c in chunks]
        lp = [both[c][:, :2 * C] for c in chunks]
        power *= 2
    tinv = [tinv[c] + _mm(lp[c], tinv[c]) for c in chunks]

    wu = [_mm(tinv[c], jnp.concatenate([xa[c], lakv[c]], axis=1)) for c in chunks]
    zero = jnp.zeros((2 * C, LANES), F32)
    for c in chunks:
        sl = slice(c * C, (c + 1) * C)
        rhs = jnp.concatenate([wu[c], jnp.concatenate([zero, vb[c]], axis=1)], axis=0)
        out = _mm(jnp.concatenate([m_rbk[c], lhs_t[c]], axis=0), rhs)
        rp = xr[c] + out[:2 * C, :LANES]
        yp = out[:2 * C, LANES:]
        rp_ref[sl, :] = (rp[:C] + rp[C:]).astype(rp_ref.dtype)
        yp_ref[sl, :] = yp[:C] + yp[C:]
        pm, qm = out[2 * C:, :LANES], out[2 * C:, LANES:]
        p_ref[c, 0] = jnp.where(head_a, pm[:C], pm[C:]) + jnp.where(diag, gam[c], 0.0)
        q_ref[c, 0] = jnp.where(head_a, qm[:C], qm[C:])


def _rwkv_chunks(prk, params, wd, wa, wg, tm):
    T = prk.shape[0]
    npair = 8
    zblk = 3 * LANES
    zcol = (3 * npair * LANES) // zblk

    def col(base):
        return pl.BlockSpec((tm, LANES), lambda i, j, base=base: (i, base + j))

    in_specs = [col(0), col(npair), col(2 * npair), pl.BlockSpec((tm, zblk), lambda i, j: (i, zcol)),
                pl.BlockSpec((8, LANES), lambda i, j: (0, j)),
                pl.BlockSpec((LANES, LANES), lambda i, j: (0, j)),
                pl.BlockSpec((LANES, LANES), lambda i, j: (0, j)),
                pl.BlockSpec((2 * LANES, LANES), lambda i, j: (0, j))]
    tok = pl.BlockSpec((tm, LANES), lambda i, j: (i, j))
    mat = pl.BlockSpec((tm // CHUNK, 1, CHUNK, LANES), lambda i, j: (i, j, 0, 0))
    tok_shape = jax.ShapeDtypeStruct((T, npair * LANES), F32)
    tok_bf16 = jax.ShapeDtypeStruct((T, npair * LANES), BF16)
    mat_shape = jax.ShapeDtypeStruct((T // CHUNK, npair, CHUNK, LANES), F32)
    return pl.pallas_call(
        _rwkv_chunk_kernel, grid=(T // tm, npair),
        in_specs=in_specs, out_specs=[tok, tok, tok, tok, mat, mat],
        out_shape=[tok_bf16, tok_shape, tok_bf16, tok_bf16, mat_shape, mat_shape],
        compiler_params=_cparams("parallel", "parallel"), name="rwkv_chunks")(
            prk, prk, prk, prk, params, wd, wa, wg)


def _rwkv_scan_kernel(rp_ref, yp_ref, p_ref, q_ref, y_ref, s_scr):
    nseq, npair = s_scr.shape[0], s_scr.shape[1]
    nchunk = p_ref.shape[1]
    C = CHUNK

    @pl.when(pl.program_id(0) == 0)
    def _():
        s_scr[...] = jnp.zeros(s_scr.shape, F32)

    head_a = lax.broadcasted_iota(jnp.int32, (C, LANES), 1) < HEAD_DIM

    def chunk(c, carry):
        rows = pl.ds(pl.multiple_of(c * C, C), C)
        units = [(b, j) for b in range(nseq) for j in range(npair)]
        s = [s_scr[b, j] for b, j in units]
        for (b, j), s_u in zip(units, s):
            cols = slice(j * LANES, (j + 1) * LANES)
            y_ref[b, rows, cols] = _mm(rp_ref[b, rows, cols], s_u) + yp_ref[b, rows, cols]
        for (b, j), s_u in zip(units, s):
            s_scr[b, j] = _mm3(_stack_heads(p_ref[b, c, j], head_a), s_u) + _stack_heads(q_ref[b, c, j], head_a)
        return carry

    lax.fori_loop(0, nchunk, chunk, 0)


def _rwkv_scan(rp, yp, p, q, batch, seq, tm):
    T, W = rp.shape
    npair = W // LANES
    rp, yp = rp.reshape(batch, seq, W), yp.reshape(batch, seq, W)
    p = p.reshape(batch, seq // CHUNK, npair, CHUNK, LANES)
    q = q.reshape(batch, seq // CHUNK, npair, CHUNK, LANES)
    tok = pl.BlockSpec((batch, tm, W), lambda i: (0, i, 0))
    mat = pl.BlockSpec((batch, tm // CHUNK, npair, CHUNK, LANES), lambda i: (0, i, 0, 0, 0))
    y = pl.pallas_call(
        _rwkv_scan_kernel, grid=(seq // tm,),
        in_specs=[tok, tok, mat, mat], out_specs=tok,
        out_shape=jax.ShapeDtypeStruct((batch, seq, W), F32),
        scratch_shapes=[pltpu.VMEM((batch, npair, LANES, LANES), F32)],
        compiler_params=_cparams("arbitrary"), name="rwkv_scan")(rp, yp, p, q)
    return y.reshape(T, W)


def _merge_kernel(x_ref, ya_ref, y_ref, bonus_ref, rg_ref, lnw_ref, lnb_ref, ga_ref, gr_ref,
                  wba_ref, wbr_ref, wo_ref, o_ref, yr_scr):
    ones_bd = _head_ones()
    inv_n = 1.0 / HEAD_DIM
    for j in range(y_ref.shape[1] // LANES):
        cols = slice(j * LANES, (j + 1) * LANES)
        y = y_ref[:, cols]
        yc = y - _mm(y, ones_bd) * inv_n
        var = _mm(yc * yc, ones_bd) * inv_n
        yn = yc * lax.rsqrt(var + GN_EPS) * lnw_ref[:, cols] + lnb_ref[:, cols]
        yr_scr[:, cols] = ((yn + bonus_ref[:, cols]) * rg_ref[:, cols]).astype(BF16)

    ba = jnp.dot(ya_ref[...], wba_ref[...], preferred_element_type=F32)
    br = jnp.dot(yr_scr[...], wbr_ref[...], preferred_element_type=F32)
    merged = ga_ref[...] * ba + gr_ref[...] * br
    o_ref[...] = x_ref[...] + jnp.dot(merged.astype(BF16), wo_ref[...], preferred_element_type=F32)


def _merge(x, ya, y, bonus, rgate, lnw, lnb, gates, wba, wbr, wo, tm):
    T, D = x.shape
    row = lambda w: pl.BlockSpec((tm, w), lambda i: (i, 0))
    full = lambda a: pl.BlockSpec(a.shape, lambda i: (0, 0))
    return pl.pallas_call(
        _merge_kernel, grid=(T // tm,),
        in_specs=[row(D), row(ya.shape[1]), row(D), row(D), row(D), full(lnw), full(lnb),
                  pl.BlockSpec((tm, D), lambda i: (i, 0)), pl.BlockSpec((tm, D), lambda i: (i, 1)),
                  full(wba), full(wbr), full(wo)],
        out_specs=row(D), out_shape=jax.ShapeDtypeStruct((T, D), F32),
        scratch_shapes=[pltpu.VMEM((tm, D), BF16)],
        compiler_params=_cparams("parallel"), name="merge_out")(
            x, ya, y, bonus, rgate, lnw, lnb, gates, gates, wba, wbr, wo)


FFN_HALO = 16


def _ffn_kernel(x_ref, xh_ref, g_ref, wu_ref, cw_ref, cb_ref, wd_ref, gf_ref, o_ref, act_scr, *, seq, pieces):
    tm = x_ref.shape[0]
    H = FFN_HALO
    dff = wd_ref.shape[0]

    def norm(x):
        ms = jnp.mean(x * x, axis=-1, keepdims=True)
        return x * lax.rsqrt(ms + RMS_EPS) * g_ref[...]

    first = (pl.program_id(0) * tm) % seq == 0
    h = jnp.concatenate([jnp.where(first, 0.0, norm(xh_ref[...])).astype(BF16),
                         norm(x_ref[...]).astype(BF16)], axis=0)

    def conv(cols):
        u = jnp.dot(h, wu_ref[:, cols], preferred_element_type=F32)
        c = (u * cw_ref[2:3, cols] + pltpu.roll(u, 1, 0) * cw_ref[1:2, cols]
             + pltpu.roll(u, 2, 0) * cw_ref[0:1, cols])
        return c[H:] + cb_ref[:, cols]

    for lo, hi in pieces:
        cg = conv(slice(lo, hi))
        cv = conv(slice(dff + lo, dff + hi))
        act_scr[:, lo:hi] = (cg * _sigmoid(cg) * cv).astype(BF16)

    x2 = x_ref[...] + jnp.dot(act_scr[...], wd_ref[...], preferred_element_type=F32)
    ms = jnp.mean(x2 * x2, axis=-1, keepdims=True)
    o_ref[...] = x2 * lax.rsqrt(ms + RMS_EPS) * gf_ref[...]


def _conv_ffn(x, g, w_up, conv_w, conv_b, w_down, g_final, seq, tm):
    T, D = x.shape
    dff = w_down.shape[0]
    hb = tm // FFN_HALO
    mid = (dff // 2 + MXU_TILE - 1) // MXU_TILE * MXU_TILE
    pieces = ((0, mid), (mid, dff))
    resident = lambda a: pl.BlockSpec(a.shape, lambda i: (0, 0), pipeline_mode=pl.Buffered(1))
    in_specs = [pl.BlockSpec((tm, D), lambda i: (i, 0)),
                pl.BlockSpec((FFN_HALO, D), lambda i: (jnp.maximum(i * hb - 1, 0), 0)),
                resident(g), resident(w_up), resident(conv_w), resident(conv_b), resident(w_down),
                resident(g_final)]
    return pl.pallas_call(
        functools.partial(_ffn_kernel, seq=seq, pieces=pieces), grid=(T // tm,), in_specs=in_specs,
        out_specs=pl.BlockSpec((tm, D), lambda i: (i, 0)),
        out_shape=jax.ShapeDtypeStruct((T, D), F32),
        scratch_shapes=[pltpu.VMEM((tm, dff), BF16)],
        compiler_params=_cparams("parallel"), name="conv_ffn")(
            x, x, g, w_up, conv_w, conv_b, w_down, g_final)


def _rope_tables(seq):
    half = ROPE_DIM // 2
    lane = jnp.arange(LANES, dtype=jnp.int32) % HEAD_DIM
    expo = (lane % half).astype(F32) * (2.0 / ROPE_DIM)
    freq = jnp.where(lane < ROPE_DIM, jnp.power(ROPE_THETA, -expo), 0.0)
    lo_n = 128
    hi = (jnp.arange(seq // lo_n, dtype=jnp.int32) * lo_n).astype(F32)[:, None] * freq[None, :]
    lo = jnp.arange(lo_n, dtype=jnp.int32).astype(F32)[:, None] * freq[None, :]
    ch, sh, cl, sl = jnp.cos(hi)[:, None, :], jnp.sin(hi)[:, None, :], jnp.cos(lo)[None], jnp.sin(lo)[None]
    cos = (ch * cl - sh * sl).reshape(seq, LANES)
    sin = (sh * cl + ch * sl).reshape(seq, LANES)
    sa = jnp.where((lane >= half) & (lane < ROPE_DIM), sin, 0.0)
    sb = jnp.where(lane < half, -sin, 0.0)
    return cos, sa, sb


def kernel(x, norm_mix_g, w_in, b_gate, mu_shift, w0, w_decay_up, a0, w_a_up, w_g_up, k_k, k_a, r_k,
           ln_x_w, ln_x_b, w_branch_attn, w_branch_rwkv, w_out, norm_ffn_g, w_ffn_up, conv_w, conv_b,
           w_ffn_down, norm_final_g):
    B, S, D = x.shape
    T = B * S
    depth = norm_mix_g.shape[0]
    attn_w = 3 * 4 * HEAD_DIM
    rw = D
    n_decay, n_iclr, n_gate = w_decay_up.shape[1], w_a_up.shape[1], w_g_up.shape[1]
    lora_w = n_decay + n_iclr + n_gate
    lora_pad = 3 * LANES
    c_rwkv = 3 * attn_w
    c_lora = c_rwkv + 3 * rw
    c_gate = c_lora + lora_w

    assert depth == 1 and S % ATTN_TILE == 0 and D == 16 * HEAD_DIM and n_decay + n_iclr == LANES
    xt = x.reshape(T, D)
    cos, sa, sb = _rope_tables(S)
    for l in range(depth):
        wi = w_in[l]
        w_qkv = wi[:, :c_rwkv].astype(BF16)
        w_rkvz = wi[:, c_rwkv:c_lora + lora_pad].astype(BF16)
        w_gt = wi[:, c_gate:].astype(BF16)
        gmix = norm_mix_g[l].reshape(1, D)

        mu = jnp.concatenate([mu_shift[l], jnp.zeros((lora_pad - lora_w,), F32)]).reshape(1, -1)
        qkv, prk, gates = _in_proj(xt, gmix, w_qkv, w_rkvz, w_gt, b_gate[l].reshape(1, -1), mu, (cos, sa, sb), S,
                                   512, attn_w, D, rope_tiles=2, n_decay=n_decay, lora_col=3 * rw)

        y_attn = _attention(qkv, B, S)

        params = jnp.concatenate(
            [jnp.stack([w0[l], a0[l], k_k[l], k_a[l], r_k[l].reshape(-1)]), jnp.zeros((3, rw), F32)], axis=0)
        wd = jnp.concatenate([w_decay_up[l], jnp.zeros((LANES - n_decay, rw), F32)], axis=0).astype(BF16)
        wa = jnp.concatenate([jnp.zeros((n_decay, rw), F32), w_a_up[l],
                              jnp.zeros((LANES - n_decay - n_iclr, rw), F32)], axis=0).astype(BF16)
        wg = jnp.concatenate([w_g_up[l], jnp.zeros((2 * LANES - n_gate, rw), F32)], axis=0).astype(BF16)
        rp, yp, bonus, gate, pm, qm = _rwkv_chunks(prk, params, wd, wa, wg, 1024)
        y_scan = _rwkv_scan(rp, yp, pm, qm, B, S, 256)

        xt = _merge(xt, y_attn, y_scan, bonus, gate, ln_x_w[l].reshape(1, -1), ln_x_b[l].reshape(1, -1),
                    gates, w_branch_attn[l].astype(BF16),
                    w_branch_rwkv[l].astype(BF16), w_out[l].astype(BF16), 512)

        xt = _conv_ffn(xt, norm_ffn_g[l].reshape(1, D), w_ffn_up[l].astype(BF16), conv_w[l].reshape(3, -1),
                       conv_b[l].reshape(1, -1), w_ffn_down[l].astype(BF16), norm_final_g.reshape(1, D),
                       S, 512)
    return xt.reshape(B, S, D)
```

```python
import functools

import jax
import jax.numpy as jnp
from jax import lax
from jax.experimental import pallas as pl
from jax.experimental.pallas import tpu as pltpu

F32 = jnp.float32
BF16 = jnp.bfloat16

HEAD_DIM = 64
LANES = 128
MXU_TILE = 256
ATTN_GROUPS = ((128, 1), (512, 4), (2048, 16))
ATTN_SPAN = 128
ATTN_TILE = 2048
ATTN_UNROLL = 4
ROPE_THETA = 500000.0
ROPE_DIM = HEAD_DIM // 4
RMS_EPS = 1e-6
GN_EPS = 64e-5
CHUNK = 64
DECAY_SCALE = 0.6065306597126334
IN_HALO = 16
NEG_BIG = -1e30
VMEM_LIMIT = 56 * 1024 * 1024


def _cparams(*sem):
    return pltpu.CompilerParams(dimension_semantics=sem, vmem_limit_bytes=VMEM_LIMIT)


def _mm(a, b):
    return jnp.dot(a.astype(BF16), b.astype(BF16), preferred_element_type=F32)


def _mm_nt(a, b):
    return lax.dot_general(a.astype(BF16), b.astype(BF16), (((1,), (1,)), ((), ())),
                           preferred_element_type=F32)


def _mm_tn(a, b):
    return lax.dot_general(a.astype(BF16), b.astype(BF16), (((0,), (0,)), ((), ())),
                           preferred_element_type=F32)


def _split3(a):
    hi = a.astype(BF16)
    r1 = a - hi.astype(F32)
    mid = r1.astype(BF16)
    lo = (r1 - mid.astype(F32)).astype(BF16)
    return hi, mid, lo


def _mm_exact_rhs(a, b_bf16):
    hi, mid, lo = _split3(a)
    out = jnp.dot(lo, b_bf16, preferred_element_type=F32)
    out = out + jnp.dot(mid, b_bf16, preferred_element_type=F32)
    return out + jnp.dot(hi, b_bf16, preferred_element_type=F32)


def _mm_exact_lhs(a_bf16, b):
    hi, mid, lo = _split3(b)
    out = jnp.dot(a_bf16, lo, preferred_element_type=F32)
    out = out + jnp.dot(a_bf16, mid, preferred_element_type=F32)
    return out + jnp.dot(a_bf16, hi, preferred_element_type=F32)


def _mm3(a, b):
    ah = a.astype(BF16)
    al = (a - ah.astype(F32)).astype(BF16)
    bh = b.astype(BF16)
    bl = (b - bh.astype(F32)).astype(BF16)
    out = jnp.dot(al, bh, preferred_element_type=F32)
    out = out + jnp.dot(ah, bl, preferred_element_type=F32)
    return out + jnp.dot(ah, bh, preferred_element_type=F32)


def _sigmoid(z):
    return 1.0 / (1.0 + jnp.exp(-z))


def _head_ones():
    r = jnp.where(lax.broadcasted_iota(jnp.int32, (LANES, LANES), 0) < HEAD_DIM, 0, 1)
    c = jnp.where(lax.broadcasted_iota(jnp.int32, (LANES, LANES), 1) < HEAD_DIM, 0, 1)
    return jnp.where(r == c, 1.0, 0.0).astype(BF16)


def _in_proj_kernel(x_ref, xh_ref, g_ref, wq_ref, wr_ref, wg_ref, bg_ref, mu_ref, cos_ref, sa_ref, sb_ref,
                    oq_ref, or_ref, og_ref, *, seq, tq, tg, rope_tiles, n_decay, lora_col):
    tm = x_ref.shape[0]

    def norm(x):
        ms = jnp.mean(x * x, axis=-1, keepdims=True)
        return (x * lax.rsqrt(ms + RMS_EPS) * g_ref[...]).astype(BF16)

    h = norm(x_ref[...])

    first = (pl.program_id(0) * tm) % seq == 0
    o_all = jnp.dot(jnp.concatenate([norm(xh_ref[...]), h], axis=0), wr_ref[...], preferred_element_type=F32)
    row = lax.broadcasted_iota(jnp.int32, (tm, 1), 0)
    lane = lax.broadcasted_iota(jnp.int32, (tm, LANES), 1)
    starts_sequence = (row == 0) & first
    for c in range(wr_ref.shape[1] // LANES):
        cols = slice(c * LANES, (c + 1) * LANES)
        ob_all = o_all[:, cols]
        ob = ob_all[IN_HALO:]
        prev = jnp.where(starts_sequence, 0.0, pltpu.roll(ob_all, 1, 0)[IN_HALO:])
        s = ob + (prev - ob) * mu_ref[:, cols]
        if c * LANES == lora_col:
            s = jnp.where(lane < n_decay, jnp.tanh(s), s)
        elif c * LANES > lora_col:
            s = _sigmoid(s)
        or_ref[:, cols] = s.astype(or_ref.dtype)

    half = ROPE_DIM // 2
    cos, sa, sb = cos_ref[...], sa_ref[...], sb_ref[...]
    for t in range(wq_ref.shape[1] // tq):
        cols = slice(t * tq, (t + 1) * tq)
        o = jnp.dot(h, wq_ref[:, cols], preferred_element_type=F32)
        if t >= rope_tiles:
            oq_ref[:, cols] = o
            continue
        for c in range(tq // LANES):
            ob = o[:, c * LANES:(c + 1) * LANES]
            oq_ref[:, t * tq + c * LANES:t * tq + (c + 1) * LANES] = (
                ob * cos + pltpu.roll(ob, half, 1) * sa + pltpu.roll(ob, LANES - half, 1) * sb)

    for t in range(wg_ref.shape[1] // tg):
        cols = slice(t * tg, (t + 1) * tg)
        o = jnp.dot(h, wg_ref[:, cols], preferred_element_type=F32)
        og_ref[:, cols] = _sigmoid(o + bg_ref[:, cols]).astype(og_ref.dtype)


def _in_proj(x, g, w_qkv, w_rkvz, w_gate, b_gate, mu, rope, seq, tm, tq, tg, rope_tiles, n_decay, lora_col):
    T, D = x.shape
    cos, sa, sb = rope
    nseq = seq // tm
    hb = tm // IN_HALO
    resident = lambda a: pl.BlockSpec(a.shape, lambda i: (0, 0), pipeline_mode=pl.Buffered(1))
    row = lambda w: pl.BlockSpec((tm, w), lambda i: (i, 0))
    t_spec = pl.BlockSpec((tm, LANES), lambda i: (i % nseq, 0))
    in_specs = [row(D), pl.BlockSpec((IN_HALO, D), lambda i: (jnp.maximum(i * hb - 1, 0), 0)),
                resident(g), resident(w_qkv), resident(w_rkvz), resident(w_gate), resident(b_gate), resident(mu),
                t_spec, t_spec, t_spec]
    outs = [(w_qkv.shape[1], F32), (w_rkvz.shape[1], BF16), (w_gate.shape[1], BF16)]
    return pl.pallas_call(
        functools.partial(_in_proj_kernel, seq=seq, tq=tq, tg=tg, rope_tiles=rope_tiles, n_decay=n_decay,
                          lora_col=lora_col),
        grid=(T // tm,), in_specs=in_specs, out_specs=[row(w) for w, _ in outs],
        out_shape=[jax.ShapeDtypeStruct((T, w), dt) for w, dt in outs],
        compiler_params=_cparams("parallel"), name="in_proj")(
            x, x, g, w_qkv, w_rkvz, w_gate, b_gate, mu, cos, sa, sb)


def _attn_kernel(q0, q1, q2, k0, k1, k2, v0, v1, v2, kp0, kp1, kp2, vp0, vp1, vp2, o_ref, obuf, mbuf, lbuf):
    tile = pl.program_id(2)
    q_refs, k_refs, v_refs = (q0, q1, q2), (k0, k1, k2), (v0, v1, v2)
    kp_refs, vp_refs = (kp0, kp1, kp2), (vp0, vp1, vp2)
    L = ATTN_SPAN
    TL = ATTN_TILE
    U = ATTN_UNROLL

    lane = lax.broadcasted_iota(jnp.int32, (L, LANES), 1)
    head_a = lane < HEAD_DIM
    qi = lax.broadcasted_iota(jnp.int32, (L, 2 * L), 0)
    kj = lax.broadcasted_iota(jnp.int32, (L, 2 * L), 1)
    rel = L + qi - kj
    band = (rel >= 0) & (rel <= L)
    band_first = band & (kj >= jnp.where(tile > 0, 0, L))

    def run_blocks(g, d, blocks):
        qs, kcs, vcs, valids = [], [], [], []
        for start, from_prev in blocks:
            rows = pl.ds(start, L, stride=d)
            qs.append(q_refs[g][rows, :] * (HEAD_DIM ** -0.5))
            if from_prev:
                k1, v1 = kp_refs[g][rows, :], vp_refs[g][rows, :]
            else:
                back = pl.ds(start - L * d, L, stride=d)
                k1, v1 = k_refs[g][back, :], v_refs[g][back, :]
            kcs.append(jnp.concatenate([k1, k_refs[g][rows, :]], axis=0).astype(BF16))
            vcs.append(jnp.concatenate([v1, v_refs[g][rows, :]], axis=0).astype(BF16))
            valids.append(band_first if from_prev else band)
        nb = range(len(blocks))
        qh = [[jnp.where(head_a, qs[b], 0.0).astype(BF16), jnp.where(head_a, 0.0, qs[b]).astype(BF16)] for b in nb]
        s = [[lax.dot_general(qh[b][h], kcs[b], (((1,), (1,)), ((), ())), preferred_element_type=F32)
              for h in range(2)] for b in nb]
        s = [[jnp.where(valids[b], s[b][h], NEG_BIG) for h in range(2)] for b in nb]
        m = [[jnp.max(s[b][h], axis=1, keepdims=True) for h in range(2)] for b in nb]
        e = [[jnp.exp(s[b][h] - m[b][h]) for h in range(2)] for b in nb]
        l = [[jnp.sum(e[b][h], axis=1, keepdims=True) for h in range(2)] for b in nb]
        o = [[jnp.dot(e[b][h].astype(BF16), vcs[b], preferred_element_type=F32) for h in range(2)] for b in nb]
        for b, (start, _) in enumerate(blocks):
            rows = pl.ds(start, L, stride=d)
            obuf[g, rows, :] = jnp.where(head_a, o[b][0], o[b][1])
            mbuf[g, rows, :] = jnp.where(head_a, jnp.broadcast_to(m[b][0], (L, LANES)),
                                         jnp.broadcast_to(m[b][1], (L, LANES)))
            lbuf[g, rows, :] = jnp.where(head_a, jnp.broadcast_to(l[b][0], (L, LANES)),
                                         jnp.broadcast_to(l[b][1], (L, LANES)))

    nblk = TL // L
    for g, (_, d) in enumerate(ATTN_GROUPS):
        def cur_body(it, carry, g=g, d=d):
            idxs = [it * U + u for u in range(U)]
            run_blocks(g, d, [((idx % d) + (L * d) * (idx // d), False) for idx in idxs])
            return carry

        if d >= U:
            def prev_body(it, carry, g=g, d=d):
                run_blocks(g, d, [(it * U + u, True) for u in range(U)])
                return carry

            lax.fori_loop(0, d // U, prev_body, 0)
            if d < nblk:
                lax.fori_loop(d // U, nblk // U, cur_body, 0)
        else:
            run_blocks(g, d, [((idx % d) + (L * d) * (idx // d), idx // d == 0) for idx in range(U)])
            lax.fori_loop(1, nblk // U, cur_body, 0)

    MR = 256

    def merge(i, carry):
        rows = pl.ds(pl.multiple_of(i * MR, MR), MR)
        m0, m1, m2 = mbuf[0, rows, :], mbuf[1, rows, :], mbuf[2, rows, :]
        mx = jnp.maximum(jnp.maximum(m0, m1), m2)
        w0, w1, w2 = jnp.exp(m0 - mx), jnp.exp(m1 - mx), jnp.exp(m2 - mx)
        num = w0 * obuf[0, rows, :] + w1 * obuf[1, rows, :] + w2 * obuf[2, rows, :]
        den = w0 * lbuf[0, rows, :] + w1 * lbuf[1, rows, :] + w2 * lbuf[2, rows, :]
        o_ref[rows, :] = (num / den).astype(o_ref.dtype)
        return carry

    lax.fori_loop(0, TL // MR, merge, 0)


def _attention(qkv, batch, seq):
    T = qkv.shape[0]
    TL = ATTN_TILE
    tiles = seq // TL
    npair = 2
    qcols = 3 * 4 * HEAD_DIM // LANES

    def spec(section, g):
        return pl.BlockSpec((TL, LANES),
                            lambda p, b, t, section=section, g=g: (b * tiles + t, section * qcols + 2 * g + p))

    def prev_spec(section, g):
        rows = ATTN_SPAN * ATTN_GROUPS[g][1]
        per_tile = TL // rows
        return pl.BlockSpec(
            (rows, LANES),
            lambda p, b, t, section=section, g=g: (jnp.maximum((b * tiles + t) * per_tile - 1, 0),
                                                   section * qcols + 2 * g + p))

    in_specs = ([spec(s, g) for s in range(3) for g in range(3)]
                + [prev_spec(s, g) for s in (1, 2) for g in range(3)])
    out_spec = pl.BlockSpec((TL, LANES), lambda p, b, t: (b * tiles + t, p))
    return pl.pallas_call(
        _attn_kernel, grid=(npair, batch, tiles), in_specs=in_specs, out_specs=out_spec,
        out_shape=jax.ShapeDtypeStruct((T, npair * LANES), BF16),
        scratch_shapes=[pltpu.VMEM((3, TL, LANES), F32), pltpu.VMEM((3, TL, LANES), F32),
                        pltpu.VMEM((3, TL, LANES), F32)],
        compiler_params=_cparams("parallel", "parallel", "parallel"), name="dilated_attn")(*([qkv] * 15))


def _stack_heads(x, head_a):
    return jnp.concatenate([jnp.where(head_a, x, 0.0), jnp.where(head_a, 0.0, x)], axis=0)


def _rwkv_chunk_kernel(r_ref, k_ref, v_ref, z_ref, par_ref, wd_ref, wa_ref, wg_ref,
                       rp_ref, yp_ref, bonus_ref, gate_ref, p_ref, q_ref):
    tm = r_ref.shape[0]
    C = CHUNK
    r = r_ref[...].astype(F32)
    k = k_ref[...].astype(F32)
    v = v_ref[...].astype(F32)

    w0, a0, k_k, k_a, r_k = (par_ref[i:i + 1, :] for i in range(5))
    zwa = z_ref[:, 0:LANES]
    dpre = w0 + jnp.dot(zwa, wd_ref[...], preferred_element_type=F32)
    w_log = -(jnp.maximum(-dpre, 0.0) + jnp.log(1.0 + jnp.exp(-jnp.abs(dpre)))) - 0.5
    lw = -jnp.exp(w_log)
    a = _sigmoid(a0 + jnp.dot(zwa, wa_ref[...], preferred_element_type=F32))
    gate_ref[...] = jnp.dot(z_ref[:, LANES:3 * LANES], wg_ref[...],
                            preferred_element_type=F32).astype(gate_ref.dtype)

    ones_bd = _head_ones()
    kk = k * k_k
    kk = kk * lax.rsqrt(jnp.maximum(_mm(kk * kk, ones_bd), 1e-24))
    k2 = k * (1.0 + (a - 1.0) * k_a)
    bonus_ref[...] = (_mm(r * k2 * r_k, ones_bd) * v).astype(bonus_ref.dtype)
    beta = kk * a

    ti = lax.broadcasted_iota(jnp.int32, (C, C), 0)
    si = lax.broadcasted_iota(jnp.int32, (C, C), 1)
    tril = jnp.where(si <= ti, 1.0, 0.0).astype(BF16)
    t2 = lax.broadcasted_iota(jnp.int32, (2 * C, 2 * C), 0)
    s2 = lax.broadcasted_iota(jnp.int32, (2 * C, 2 * C), 1)
    same = jnp.where(t2 < C, 0, 1) == jnp.where(s2 < C, 0, 1)
    strict = same & (s2 < t2)
    incl = same & (s2 <= t2)
    eye = jnp.where(t2 == s2, 1.0, 0.0)
    lane = lax.broadcasted_iota(jnp.int32, (C, LANES), 1)
    head_a = lane < HEAD_DIM
    krow = lax.broadcasted_iota(jnp.int32, (C, LANES), 0)
    diag = (lane == krow) | (lane == krow + HEAD_DIM)

    chunks = range(tm // C)
    xa, xr, yy, lhs_t, vb, gam, zz = [], [], [], [], [], [], []
    for c in chunks:
        sl = slice(c * C, (c + 1) * C)
        lw_c = lw[sl]
        cum_c = _mm_exact_lhs(tril, lw_c)
        tot = cum_c[C - 1:C, :]
        g_inc = jnp.exp(cum_c)
        g_prev = jnp.exp(cum_c - lw_c)
        g_inv = jnp.exp(-cum_c)
        g_end = jnp.exp(tot - cum_c)
        gam.append(jnp.exp(tot))
        xa.append(_stack_heads(-kk[sl] * g_prev, head_a))
        xr.append(_stack_heads(r[sl] * g_inc, head_a))
        yb, yk = beta[sl] * g_inv, k2[sl] * g_inv
        yy.append(jnp.concatenate([yb, yb, yk, yk], axis=0))
        bh, kh = beta[sl] * g_end, k2[sl] * g_end
        lhs_t.append(jnp.concatenate([bh, bh, kh, kh], axis=0).T)
        vb.append(_stack_heads(v[sl], head_a))
        zz.append(_mm_nt(jnp.concatenate([xa[c], xr[c]], axis=0), yy[c]))

    l_ab = [jnp.where(strict, zz[c][:2 * C, :2 * C], 0.0) for c in chunks]
    l_ak = [jnp.where(strict, zz[c][:2 * C, 2 * C:], 0.0) for c in chunks]
    m_rbk = [jnp.concatenate([jnp.where(incl, zz[c][2 * C:, :2 * C], 0.0),
                              jnp.where(incl, zz[c][2 * C:, 2 * C:], 0.0)], axis=1) for c in chunks]
    lakv = [_mm(l_ak[c], vb[c]) for c in chunks]

    lp = [_mm(l_ab[c], l_ab[c]) for c in chunks]
    tinv = [eye + l_ab[c] for c in chunks]
    power = 2
    while 2 * power < C:
        both = [_mm(lp[c], jnp.concatenate([lp[c], tinv[c]], axis=1)) for c in chunks]
        tinv = [tinv[c] + both[c][:, 2 * C:] for c in chunks]
        lp = [both[c][:, :2 * C] for c in chunks]
        power *= 2
    tinv = [tinv[c] + _mm(lp[c], tinv[c]) for c in chunks]

    wu = [_mm(tinv[c], jnp.concatenate([xa[c], lakv[c]], axis=1)) for c in chunks]
    zero = jnp.zeros((2 * C, LANES), F32)
    for c in chunks:
        sl = slice(c * C, (c + 1) * C)
        rhs = jnp.concatenate([wu[c], jnp.concatenate([zero, vb[c]], axis=1)], axis=0)
        out = _mm(jnp.concatenate([m_rbk[c], lhs_t[c]], axis=0), rhs)
        rp = xr[c] + out[:2 * C, :LANES]
        yp = out[:2 * C, LANES:]
        rp_ref[sl, :] = (rp[:C] + rp[C:]).astype(rp_ref.dtype)
        yp_ref[sl, :] = yp[:C] + yp[C:]
        pm, qm = out[2 * C:, :LANES], out[2 * C:, LANES:]
        p_ref[c, 0] = jnp.where(head_a, pm[:C], pm[C:]) + jnp.where(diag, gam[c], 0.0)
        q_ref[c, 0] = jnp.where(head_a, qm[:C], qm[C:])


def _rwkv_chunks(prk, params, wd, wa, wg, tm):
    T = prk.shape[0]
    npair = 8
    zblk = 3 * LANES
    zcol = (3 * npair * LANES) // zblk

    def col(base):
        return pl.BlockSpec((tm, LANES), lambda i, j, base=base: (i, base + j))

    in_specs = [col(0), col(npair), col(2 * npair), pl.BlockSpec((tm, zblk), lambda i, j: (i, zcol)),
                pl.BlockSpec((8, LANES), lambda i, j: (0, j)),
                pl.BlockSpec((LANES, LANES), lambda i, j: (0, j)),
                pl.BlockSpec((LANES, LANES), lambda i, j: (0, j)),
                pl.BlockSpec((2 * LANES, LANES), lambda i, j: (0, j))]
    tok = pl.BlockSpec((tm, LANES), lambda i, j: (i, j))
    mat = pl.BlockSpec((tm // CHUNK, 1, CHUNK, LANES), lambda i, j: (i, j, 0, 0))
    tok_shape = jax.ShapeDtypeStruct((T, npair * LANES), F32)
    tok_bf16 = jax.ShapeDtypeStruct((T, npair * LANES), BF16)
    mat_shape = jax.ShapeDtypeStruct((T // CHUNK, npair, CHUNK, LANES), F32)
    return pl.pallas_call(
        _rwkv_chunk_kernel, grid=(T // tm, npair),
        in_specs=in_specs, out_specs=[tok, tok, tok, tok, mat, mat],
        out_shape=[tok_bf16, tok_shape, tok_bf16, tok_bf16, mat_shape, mat_shape],
        compiler_params=_cparams("parallel", "parallel"), name="rwkv_chunks")(
            prk, prk, prk, prk, params, wd, wa, wg)


def _rwkv_pair_program(lanes, r_ref, k_ref, v_ref, z_ref, par_ref, wd_ref, wa_ref, wg_ref,
                       rp_ref, yp_ref, bonus_ref, gate_ref, p_ref, q_ref, slot):
    tm = r_ref.shape[0]
    C = CHUNK
    r = r_ref[:, lanes].astype(F32)
    k = k_ref[:, lanes].astype(F32)
    v = v_ref[:, lanes].astype(F32)

    w0, a0, k_k, k_a, r_k = (par_ref[i:i + 1, lanes] for i in range(5))
    zwa = z_ref[:, 0:LANES]
    dpre = w0 + jnp.dot(zwa, wd_ref[:, lanes], preferred_element_type=F32)
    lw = -DECAY_SCALE * _sigmoid(dpre)
    a = _sigmoid(a0 + jnp.dot(zwa, wa_ref[:, lanes], preferred_element_type=F32))
    gate_ref[:, lanes] = jnp.dot(z_ref[:, LANES:3 * LANES], wg_ref[:, lanes],
                                 preferred_element_type=F32).astype(gate_ref.dtype)

    ones_bd = _head_ones()
    kk = k * k_k
    kk = kk * lax.rsqrt(jnp.maximum(_mm(kk * kk, ones_bd), 1e-24))
    k2 = k * (1.0 + (a - 1.0) * k_a)
    bonus_ref[:, lanes] = (_mm(r * k2 * r_k, ones_bd) * v).astype(bonus_ref.dtype)
    beta = kk * a
    yield

    ti = lax.broadcasted_iota(jnp.int32, (C, C), 0)
    si = lax.broadcasted_iota(jnp.int32, (C, C), 1)
    tril = jnp.where(si <= ti, 1.0, 0.0).astype(BF16)
    t2 = lax.broadcasted_iota(jnp.int32, (2 * C, 2 * C), 0)
    s2 = lax.broadcasted_iota(jnp.int32, (2 * C, 2 * C), 1)
    same = jnp.where(t2 < C, 0, 1) == jnp.where(s2 < C, 0, 1)
    strict = same & (s2 < t2)
    incl = same & (s2 <= t2)
    eye = jnp.where(t2 == s2, 1.0, 0.0)
    lane = lax.broadcasted_iota(jnp.int32, (C, LANES), 1)
    head_a = lane < HEAD_DIM
    krow = lax.broadcasted_iota(jnp.int32, (C, LANES), 0)
    diag = (lane == krow) | (lane == krow + HEAD_DIM)

    chunks = range(tm // C)
    cum_all = _mm_exact_lhs(tril, jnp.concatenate([lw[c * C:(c + 1) * C] for c in chunks], axis=1))
    xa, xr, yy, lhs_t, vb, gam, zz = [], [], [], [], [], [], []
    for c in chunks:
        sl = slice(c * C, (c + 1) * C)
        lw_c = lw[sl]
        cum_c = cum_all[:, c * LANES:(c + 1) * LANES]
        tot = cum_c[C - 1:C, :]
        g_inc = jnp.exp(cum_c)
        g_prev = jnp.exp(cum_c - lw_c)
        g_inv = jnp.exp(-cum_c)
        g_end = jnp.exp(tot - cum_c)
        gam.append(jnp.exp(tot))
        xa.append(_stack_heads(-kk[sl] * g_prev, head_a))
        xr.append(_stack_heads(r[sl] * g_inc, head_a))
        yb, yk = beta[sl] * g_inv, k2[sl] * g_inv
        yy.append(jnp.concatenate([yb, yb, yk, yk], axis=0))
        bh, kh = beta[sl] * g_end, k2[sl] * g_end
        lhs_t.append(jnp.concatenate([bh, bh, kh, kh], axis=0).T)
        vb.append(_stack_heads(v[sl], head_a))
        zz.append(_mm_nt(jnp.concatenate([xa[c], xr[c]], axis=0), yy[c]))
    yield

    l_ab = [jnp.where(strict, zz[c][:2 * C, :2 * C], 0.0) for c in chunks]
    l_ak = [jnp.where(strict, zz[c][:2 * C, 2 * C:], 0.0) for c in chunks]
    m_rbk = [jnp.concatenate([jnp.where(incl, zz[c][2 * C:, :2 * C], 0.0),
                              jnp.where(incl, zz[c][2 * C:, 2 * C:], 0.0)], axis=1) for c in chunks]
    lakv = [_mm(l_ak[c], vb[c]) for c in chunks]
    yield

    lp = [_mm(l_ab[c], l_ab[c]) for c in chunks]
    tinv = [eye + l_ab[c] for c in chunks]
    yield
    power = 2
    while 2 * power < C:
        both = [_mm(lp[c], jnp.concatenate([lp[c], tinv[c]], axis=1)) for c in chunks]
        tinv = [tinv[c] + both[c][:, 2 * C:] for c in chunks]
        lp = [both[c][:, :2 * C] for c in chunks]
        power *= 2
        yield
    tinv = [tinv[c] + _mm(lp[c], tinv[c]) for c in chunks]
    yield

    wu = [_mm(tinv[c], jnp.concatenate([xa[c], lakv[c]], axis=1)) for c in chunks]
    yield
    zero = jnp.zeros((2 * C, LANES), F32)
    for c in chunks:
        sl = slice(c * C, (c + 1) * C)
        rhs = jnp.concatenate([wu[c], jnp.concatenate([zero, vb[c]], axis=1)], axis=0)
        out = _mm(jnp.concatenate([m_rbk[c], lhs_t[c]], axis=0), rhs)
        rp = xr[c] + out[:2 * C, :LANES]
        yp = out[:2 * C, LANES:]
        rp_ref[sl, lanes] = (rp[:C] + rp[C:]).astype(rp_ref.dtype)
        yp_ref[sl, lanes] = yp[:C] + yp[C:]
        pm, qm = out[2 * C:, :LANES], out[2 * C:, LANES:]
        p_ref[c, slot] = jnp.where(head_a, pm[:C], pm[C:]) + jnp.where(diag, gam[c], 0.0)
        q_ref[c, slot] = jnp.where(head_a, qm[:C], qm[C:])
    yield


def _rwkv_chunk_kernel2(*refs, pairs, lag):
    programs = [_rwkv_pair_program(slice(p * LANES, (p + 1) * LANES), *refs, slot=p) for p in range(pairs)]
    live = [True] * pairs
    step = 0
    while any(live):
        for p in range(pairs):
            if live[p] and step >= p * lag:
                try:
                    next(programs[p])
                except StopIteration:
                    live[p] = False
        step += 1


def _rwkv_chunks2(prk, params, wd, wa, wg, tm, pairs, lag):
    T = prk.shape[0]
    npair = 8
    W = pairs * LANES
    groups = npair // pairs
    zblk = 3 * LANES
    zcol = (3 * npair * LANES) // zblk

    def col(base):
        return pl.BlockSpec((tm, W), lambda i, j, base=base: (i, base + j))

    in_specs = [col(0), col(groups), col(2 * groups), pl.BlockSpec((tm, zblk), lambda i, j: (i, zcol)),
                pl.BlockSpec((8, W), lambda i, j: (0, j)),
                pl.BlockSpec((LANES, W), lambda i, j: (0, j)),
                pl.BlockSpec((LANES, W), lambda i, j: (0, j)),
                pl.BlockSpec((2 * LANES, W), lambda i, j: (0, j))]
    tok = pl.BlockSpec((tm, W), lambda i, j: (i, j))
    mat = pl.BlockSpec((tm // CHUNK, pairs, CHUNK, LANES), lambda i, j: (i, j, 0, 0))
    tok_shape = jax.ShapeDtypeStruct((T, npair * LANES), F32)
    tok_bf16 = jax.ShapeDtypeStruct((T, npair * LANES), BF16)
    mat_shape = jax.ShapeDtypeStruct((T // CHUNK, npair, CHUNK, LANES), F32)
    return pl.pallas_call(
        functools.partial(_rwkv_chunk_kernel2, pairs=pairs, lag=lag), grid=(T // tm, groups),
        in_specs=in_specs, out_specs=[tok, tok, tok, tok, mat, mat],
        out_shape=[tok_bf16, tok_shape, tok_bf16, tok_bf16, mat_shape, mat_shape],
        compiler_params=_cparams("parallel", "parallel"), name="rwkv_chunks")(
            prk, prk, prk, prk, params, wd, wa, wg)


def _rwkv_scan_kernel(rp_ref, yp_ref, p_ref, q_ref, y_ref, s_scr):
    nseq, npair = s_scr.shape[0], s_scr.shape[1]
    nchunk = p_ref.shape[1]
    C = CHUNK

    @pl.when(pl.program_id(0) == 0)
    def _():
        s_scr[...] = jnp.zeros(s_scr.shape, F32)

    head_a = lax.broadcasted_iota(jnp.int32, (C, LANES), 1) < HEAD_DIM

    def chunk(c, carry):
        rows = pl.ds(pl.multiple_of(c * C, C), C)
        units = [(b, j) for b in range(nseq) for j in range(npair)]
        outs, cross = [], []
        for b, j in units:
            s_u = s_scr[b, j]
            s_hi = s_u.astype(BF16)
            s_lo = (s_u - s_hi.astype(F32)).astype(BF16)
            p = _stack_heads(p_ref[b, c, j], head_a)
            p_hi = p.astype(BF16)
            p_lo = (p - p_hi.astype(F32)).astype(BF16)
            lhs = jnp.concatenate([rp_ref[b, rows, j * LANES:(j + 1) * LANES], p_lo, p_hi], axis=0)
            outs.append(jnp.dot(lhs, s_hi, preferred_element_type=F32))
            cross.append(jnp.dot(p_hi, s_lo, preferred_element_type=F32))
        for (b, j), o, x in zip(units, outs, cross):
            cols = slice(j * LANES, (j + 1) * LANES)
            y_ref[b, rows, cols] = o[:C] + yp_ref[b, rows, cols]
            s_scr[b, j] = o[C:3 * C] + x + o[3 * C:] + _stack_heads(q_ref[b, c, j], head_a)
        return carry

    lax.fori_loop(0, nchunk, chunk, 0)


def _rwkv_scan(rp, yp, p, q, batch, seq, tm):
    T, W = rp.shape
    npair = W // LANES
    rp, yp = rp.reshape(batch, seq, W), yp.reshape(batch, seq, W)
    p = p.reshape(batch, seq // CHUNK, npair, CHUNK, LANES)
    q = q.reshape(batch, seq // CHUNK, npair, CHUNK, LANES)
    tok = pl.BlockSpec((batch, tm, W), lambda i: (0, i, 0))
    mat = pl.BlockSpec((batch, tm // CHUNK, npair, CHUNK, LANES), lambda i: (0, i, 0, 0, 0))
    y = pl.pallas_call(
        _rwkv_scan_kernel, grid=(seq // tm,),
        in_specs=[tok, tok, mat, mat], out_specs=tok,
        out_shape=jax.ShapeDtypeStruct((batch, seq, W), F32),
        scratch_shapes=[pltpu.VMEM((batch, npair, LANES, LANES), F32)],
        compiler_params=_cparams("arbitrary"), name="rwkv_scan")(rp, yp, p, q)
    return y.reshape(T, W)


def _merge_kernel(x_ref, ya_ref, y_ref, bonus_ref, rg_ref, lnw_ref, lnb_ref, ga_ref, gr_ref,
                  wba_ref, wbr_ref, wo_ref, o_ref, yr_scr):
    shift = HEAD_DIM.bit_length() - 1
    hr = lax.shift_right_logical(lax.broadcasted_iota(jnp.int32, (MXU_TILE, MXU_TILE), 0), shift)
    hc = lax.shift_right_logical(lax.broadcasted_iota(jnp.int32, (MXU_TILE, MXU_TILE), 1), shift)
    ones_bd = jnp.where(hr == hc, 1.0, 0.0).astype(BF16)
    inv_n = 1.0 / HEAD_DIM
    for j in range(y_ref.shape[1] // MXU_TILE):
        cols = slice(j * MXU_TILE, (j + 1) * MXU_TILE)
        y = y_ref[:, cols]
        yc = y - _mm(y, ones_bd) * inv_n
        var = _mm(yc * yc, ones_bd) * inv_n
        yn = yc * lax.rsqrt(var + GN_EPS) * lnw_ref[:, cols] + lnb_ref[:, cols]
        yr_scr[:, cols] = ((yn + bonus_ref[:, cols]) * rg_ref[:, cols]).astype(BF16)

    ba = jnp.dot(ya_ref[...], wba_ref[...], preferred_element_type=F32)
    br = jnp.dot(yr_scr[...], wbr_ref[...], preferred_element_type=F32)
    merged = ga_ref[...] * ba + gr_ref[...] * br
    o_ref[...] = x_ref[...] + jnp.dot(merged.astype(BF16), wo_ref[...], preferred_element_type=F32)


def _merge(x, ya, y, bonus, rgate, lnw, lnb, gates, wba, wbr, wo, tm):
    T, D = x.shape
    row = lambda w: pl.BlockSpec((tm, w), lambda i: (i, 0))
    full = lambda a: pl.BlockSpec(a.shape, lambda i: (0, 0))
    return pl.pallas_call(
        _merge_kernel, grid=(T // tm,),
        in_specs=[row(D), row(ya.shape[1]), row(D), row(D), row(D), full(lnw), full(lnb),
                  pl.BlockSpec((tm, D), lambda i: (i, 0)), pl.BlockSpec((tm, D), lambda i: (i, 1)),
                  full(wba), full(wbr), full(wo)],
        out_specs=row(D), out_shape=jax.ShapeDtypeStruct((T, D), F32),
        scratch_shapes=[pltpu.VMEM((tm, D), BF16)],
        compiler_params=_cparams("parallel"), name="merge_out")(
            x, ya, y, bonus, rgate, lnw, lnb, gates, gates, wba, wbr, wo)


FFN_HALO = 16


def _ffn_kernel(x_ref, xh_ref, g_ref, wu_ref, cw_ref, cb_ref, wd_ref, gf_ref, o_ref, act_scr, *, seq, pieces):
    tm = x_ref.shape[0]
    H = FFN_HALO
    dff = wd_ref.shape[0]

    def norm(x):
        ms = jnp.mean(x * x, axis=-1, keepdims=True)
        return x * lax.rsqrt(ms + RMS_EPS) * g_ref[...]

    first = (pl.program_id(0) * tm) % seq == 0
    h = jnp.concatenate([jnp.where(first, 0.0, norm(xh_ref[...])).astype(BF16),
                         norm(x_ref[...]).astype(BF16)], axis=0)

    def conv(cols):
        u = jnp.dot(h, wu_ref[:, cols], preferred_element_type=F32)
        c = (u * cw_ref[2:3, cols] + pltpu.roll(u, 1, 0) * cw_ref[1:2, cols]
             + pltpu.roll(u, 2, 0) * cw_ref[0:1, cols])
        return c[H:] + cb_ref[:, cols]

    for lo, hi in pieces:
        cg = conv(slice(lo, hi))
        cv = conv(slice(dff + lo, dff + hi))
        act_scr[:, lo:hi] = (cg * _sigmoid(cg) * cv).astype(BF16)

    x2 = x_ref[...] + jnp.dot(act_scr[...], wd_ref[...], preferred_element_type=F32)
    ms = jnp.mean(x2 * x2, axis=-1, keepdims=True)
    o_ref[...] = x2 * lax.rsqrt(ms + RMS_EPS) * gf_ref[...]


def _conv_ffn(x, g, w_up, conv_w, conv_b, w_down, g_final, seq, tm):
    T, D = x.shape
    dff = w_down.shape[0]
    hb = tm // FFN_HALO
    mid = (dff // 2 + MXU_TILE - 1) // MXU_TILE * MXU_TILE
    pieces = ((0, mid), (mid, dff))
    resident = lambda a: pl.BlockSpec(a.shape, lambda i: (0, 0), pipeline_mode=pl.Buffered(1))
    in_specs = [pl.BlockSpec((tm, D), lambda i: (i, 0)),
                pl.BlockSpec((FFN_HALO, D), lambda i: (jnp.maximum(i * hb - 1, 0), 0)),
                resident(g), resident(w_up), resident(conv_w), resident(conv_b), resident(w_down),
                resident(g_final)]
    return pl.pallas_call(
        functools.partial(_ffn_kernel, seq=seq, pieces=pieces), grid=(T // tm,), in_specs=in_specs,
        out_specs=pl.BlockSpec((tm, D), lambda i: (i, 0)),
        out_shape=jax.ShapeDtypeStruct((T, D), F32),
        scratch_shapes=[pltpu.VMEM((tm, dff), BF16)],
        compiler_params=_cparams("parallel"), name="conv_ffn")(
            x, x, g, w_up, conv_w, conv_b, w_down, g_final)


def _rope_tables(seq):
    half = ROPE_DIM // 2
    lane = jnp.arange(LANES, dtype=jnp.int32) % HEAD_DIM
    expo = (lane % half).astype(F32) * (2.0 / ROPE_DIM)
    freq = jnp.where(lane < ROPE_DIM, jnp.power(ROPE_THETA, -expo), 0.0)
    lo_n = 128
    hi = (jnp.arange(seq // lo_n, dtype=jnp.int32) * lo_n).astype(F32)[:, None] * freq[None, :]
    lo = jnp.arange(lo_n, dtype=jnp.int32).astype(F32)[:, None] * freq[None, :]
    ch, sh, cl, sl = jnp.cos(hi)[:, None, :], jnp.sin(hi)[:, None, :], jnp.cos(lo)[None], jnp.sin(lo)[None]
    cos = (ch * cl - sh * sl).reshape(seq, LANES)
    sin = (sh * cl + ch * sl).reshape(seq, LANES)
    sa = jnp.where((lane >= half) & (lane < ROPE_DIM), sin, 0.0)
    sb = jnp.where(lane < half, -sin, 0.0)
    return cos, sa, sb


def kernel(x, norm_mix_g, w_in, b_gate, mu_shift, w0, w_decay_up, a0, w_a_up, w_g_up, k_k, k_a, r_k,
           ln_x_w, ln_x_b, w_branch_attn, w_branch_rwkv, w_out, norm_ffn_g, w_ffn_up, conv_w, conv_b,
           w_ffn_down, norm_final_g):
    B, S, D = x.shape
    T = B * S
    depth = norm_mix_g.shape[0]
    attn_w = 3 * 4 * HEAD_DIM
    rw = D
    n_decay, n_iclr, n_gate = w_decay_up.shape[1], w_a_up.shape[1], w_g_up.shape[1]
    lora_w = n_decay + n_iclr + n_gate
    lora_pad = 3 * LANES
    c_rwkv = 3 * attn_w
    c_lora = c_rwkv + 3 * rw
    c_gate = c_lora + lora_w

    assert depth == 1 and S % ATTN_TILE == 0 and D == 16 * HEAD_DIM and n_decay + n_iclr == LANES
    xt = x.reshape(T, D)
    cos, sa, sb = _rope_tables(S)
    for l in range(depth):
        wi = w_in[l]
        w_qkv = wi[:, :c_rwkv].astype(BF16)
        w_rkvz = wi[:, c_rwkv:c_lora + lora_pad].astype(BF16)
        w_gt = wi[:, c_gate:].astype(BF16)
        gmix = norm_mix_g[l].reshape(1, D)

        mu = jnp.concatenate([mu_shift[l], jnp.zeros((lora_pad - lora_w,), F32)]).reshape(1, -1)
        qkv, prk, gates = _in_proj(xt, gmix, w_qkv, w_rkvz, w_gt, b_gate[l].reshape(1, -1), mu, (cos, sa, sb), S,
                                   512, attn_w, D, rope_tiles=2, n_decay=n_decay, lora_col=3 * rw)

        y_attn = _attention(qkv, B, S)

        params = jnp.concatenate(
            [jnp.stack([w0[l], a0[l], k_k[l], k_a[l], r_k[l].reshape(-1)]), jnp.zeros((3, rw), F32)], axis=0)
        wd = jnp.concatenate([w_decay_up[l], jnp.zeros((LANES - n_decay, rw), F32)], axis=0).astype(BF16)
        wa = jnp.concatenate([jnp.zeros((n_decay, rw), F32), w_a_up[l],
                              jnp.zeros((LANES - n_decay - n_iclr, rw), F32)], axis=0).astype(BF16)
        wg = jnp.concatenate([w_g_up[l], jnp.zeros((2 * LANES - n_gate, rw), F32)], axis=0).astype(BF16)
        rp, yp, bonus, gate, pm, qm = _rwkv_chunks2(prk, params, wd, wa, wg, 1024, pairs=1, lag=0)
        y_scan = _rwkv_scan(rp, yp, pm, qm, B, S, 256)

        xt = _merge(xt, y_attn, y_scan, bonus, gate, ln_x_w[l].reshape(1, -1), ln_x_b[l].reshape(1, -1),
                    gates, w_branch_attn[l].astype(BF16),
                    w_branch_rwkv[l].astype(BF16), w_out[l].astype(BF16), 512)

        xt = _conv_ffn(xt, norm_ffn_g[l].reshape(1, D), w_ffn_up[l].astype(BF16), conv_w[l].reshape(3, -1),
                       conv_b[l].reshape(1, -1), w_ffn_down[l].astype(BF16), norm_final_g.reshape(1, D),
                       S, 512)
    return xt.reshape(B, S, D)
```

```python
import functools

import jax
import jax.numpy as jnp
from jax import lax
from jax.experimental import pallas as pl
from jax.experimental.pallas import tpu as pltpu

F32 = jnp.float32
BF16 = jnp.bfloat16

HEAD_DIM = 64
LANES = 128
MXU_TILE = 256
ATTN_GROUPS = ((128, 1), (512, 4), (2048, 16))
ATTN_SPAN = 128
ATTN_TILE = 2048
ATTN_UNROLL = 4
ROPE_THETA = 500000.0
ROPE_DIM = HEAD_DIM // 4
RMS_EPS = 1e-6
GN_EPS = 64e-5
CHUNK = 64
DECAY_SCALE = 0.6065306597126334
HALO = 16
NEG_BIG = -1e30
VMEM_LIMIT = 56 * 1024 * 1024
ROW_BLOCK = 512
RWKV_BLOCK = 1024
SCAN_BLOCK = 512


def _cparams(*sem):
    return pltpu.CompilerParams(dimension_semantics=sem, vmem_limit_bytes=VMEM_LIMIT)


def _mm(a, b):
    return jnp.dot(a.astype(BF16), b.astype(BF16), preferred_element_type=F32)


def _mm_nt(a, b):
    return lax.dot_general(a.astype(BF16), b.astype(BF16), (((1,), (1,)), ((), ())),
                           preferred_element_type=F32)


def _split3(a):
    hi = a.astype(BF16)
    r1 = a - hi.astype(F32)
    mid = r1.astype(BF16)
    lo = (r1 - mid.astype(F32)).astype(BF16)
    return hi, mid, lo


def _mm_exact_lhs(a_bf16, b):
    hi, mid, lo = _split3(b)
    out = jnp.dot(a_bf16, lo, preferred_element_type=F32)
    out = out + jnp.dot(a_bf16, mid, preferred_element_type=F32)
    return out + jnp.dot(a_bf16, hi, preferred_element_type=F32)


def _sigmoid(z):
    return 1.0 / (1.0 + jnp.exp(-z))


def _head_ones(n):
    shift = HEAD_DIM.bit_length() - 1
    r = lax.shift_right_logical(lax.broadcasted_iota(jnp.int32, (n, n), 0), shift)
    c = lax.shift_right_logical(lax.broadcasted_iota(jnp.int32, (n, n), 1), shift)
    return jnp.where(r == c, 1.0, 0.0).astype(BF16)


def _in_proj_kernel(x_ref, xh_ref, g_ref, wq_ref, wr_ref, wg_ref, bg_ref, mu_ref, cos_ref, sa_ref, sb_ref,
                    oq_ref, or_ref, og_ref, *, seq, tq, tg, rope_tiles, n_decay, lora_col):
    tm = x_ref.shape[0]

    def norm(x):
        ms = jnp.mean(x * x, axis=-1, keepdims=True)
        return (x * lax.rsqrt(ms + RMS_EPS) * g_ref[...]).astype(BF16)

    h = norm(x_ref[...])

    first = (pl.program_id(0) * tm) % seq == 0
    o_all = jnp.dot(jnp.concatenate([norm(xh_ref[...]), h], axis=0), wr_ref[...], preferred_element_type=F32)
    row = lax.broadcasted_iota(jnp.int32, (tm, 1), 0)
    lane = lax.broadcasted_iota(jnp.int32, (tm, LANES), 1)
    starts_sequence = (row == 0) & first
    for c in range(wr_ref.shape[1] // LANES):
        cols = slice(c * LANES, (c + 1) * LANES)
        ob_all = o_all[:, cols]
        ob = ob_all[HALO:]
        prev = jnp.where(starts_sequence, 0.0, pltpu.roll(ob_all, 1, 0)[HALO:])
        s = ob + (prev - ob) * mu_ref[:, cols]
        if c * LANES == lora_col:
            s = jnp.where(lane < n_decay, jnp.tanh(s), s)
        elif c * LANES > lora_col:
            s = _sigmoid(s)
        or_ref[:, cols] = s.astype(or_ref.dtype)

    half = ROPE_DIM // 2
    cos, sa, sb = cos_ref[...], sa_ref[...], sb_ref[...]
    for t in range(wq_ref.shape[1] // tq):
        cols = slice(t * tq, (t + 1) * tq)
        o = jnp.dot(h, wq_ref[:, cols], preferred_element_type=F32)
        if t >= rope_tiles:
            oq_ref[:, cols] = o
            continue
        for c in range(tq // LANES):
            ob = o[:, c * LANES:(c + 1) * LANES]
            oq_ref[:, t * tq + c * LANES:t * tq + (c + 1) * LANES] = (
                ob * cos + pltpu.roll(ob, half, 1) * sa + pltpu.roll(ob, LANES - half, 1) * sb)

    for t in range(wg_ref.shape[1] // tg):
        cols = slice(t * tg, (t + 1) * tg)
        o = jnp.dot(h, wg_ref[:, cols], preferred_element_type=F32)
        og_ref[:, cols] = _sigmoid(o + bg_ref[:, cols]).astype(og_ref.dtype)


def _in_proj(x, g, w_qkv, w_rkvz, w_gate, b_gate, mu, rope, seq, tm, tq, tg, rope_tiles, n_decay, lora_col):
    T, D = x.shape
    cos, sa, sb = rope
    nseq = seq // tm
    hb = tm // HALO
    resident = lambda a: pl.BlockSpec(a.shape, lambda i: (0, 0), pipeline_mode=pl.Buffered(1))
    row = lambda w: pl.BlockSpec((tm, w), lambda i: (i, 0))
    t_spec = pl.BlockSpec((tm, LANES), lambda i: (i % nseq, 0))
    in_specs = [row(D), pl.BlockSpec((HALO, D), lambda i: (jnp.maximum(i * hb - 1, 0), 0)),
                resident(g), resident(w_qkv), resident(w_rkvz), resident(w_gate), resident(b_gate), resident(mu),
                t_spec, t_spec, t_spec]
    outs = [(w_qkv.shape[1], F32), (w_rkvz.shape[1], BF16), (w_gate.shape[1], BF16)]
    return pl.pallas_call(
        functools.partial(_in_proj_kernel, seq=seq, tq=tq, tg=tg, rope_tiles=rope_tiles, n_decay=n_decay,
                          lora_col=lora_col),
        grid=(T // tm,), in_specs=in_specs, out_specs=[row(w) for w, _ in outs],
        out_shape=[jax.ShapeDtypeStruct((T, w), dt) for w, dt in outs],
        compiler_params=_cparams("parallel"), name="in_proj")(
            x, x, g, w_qkv, w_rkvz, w_gate, b_gate, mu, cos, sa, sb)


def _attn_kernel(q0, q1, q2, k0, k1, k2, v0, v1, v2, kp0, kp1, kp2, vp0, vp1, vp2, o_ref, obuf, mbuf, lbuf):
    tile = pl.program_id(2)
    q_refs, k_refs, v_refs = (q0, q1, q2), (k0, k1, k2), (v0, v1, v2)
    kp_refs, vp_refs = (kp0, kp1, kp2), (vp0, vp1, vp2)
    L = ATTN_SPAN
    TL = ATTN_TILE
    U = ATTN_UNROLL

    lane = lax.broadcasted_iota(jnp.int32, (L, LANES), 1)
    head_a = lane < HEAD_DIM
    qi = lax.broadcasted_iota(jnp.int32, (L, 2 * L), 0)
    kj = lax.broadcasted_iota(jnp.int32, (L, 2 * L), 1)
    rel = L + qi - kj
    band = (rel >= 0) & (rel <= L)
    band_first = band & (kj >= jnp.where(tile > 0, 0, L))

    def run_blocks(g, d, blocks):
        qs, kcs, vcs, valids = [], [], [], []
        for start, from_prev in blocks:
            rows = pl.ds(start, L, stride=d)
            qs.append(q_refs[g][rows, :] * (HEAD_DIM ** -0.5))
            if from_prev:
                k1, v1 = kp_refs[g][rows, :], vp_refs[g][rows, :]
            else:
                back = pl.ds(start - L * d, L, stride=d)
                k1, v1 = k_refs[g][back, :], v_refs[g][back, :]
            kcs.append(jnp.concatenate([k1, k_refs[g][rows, :]], axis=0).astype(BF16))
            vcs.append(jnp.concatenate([v1, v_refs[g][rows, :]], axis=0).astype(BF16))
            valids.append(band_first if from_prev else band)
        nb = range(len(blocks))
        qh = [[jnp.where(head_a, qs[b], 0.0).astype(BF16), jnp.where(head_a, 0.0, qs[b]).astype(BF16)] for b in nb]
        s = [[lax.dot_general(qh[b][h], kcs[b], (((1,), (1,)), ((), ())), preferred_element_type=F32)
              for h in range(2)] for b in nb]
        s = [[jnp.where(valids[b], s[b][h], NEG_BIG) for h in range(2)] for b in nb]
        m = [[jnp.max(s[b][h], axis=1, keepdims=True) for h in range(2)] for b in nb]
        e = [[jnp.exp(s[b][h] - m[b][h]) for h in range(2)] for b in nb]
        l = [[jnp.sum(e[b][h], axis=1, keepdims=True) for h in range(2)] for b in nb]
        o = [[jnp.dot(e[b][h].astype(BF16), vcs[b], preferred_element_type=F32) for h in range(2)] for b in nb]
        for b, (start, _) in enumerate(blocks):
            rows = pl.ds(start, L, stride=d)
            obuf[g, rows, :] = jnp.where(head_a, o[b][0], o[b][1])
            mbuf[g, rows, :] = jnp.where(head_a, jnp.broadcast_to(m[b][0], (L, LANES)),
                                         jnp.broadcast_to(m[b][1], (L, LANES)))
            lbuf[g, rows, :] = jnp.where(head_a, jnp.broadcast_to(l[b][0], (L, LANES)),
                                         jnp.broadcast_to(l[b][1], (L, LANES)))

    nblk = TL // L
    for g, (_, d) in enumerate(ATTN_GROUPS):
        def cur_body(it, carry, g=g, d=d):
            idxs = [it * U + u for u in range(U)]
            run_blocks(g, d, [((idx % d) + (L * d) * (idx // d), False) for idx in idxs])
            return carry

        if d >= U:
            def prev_body(it, carry, g=g, d=d):
                run_blocks(g, d, [(it * U + u, True) for u in range(U)])
                return carry

            lax.fori_loop(0, d // U, prev_body, 0)
            if d < nblk:
                lax.fori_loop(d // U, nblk // U, cur_body, 0)
        else:
            run_blocks(g, d, [((idx % d) + (L * d) * (idx // d), idx // d == 0) for idx in range(U)])
            lax.fori_loop(1, nblk // U, cur_body, 0)

    MR = 256

    def merge(i, carry):
        rows = pl.ds(pl.multiple_of(i * MR, MR), MR)
        m0, m1, m2 = mbuf[0, rows, :], mbuf[1, rows, :], mbuf[2, rows, :]
        mx = jnp.maximum(jnp.maximum(m0, m1), m2)
        w0, w1, w2 = jnp.exp(m0 - mx), jnp.exp(m1 - mx), jnp.exp(m2 - mx)
        num = w0 * obuf[0, rows, :] + w1 * obuf[1, rows, :] + w2 * obuf[2, rows, :]
        den = w0 * lbuf[0, rows, :] + w1 * lbuf[1, rows, :] + w2 * lbuf[2, rows, :]
        o_ref[rows, :] = (num / den).astype(o_ref.dtype)
        return carry

    lax.fori_loop(0, TL // MR, merge, 0)


def _attention(qkv, batch, seq):
    T = qkv.shape[0]
    TL = ATTN_TILE
    tiles = seq // TL
    npair = 2
    qcols = 3 * 4 * HEAD_DIM // LANES

    def spec(section, g):
        return pl.BlockSpec((TL, LANES),
                            lambda p, b, t, section=section, g=g: (b * tiles + t, section * qcols + 2 * g + p))

    def prev_spec(section, g):
        rows = ATTN_SPAN * ATTN_GROUPS[g][1]
        per_tile = TL // rows
        return pl.BlockSpec(
            (rows, LANES),
            lambda p, b, t, section=section, g=g: (jnp.maximum((b * tiles + t) * per_tile - 1, 0),
                                                   section * qcols + 2 * g + p))

    in_specs = ([spec(s, g) for s in range(3) for g in range(3)]
                + [prev_spec(s, g) for s in (1, 2) for g in range(3)])
    out_spec = pl.BlockSpec((TL, LANES), lambda p, b, t: (b * tiles + t, p))
    return pl.pallas_call(
        _attn_kernel, grid=(npair, batch, tiles), in_specs=in_specs, out_specs=out_spec,
        out_shape=jax.ShapeDtypeStruct((T, npair * LANES), BF16),
        scratch_shapes=[pltpu.VMEM((3, TL, LANES), F32), pltpu.VMEM((3, TL, LANES), F32),
                        pltpu.VMEM((3, TL, LANES), F32)],
        compiler_params=_cparams("parallel", "parallel", "parallel"), name="dilated_attn")(*([qkv] * 15))


def _stack_heads(x, head_a):
    return jnp.concatenate([jnp.where(head_a, x, 0.0), jnp.where(head_a, 0.0, x)], axis=0)


def _rwkv_chunk_kernel(r_ref, k_ref, v_ref, z_ref, par_ref, wda_ref, wg_ref,
                       rp_ref, yp_ref, bonus_ref, gate_ref, p_ref, q_ref):
    tm = r_ref.shape[0]
    C = CHUNK
    r = r_ref[...].astype(F32)
    k = k_ref[...].astype(F32)
    v = v_ref[...].astype(F32)

    w0, a0, k_k, k_a, r_k = (par_ref[i:i + 1, :] for i in range(5))
    da = jnp.dot(z_ref[:, 0:LANES], wda_ref[0], preferred_element_type=F32)
    lw = -DECAY_SCALE * _sigmoid(w0 + da[:, :LANES])
    a = _sigmoid(a0 + da[:, LANES:])
    gate_ref[...] = jnp.dot(z_ref[:, LANES:3 * LANES], wg_ref[...],
                            preferred_element_type=F32).astype(gate_ref.dtype)

    kk = k * k_k
    k2 = k * (1.0 + (a - 1.0) * k_a)
    sums = _mm(jnp.concatenate([kk * kk, r * k2 * r_k], axis=1), _head_ones(2 * LANES))
    kk = kk * lax.rsqrt(jnp.maximum(sums[:, :LANES], 1e-24))
    bonus_ref[...] = (sums[:, LANES:] * v).astype(bonus_ref.dtype)
    beta = kk * a

    ti = lax.broadcasted_iota(jnp.int32, (C, C), 0)
    si = lax.broadcasted_iota(jnp.int32, (C, C), 1)
    tril = jnp.where(si <= ti, 1.0, 0.0).astype(BF16)
    t2 = lax.broadcasted_iota(jnp.int32, (2 * C, 2 * C), 0)
    s2 = lax.broadcasted_iota(jnp.int32, (2 * C, 2 * C), 1)
    same = jnp.where(t2 < C, 0, 1) == jnp.where(s2 < C, 0, 1)
    strict = same & (s2 < t2)
    incl = same & (s2 <= t2)
    eye = jnp.where(t2 == s2, 1.0, 0.0)
    lane = lax.broadcasted_iota(jnp.int32, (C, LANES), 1)
    head_a = lane < HEAD_DIM
    krow = lax.broadcasted_iota(jnp.int32, (C, LANES), 0)
    diag = (lane == krow) | (lane == krow + HEAD_DIM)

    chunks = range(tm // C)
    cum_all = _mm_exact_lhs(tril, jnp.concatenate([lw[c * C:(c + 1) * C] for c in chunks], axis=1))
    xa, xr, yy, lhs_t, vb, gam, zz = [], [], [], [], [], [], []
    for c in chunks:
        sl = slice(c * C, (c + 1) * C)
        lw_c = lw[sl]
        cum_c = cum_all[:, c * LANES:(c + 1) * LANES]
        tot = cum_c[C - 1:C, :]
        g_inc = jnp.exp(cum_c)
        g_prev = jnp.exp(cum_c - lw_c)
        g_inv = jnp.exp(-cum_c)
        g_end = jnp.exp(tot - cum_c)
        gam.append(jnp.exp(tot))
        xa.append(_stack_heads(-kk[sl] * g_prev, head_a))
        xr.append(_stack_heads(r[sl] * g_inc, head_a))
        yb, yk = beta[sl] * g_inv, k2[sl] * g_inv
        yy.append(jnp.concatenate([yb, yb, yk, yk], axis=0))
        bh, kh = beta[sl] * g_end, k2[sl] * g_end
        lhs_t.append(jnp.concatenate([bh, bh, kh, kh], axis=0).T)
        vb.append(_stack_heads(v[sl], head_a))
        zz.append(_mm_nt(jnp.concatenate([xa[c], xr[c]], axis=0), yy[c]))

    l_ab = [jnp.where(strict, zz[c][:2 * C, :2 * C], 0.0) for c in chunks]
    l_ak = [jnp.where(strict, zz[c][:2 * C, 2 * C:], 0.0) for c in chunks]
    m_rbk = [jnp.concatenate([jnp.where(incl, zz[c][2 * C:, :2 * C], 0.0),
                              jnp.where(incl, zz[c][2 * C:, 2 * C:], 0.0)], axis=1) for c in chunks]
    lakv = [_mm(l_ak[c], vb[c]) for c in chunks]

    lp = [_mm(l_ab[c], l_ab[c]) for c in chunks]
    tinv = [eye + l_ab[c] for c in chunks]
    power = 2
    while 2 * power < C:
        both = [_mm(lp[c], jnp.concatenate([lp[c], tinv[c]], axis=1)) for c in chunks]
        tinv = [tinv[c] + both[c][:, 2 * C:] for c in chunks]
        lp = [both[c][:, :2 * C] for c in chunks]
        power *= 2
    tinv = [tinv[c] + _mm(lp[c], tinv[c]) for c in chunks]

    wu = [_mm(tinv[c], jnp.concatenate([xa[c], lakv[c]], axis=1)) for c in chunks]
    zero = jnp.zeros((2 * C, LANES), F32)
    for c in chunks:
        sl = slice(c * C, (c + 1) * C)
        rhs = jnp.concatenate([wu[c], jnp.concatenate([zero, vb[c]], axis=1)], axis=0)
        out = _mm(jnp.concatenate([m_rbk[c], lhs_t[c]], axis=0), rhs)
        rp = xr[c] + out[:2 * C, :LANES]
        yp = out[:2 * C, LANES:]
        rp_ref[sl, :] = (rp[:C] + rp[C:]).astype(rp_ref.dtype)
        yp_ref[sl, :] = (yp[:C] + yp[C:]).astype(yp_ref.dtype)
        pm, qm = out[2 * C:, :LANES], out[2 * C:, LANES:]
        p_ref[c, 0] = jnp.where(head_a, pm[:C], pm[C:]) + jnp.where(diag, gam[c], 0.0)
        q_ref[c, 0] = jnp.where(head_a, qm[:C], qm[C:])


def _rwkv_chunks(prk, params, wda, wg, tm):
    T = prk.shape[0]
    npair = 8
    zblk = 3 * LANES
    zcol = (3 * npair * LANES) // zblk

    def col(base):
        return pl.BlockSpec((tm, LANES), lambda i, j, base=base: (i, base + j))

    in_specs = [col(0), col(npair), col(2 * npair), pl.BlockSpec((tm, zblk), lambda i, j: (i, zcol)),
                pl.BlockSpec((8, LANES), lambda i, j: (0, j)),
                pl.BlockSpec((1, LANES, 2 * LANES), lambda i, j: (j, 0, 0)),
                pl.BlockSpec((2 * LANES, LANES), lambda i, j: (0, j))]
    tok = pl.BlockSpec((tm, LANES), lambda i, j: (i, j))
    mat = pl.BlockSpec((tm // CHUNK, 1, CHUNK, LANES), lambda i, j: (i, j, 0, 0))
    tok_bf16 = jax.ShapeDtypeStruct((T, npair * LANES), BF16)
    mat_shape = jax.ShapeDtypeStruct((T // CHUNK, npair, CHUNK, LANES), F32)
    return pl.pallas_call(
        _rwkv_chunk_kernel, grid=(T // tm, npair),
        in_specs=in_specs, out_specs=[tok, tok, tok, tok, mat, mat],
        out_shape=[tok_bf16, tok_bf16, tok_bf16, tok_bf16, mat_shape, mat_shape],
        compiler_params=_cparams("parallel", "parallel"), name="rwkv_chunks")(
            prk, prk, prk, prk, params, wda, wg)


def _rwkv_scan_kernel(rp_ref, yp_ref, p_ref, q_ref, y_ref, s_scr):
    nseq, npair = s_scr.shape[0], s_scr.shape[1]
    nchunk = p_ref.shape[1]
    C = CHUNK

    @pl.when(pl.program_id(0) == 0)
    def _():
        s_scr[...] = jnp.zeros(s_scr.shape, F32)

    head_a = lax.broadcasted_iota(jnp.int32, (C, LANES), 1) < HEAD_DIM

    def chunk(c, carry):
        rows = pl.ds(pl.multiple_of(c * C, C), C)
        units = [(b, j) for b in range(nseq) for j in range(npair)]
        outs, cross = [], []
        for b, j in units:
            s_u = s_scr[b, j]
            s_hi = s_u.astype(BF16)
            s_lo = (s_u - s_hi.astype(F32)).astype(BF16)
            p = _stack_heads(p_ref[b, c, j], head_a)
            p_hi = p.astype(BF16)
            p_lo = (p - p_hi.astype(F32)).astype(BF16)
            lhs = jnp.concatenate([rp_ref[b, rows, j * LANES:(j + 1) * LANES], p_lo, p_hi], axis=0)
            outs.append(jnp.dot(lhs, s_hi, preferred_element_type=F32))
            cross.append(jnp.dot(p_hi, s_lo, preferred_element_type=F32))
        for (b, j), o, x in zip(units, outs, cross):
            cols = slice(j * LANES, (j + 1) * LANES)
            y_ref[b, rows, cols] = (o[:C] + yp_ref[b, rows, cols]).astype(y_ref.dtype)
            s_scr[b, j] = o[C:3 * C] + x + o[3 * C:] + _stack_heads(q_ref[b, c, j], head_a)
        return carry

    lax.fori_loop(0, nchunk, chunk, 0)


def _rwkv_scan(rp, yp, p, q, batch, seq, tm):
    T, W = rp.shape
    npair = W // LANES
    rp, yp = rp.reshape(batch, seq, W), yp.reshape(batch, seq, W)
    p = p.reshape(batch, seq // CHUNK, npair, CHUNK, LANES)
    q = q.reshape(batch, seq // CHUNK, npair, CHUNK, LANES)
    tok = pl.BlockSpec((batch, tm, W), lambda i: (0, i, 0))
    mat = pl.BlockSpec((batch, tm // CHUNK, npair, CHUNK, LANES), lambda i: (0, i, 0, 0, 0))
    y = pl.pallas_call(
        _rwkv_scan_kernel, grid=(seq // tm,),
        in_specs=[tok, tok, mat, mat], out_specs=tok,
        out_shape=jax.ShapeDtypeStruct((batch, seq, W), BF16),
        scratch_shapes=[pltpu.VMEM((batch, npair, LANES, LANES), F32)],
        compiler_params=_cparams("arbitrary"), name="rwkv_scan")(rp, yp, p, q)
    return y.reshape(T, W)


def _merge_kernel(x_ref, ya_ref, y_ref, bonus_ref, rg_ref, lnw_ref, lnb_ref, ga_ref, gr_ref,
                  wba_ref, wbr_ref, wo_ref, o_ref, yr_scr):
    ones_bd = _head_ones(MXU_TILE)
    inv_n = 1.0 / HEAD_DIM
    for j in range(y_ref.shape[1] // MXU_TILE):
        cols = slice(j * MXU_TILE, (j + 1) * MXU_TILE)
        y = y_ref[:, cols]
        yc = y - _mm(y, ones_bd) * inv_n
        var = _mm(yc * yc, ones_bd) * inv_n
        yn = yc * lax.rsqrt(var + GN_EPS) * lnw_ref[:, cols] + lnb_ref[:, cols]
        yr_scr[:, cols] = ((yn + bonus_ref[:, cols]) * rg_ref[:, cols]).astype(BF16)

    ba = jnp.dot(ya_ref[...], wba_ref[...], preferred_element_type=F32)
    br = jnp.dot(yr_scr[...], wbr_ref[...], preferred_element_type=F32)
    merged = ga_ref[...] * ba + gr_ref[...] * br
    o_ref[...] = x_ref[...] + jnp.dot(merged.astype(BF16), wo_ref[...], preferred_element_type=F32)


def _merge(x, ya, y, bonus, rgate, lnw, lnb, gates, wba, wbr, wo, tm):
    T, D = x.shape
    row = lambda w: pl.BlockSpec((tm, w), lambda i: (i, 0))
    full = lambda a: pl.BlockSpec(a.shape, lambda i: (0, 0))
    return pl.pallas_call(
        _merge_kernel, grid=(T // tm,),
        in_specs=[row(D), row(ya.shape[1]), row(D), row(D), row(D), full(lnw), full(lnb),
                  pl.BlockSpec((tm, D), lambda i: (i, 0)), pl.BlockSpec((tm, D), lambda i: (i, 1)),
                  full(wba), full(wbr), full(wo)],
        out_specs=row(D), out_shape=jax.ShapeDtypeStruct((T, D), F32),
        scratch_shapes=[pltpu.VMEM((tm, D), BF16)],
        compiler_params=_cparams("parallel"), name="merge_out")(
            x, ya, y, bonus, rgate, lnw, lnb, gates, gates, wba, wbr, wo)


def _ffn_kernel(x_ref, xh_ref, g_ref, wu_ref, cw_ref, cb_ref, wd_ref, gf_ref, o_ref, act_scr, *, seq, pieces):
    tm = x_ref.shape[0]
    dff = wd_ref.shape[0]

    def norm(x):
        ms = jnp.mean(x * x, axis=-1, keepdims=True)
        return x * lax.rsqrt(ms + RMS_EPS) * g_ref[...]

    first = (pl.program_id(0) * tm) % seq == 0
    h = jnp.concatenate([jnp.where(first, 0.0, norm(xh_ref[...])).astype(BF16),
                         norm(x_ref[...]).astype(BF16)], axis=0)

    def conv(cols):
        u = jnp.dot(h, wu_ref[:, cols], preferred_element_type=F32)
        c = (u * cw_ref[2:3, cols] + pltpu.roll(u, 1, 0) * cw_ref[1:2, cols]
             + pltpu.roll(u, 2, 0) * cw_ref[0:1, cols])
        return c[HALO:] + cb_ref[:, cols]

    for lo, hi in pieces:
        cg = conv(slice(lo, hi))
        cv = conv(slice(dff + lo, dff + hi))
        act_scr[:, lo:hi] = (cg * _sigmoid(cg) * cv).astype(BF16)

    x2 = x_ref[...] + jnp.dot(act_scr[...], wd_ref[...], preferred_element_type=F32)
    ms = jnp.mean(x2 * x2, axis=-1, keepdims=True)
    o_ref[...] = x2 * lax.rsqrt(ms + RMS_EPS) * gf_ref[...]


def _conv_ffn(x, g, w_up, conv_w, conv_b, w_down, g_final, seq, tm):
    T, D = x.shape
    dff = w_down.shape[0]
    hb = tm // HALO
    mid = (dff // 2 + MXU_TILE - 1) // MXU_TILE * MXU_TILE
    pieces = ((0, mid), (mid, dff))
    resident = lambda a: pl.BlockSpec(a.shape, lambda i: (0, 0), pipeline_mode=pl.Buffered(1))
    in_specs = [pl.BlockSpec((tm, D), lambda i: (i, 0)),
                pl.BlockSpec((HALO, D), lambda i: (jnp.maximum(i * hb - 1, 0), 0)),
                resident(g), resident(w_up), resident(conv_w), resident(conv_b), resident(w_down),
                resident(g_final)]
    return pl.pallas_call(
        functools.partial(_ffn_kernel, seq=seq, pieces=pieces), grid=(T // tm,), in_specs=in_specs,
        out_specs=pl.BlockSpec((tm, D), lambda i: (i, 0)),
        out_shape=jax.ShapeDtypeStruct((T, D), F32),
        scratch_shapes=[pltpu.VMEM((tm, dff), BF16)],
        compiler_params=_cparams("parallel"), name="conv_ffn")(
            x, x, g, w_up, conv_w, conv_b, w_down, g_final)


def _rope_tables(seq):
    half = ROPE_DIM // 2
    lane = jnp.arange(LANES, dtype=jnp.int32) % HEAD_DIM
    expo = (lane % half).astype(F32) * (2.0 / ROPE_DIM)
    freq = jnp.where(lane < ROPE_DIM, jnp.power(ROPE_THETA, -expo), 0.0)
    lo_n = 128
    hi = (jnp.arange(seq // lo_n, dtype=jnp.int32) * lo_n).astype(F32)[:, None] * freq[None, :]
    lo = jnp.arange(lo_n, dtype=jnp.int32).astype(F32)[:, None] * freq[None, :]
    ch, sh, cl, sl = jnp.cos(hi)[:, None, :], jnp.sin(hi)[:, None, :], jnp.cos(lo)[None], jnp.sin(lo)[None]
    cos = (ch * cl - sh * sl).reshape(seq, LANES)
    sin = (sh * cl + ch * sl).reshape(seq, LANES)
    sa = jnp.where((lane >= half) & (lane < ROPE_DIM), sin, 0.0)
    sb = jnp.where(lane < half, -sin, 0.0)
    return cos, sa, sb


def kernel(x, norm_mix_g, w_in, b_gate, mu_shift, w0, w_decay_up, a0, w_a_up, w_g_up, k_k, k_a, r_k,
           ln_x_w, ln_x_b, w_branch_attn, w_branch_rwkv, w_out, norm_ffn_g, w_ffn_up, conv_w, conv_b,
           w_ffn_down, norm_final_g):
    B, S, D = x.shape
    T = B * S
    depth = norm_mix_g.shape[0]
    attn_w = 3 * 4 * HEAD_DIM
    rw = D
    n_decay, n_iclr, n_gate = w_decay_up.shape[1], w_a_up.shape[1], w_g_up.shape[1]
    lora_w = n_decay + n_iclr + n_gate
    lora_pad = 3 * LANES
    c_rwkv = 3 * attn_w
    c_lora = c_rwkv + 3 * rw
    c_gate = c_lora + lora_w

    assert depth == 1 and S % ATTN_TILE == 0 and D == 16 * HEAD_DIM and n_decay + n_iclr == LANES
    xt = x.reshape(T, D)
    cos, sa, sb = _rope_tables(S)
    for l in range(depth):
        wi = w_in[l]
        w_qkv = wi[:, :c_rwkv].astype(BF16)
        w_rkvz = wi[:, c_rwkv:c_lora + lora_pad].astype(BF16)
        w_gt = wi[:, c_gate:].astype(BF16)
        gmix = norm_mix_g[l].reshape(1, D)

        mu = jnp.concatenate([mu_shift[l], jnp.zeros((lora_pad - lora_w,), F32)]).reshape(1, -1)
        qkv, prk, gates = _in_proj(xt, gmix, w_qkv, w_rkvz, w_gt, b_gate[l].reshape(1, -1), mu, (cos, sa, sb), S,
                                   ROW_BLOCK, attn_w, D, rope_tiles=2, n_decay=n_decay, lora_col=3 * rw)

        y_attn = _attention(qkv, B, S)

        params = jnp.concatenate(
            [jnp.stack([w0[l], a0[l], k_k[l], k_a[l], r_k[l].reshape(-1)]), jnp.zeros((3, rw), F32)], axis=0)
        wd = jnp.concatenate([w_decay_up[l], jnp.zeros((LANES - n_decay, rw), F32)], axis=0).astype(BF16)
        wa = jnp.concatenate([jnp.zeros((n_decay, rw), F32), w_a_up[l],
                              jnp.zeros((LANES - n_decay - n_iclr, rw), F32)], axis=0).astype(BF16)
        wg = jnp.concatenate([w_g_up[l], jnp.zeros((2 * LANES - n_gate, rw), F32)], axis=0).astype(BF16)
        wda = jnp.concatenate([wd.reshape(LANES, -1, LANES), wa.reshape(LANES, -1, LANES)], axis=2)
        rp, yp, bonus, gate, pm, qm = _rwkv_chunks(prk, params, wda.transpose(1, 0, 2), wg, RWKV_BLOCK)
        y_scan = _rwkv_scan(rp, yp, pm, qm, B, S, SCAN_BLOCK)

        xt = _merge(xt, y_attn, y_scan, bonus, gate, ln_x_w[l].reshape(1, -1), ln_x_b[l].reshape(1, -1),
                    gates, w_branch_attn[l].astype(BF16),
                    w_branch_rwkv[l].astype(BF16), w_out[l].astype(BF16), ROW_BLOCK)

        xt = _conv_ffn(xt, norm_ffn_g[l].reshape(1, D), w_ffn_up[l].astype(BF16), conv_w[l].reshape(3, -1),
                       conv_b[l].reshape(1, -1), w_ffn_down[l].astype(BF16), norm_final_g.reshape(1, D),
                       S, ROW_BLOCK)
    return xt.reshape(B, S, D)
```

```python
import functools

import jax
import jax.numpy as jnp
from jax import lax
from jax.experimental import pallas as pl
from jax.experimental.pallas import tpu as pltpu

F32 = jnp.float32
BF16 = jnp.bfloat16

HEAD_DIM = 64
LANES = 128
MXU_TILE = 256
ATTN_GROUPS = ((128, 1), (512, 4), (2048, 16))
ATTN_SPAN = 128
ATTN_TILE = 2048
ATTN_UNROLL = 8
ROPE_THETA = 500000.0
ROPE_DIM = HEAD_DIM // 4
RMS_EPS = 1e-6
GN_EPS = 64e-5
CHUNK = 64
DECAY_SCALE = 0.6065306597126334
ROW_TILE_BF16 = 16
HALO = ROW_TILE_BF16
NEG_BIG = -1e30
VMEM_LIMIT = 56 * 1024 * 1024
ROW_BLOCK = 512
RWKV_BLOCK = 1024
SCAN_BLOCK = 512


def _cparams(*sem):
    return pltpu.CompilerParams(dimension_semantics=sem, vmem_limit_bytes=VMEM_LIMIT)


def _mm(a, b):
    return jnp.dot(a.astype(BF16), b.astype(BF16), preferred_element_type=F32)


def _mm_nt(a, b):
    return lax.dot_general(a.astype(BF16), b.astype(BF16), (((1,), (1,)), ((), ())),
                           preferred_element_type=F32)


def _split3(a):
    hi = a.astype(BF16)
    r1 = a - hi.astype(F32)
    mid = r1.astype(BF16)
    lo = (r1 - mid.astype(F32)).astype(BF16)
    return hi, mid, lo


def _mm_exact_lhs(a_bf16, b):
    hi, mid, lo = _split3(b)
    out = jnp.dot(a_bf16, lo, preferred_element_type=F32)
    out = out + jnp.dot(a_bf16, mid, preferred_element_type=F32)
    return out + jnp.dot(a_bf16, hi, preferred_element_type=F32)


def _sigmoid(z):
    return 1.0 / (1.0 + jnp.exp(-z))


def _head_ones(n):
    shift = HEAD_DIM.bit_length() - 1
    r = lax.shift_right_logical(lax.broadcasted_iota(jnp.int32, (n, n), 0), shift)
    c = lax.shift_right_logical(lax.broadcasted_iota(jnp.int32, (n, n), 1), shift)
    return jnp.where(r == c, 1.0, 0.0).astype(BF16)


def _in_proj_kernel(x_ref, xh_ref, g_ref, wq_ref, wr_ref, wg_ref, bg_ref, mu_ref, cos_ref, sa_ref, sb_ref,
                    oq_ref, or_ref, og_ref, *, seq, tq, tg, rope_tiles, n_decay, lora_col):
    tm = x_ref.shape[0]

    def norm(x):
        ms = jnp.mean(x * x, axis=-1, keepdims=True)
        return (x * lax.rsqrt(ms + RMS_EPS) * g_ref[...]).astype(BF16)

    h = norm(x_ref[...])

    first = (pl.program_id(0) * tm) % seq == 0
    o_all = jnp.dot(jnp.concatenate([norm(xh_ref[...]), h], axis=0), wr_ref[...], preferred_element_type=F32)
    row = lax.broadcasted_iota(jnp.int32, (tm, 1), 0)
    lane = lax.broadcasted_iota(jnp.int32, (tm, LANES), 1)
    starts_sequence = (row == 0) & first
    for c in range(wr_ref.shape[1] // LANES):
        cols = slice(c * LANES, (c + 1) * LANES)
        ob_all = o_all[:, cols]
        ob = ob_all[HALO:]
        prev = jnp.where(starts_sequence, 0.0, pltpu.roll(ob_all, 1, 0)[HALO:])
        s = ob + (prev - ob) * mu_ref[:, cols]
        if c * LANES == lora_col:
            s = jnp.where(lane < n_decay, jnp.tanh(s), s)
        elif c * LANES > lora_col:
            s = _sigmoid(s)
        or_ref[:, cols] = s.astype(or_ref.dtype)

    half = ROPE_DIM // 2
    cos, sa, sb = cos_ref[...], sa_ref[...], sb_ref[...]
    for t in range(wq_ref.shape[1] // tq):
        cols = slice(t * tq, (t + 1) * tq)
        o = jnp.dot(h, wq_ref[:, cols], preferred_element_type=F32)
        if t >= rope_tiles:
            oq_ref[:, cols] = o
            continue
        for c in range(tq // LANES):
            ob = o[:, c * LANES:(c + 1) * LANES]
            oq_ref[:, t * tq + c * LANES:t * tq + (c + 1) * LANES] = (
                ob * cos + pltpu.roll(ob, half, 1) * sa + pltpu.roll(ob, LANES - half, 1) * sb)

    for t in range(wg_ref.shape[1] // tg):
        cols = slice(t * tg, (t + 1) * tg)
        o = jnp.dot(h, wg_ref[:, cols], preferred_element_type=F32)
        og_ref[:, cols] = _sigmoid(o + bg_ref[:, cols]).astype(og_ref.dtype)


def _in_proj(x, g, w_qkv, w_rkvz, w_gate, b_gate, mu, rope, seq, tm, tq, tg, rope_tiles, n_decay, lora_col):
    T, D = x.shape
    cos, sa, sb = rope
    nseq = seq // tm
    hb = tm // HALO
    resident = lambda a: pl.BlockSpec(a.shape, lambda i: (0, 0), pipeline_mode=pl.Buffered(1))
    row = lambda w: pl.BlockSpec((tm, w), lambda i: (i, 0))
    t_spec = pl.BlockSpec((tm, LANES), lambda i: (i % nseq, 0))
    in_specs = [row(D), pl.BlockSpec((HALO, D), lambda i: (jnp.maximum(i * hb - 1, 0), 0)),
                resident(g), resident(w_qkv), resident(w_rkvz), resident(w_gate), resident(b_gate), resident(mu),
                t_spec, t_spec, t_spec]
    outs = [(w_qkv.shape[1], F32), (w_rkvz.shape[1], BF16), (w_gate.shape[1], BF16)]
    return pl.pallas_call(
        functools.partial(_in_proj_kernel, seq=seq, tq=tq, tg=tg, rope_tiles=rope_tiles, n_decay=n_decay,
                          lora_col=lora_col),
        grid=(T // tm,), in_specs=in_specs, out_specs=[row(w) for w, _ in outs],
        out_shape=[jax.ShapeDtypeStruct((T, w), dt) for w, dt in outs],
        compiler_params=_cparams("parallel"), name="in_proj")(
            x, x, g, w_qkv, w_rkvz, w_gate, b_gate, mu, cos, sa, sb)


def _attn_kernel(q0, q1, q2, k0, k1, k2, v0, v1, v2, kp0, kp1, kp2, vp0, vp1, vp2, o_ref, obuf, mbuf, lbuf):
    tile = pl.program_id(2)
    q_refs, k_refs, v_refs = (q0, q1, q2), (k0, k1, k2), (v0, v1, v2)
    kp_refs, vp_refs = (kp0, kp1, kp2), (vp0, vp1, vp2)
    L = ATTN_SPAN
    TL = ATTN_TILE
    U = ATTN_UNROLL

    lane = lax.broadcasted_iota(jnp.int32, (L, LANES), 1)
    head_a = lane < HEAD_DIM
    qi = lax.broadcasted_iota(jnp.int32, (L, 2 * L), 0)
    kj = lax.broadcasted_iota(jnp.int32, (L, 2 * L), 1)
    rel = L + qi - kj
    band = (rel >= 0) & (rel <= L)
    band_first = band & (kj >= jnp.where(tile > 0, 0, L))

    def run_blocks(g, d, blocks):
        qs, kcs, vcs, valids = [], [], [], []
        for start, from_prev in blocks:
            rows = pl.ds(start, L, stride=d)
            qs.append(q_refs[g][rows, :] * (HEAD_DIM ** -0.5))
            if from_prev:
                k1, v1 = kp_refs[g][rows, :], vp_refs[g][rows, :]
            else:
                back = pl.ds(start - L * d, L, stride=d)
                k1, v1 = k_refs[g][back, :], v_refs[g][back, :]
            kcs.append(jnp.concatenate([k1, k_refs[g][rows, :]], axis=0).astype(BF16))
            vcs.append(jnp.concatenate([v1, v_refs[g][rows, :]], axis=0).astype(BF16))
            valids.append(band_first if from_prev else band)
        nb = range(len(blocks))
        qh = [[jnp.where(head_a, qs[b], 0.0).astype(BF16), jnp.where(head_a, 0.0, qs[b]).astype(BF16)] for b in nb]
        s = [[lax.dot_general(qh[b][h], kcs[b], (((1,), (1,)), ((), ())), preferred_element_type=F32)
              for h in range(2)] for b in nb]
        s = [[jnp.where(valids[b], s[b][h], NEG_BIG) for h in range(2)] for b in nb]
        m = [[jnp.max(s[b][h], axis=1, keepdims=True) for h in range(2)] for b in nb]
        e = [[jnp.exp(s[b][h] - m[b][h]) for h in range(2)] for b in nb]
        l = [[jnp.sum(e[b][h], axis=1, keepdims=True) for h in range(2)] for b in nb]
        o = [[jnp.dot(e[b][h].astype(BF16), vcs[b], preferred_element_type=F32) for h in range(2)] for b in nb]
        for b, (start, _) in enumerate(blocks):
            rows = pl.ds(start, L, stride=d)
            obuf[g, rows, :] = jnp.where(head_a, o[b][0], o[b][1])
            mbuf[g, rows, :] = jnp.where(head_a, jnp.broadcast_to(m[b][0], (L, LANES)),
                                         jnp.broadcast_to(m[b][1], (L, LANES)))
            lbuf[g, rows, :] = jnp.where(head_a, jnp.broadcast_to(l[b][0], (L, LANES)),
                                         jnp.broadcast_to(l[b][1], (L, LANES)))

    nblk = TL // L
    for g, (_, d) in enumerate(ATTN_GROUPS):
        def cur_body(it, carry, g=g, d=d):
            idxs = [it * U + u for u in range(U)]
            run_blocks(g, d, [((idx % d) + (L * d) * (idx // d), False) for idx in idxs])
            return carry

        if d >= U:
            def prev_body(it, carry, g=g, d=d):
                run_blocks(g, d, [(it * U + u, True) for u in range(U)])
                return carry

            lax.fori_loop(0, d // U, prev_body, 0)
            if d < nblk:
                lax.fori_loop(d // U, nblk // U, cur_body, 0)
        else:
            run_blocks(g, d, [((idx % d) + (L * d) * (idx // d), idx // d == 0) for idx in range(U)])
            lax.fori_loop(1, nblk // U, cur_body, 0)

    MR = 256

    def merge(i, carry):
        rows = pl.ds(pl.multiple_of(i * MR, MR), MR)
        m0, m1, m2 = mbuf[0, rows, :], mbuf[1, rows, :], mbuf[2, rows, :]
        mx = jnp.maximum(jnp.maximum(m0, m1), m2)
        w0, w1, w2 = jnp.exp(m0 - mx), jnp.exp(m1 - mx), jnp.exp(m2 - mx)
        num = w0 * obuf[0, rows, :] + w1 * obuf[1, rows, :] + w2 * obuf[2, rows, :]
        den = w0 * lbuf[0, rows, :] + w1 * lbuf[1, rows, :] + w2 * lbuf[2, rows, :]
        o_ref[rows, :] = (num / den).astype(o_ref.dtype)
        return carry

    lax.fori_loop(0, TL // MR, merge, 0)


def _attention(qkv, batch, seq):
    T = qkv.shape[0]
    TL = ATTN_TILE
    tiles = seq // TL
    npair = 2
    qcols = 3 * 4 * HEAD_DIM // LANES

    def spec(section, g):
        return pl.BlockSpec((TL, LANES),
                            lambda p, b, t, section=section, g=g: (b * tiles + t, section * qcols + 2 * g + p))

    def prev_spec(section, g):
        rows = ATTN_SPAN * ATTN_GROUPS[g][1]
        per_tile = TL // rows
        return pl.BlockSpec(
            (rows, LANES),
            lambda p, b, t, section=section, g=g: (jnp.maximum((b * tiles + t) * per_tile - 1, 0),
                                                   section * qcols + 2 * g + p))

    in_specs = ([spec(s, g) for s in range(3) for g in range(3)]
                + [prev_spec(s, g) for s in (1, 2) for g in range(3)])
    out_spec = pl.BlockSpec((TL, LANES), lambda p, b, t: (b * tiles + t, p))
    return pl.pallas_call(
        _attn_kernel, grid=(npair, batch, tiles), in_specs=in_specs, out_specs=out_spec,
        out_shape=jax.ShapeDtypeStruct((T, npair * LANES), BF16),
        scratch_shapes=[pltpu.VMEM((3, TL, LANES), F32), pltpu.VMEM((3, TL, LANES), F32),
                        pltpu.VMEM((3, TL, LANES), F32)],
        compiler_params=_cparams("parallel", "parallel", "parallel"), name="dilated_attn")(*([qkv] * 15))


def _stack_heads(x, head_a):
    return jnp.concatenate([jnp.where(head_a, x, 0.0), jnp.where(head_a, 0.0, x)], axis=0)


def _rwkv_chunk_kernel(r_ref, k_ref, v_ref, z_ref, par_ref, wda_ref, wg_ref,
                       rp_ref, yp_ref, bonus_ref, gate_ref, p_ref, q_ref):
    tm = r_ref.shape[0]
    C = CHUNK
    r = r_ref[...].astype(F32)
    k = k_ref[...].astype(F32)
    v = v_ref[...].astype(F32)

    w0, a0, k_k, k_a, r_k = (par_ref[i:i + 1, :] for i in range(5))
    da = jnp.dot(z_ref[:, 0:LANES], wda_ref[0], preferred_element_type=F32)
    lw = -DECAY_SCALE * _sigmoid(w0 + da[:, :LANES])
    a = _sigmoid(a0 + da[:, LANES:])
    gate_ref[...] = jnp.dot(z_ref[:, LANES:3 * LANES], wg_ref[...],
                            preferred_element_type=F32).astype(gate_ref.dtype)

    kk = k * k_k
    k2 = k * (1.0 + (a - 1.0) * k_a)
    sums = _mm(jnp.concatenate([kk * kk, r * k2 * r_k], axis=1), _head_ones(2 * LANES))
    kk = kk * lax.rsqrt(jnp.maximum(sums[:, :LANES], 1e-24))
    bonus_ref[...] = (sums[:, LANES:] * v).astype(bonus_ref.dtype)
    beta = kk * a

    ti = lax.broadcasted_iota(jnp.int32, (C, C), 0)
    si = lax.broadcasted_iota(jnp.int32, (C, C), 1)
    tril = jnp.where(si <= ti, 1.0, 0.0).astype(BF16)
    t2 = lax.broadcasted_iota(jnp.int32, (2 * C, 2 * C), 0)
    s2 = lax.broadcasted_iota(jnp.int32, (2 * C, 2 * C), 1)
    same = jnp.where(t2 < C, 0, 1) == jnp.where(s2 < C, 0, 1)
    strict = same & (s2 < t2)
    incl = same & (s2 <= t2)
    eye = jnp.where(t2 == s2, 1.0, 0.0)
    lane = lax.broadcasted_iota(jnp.int32, (C, LANES), 1)
    head_a = lane < HEAD_DIM
    krow = lax.broadcasted_iota(jnp.int32, (C, LANES), 0)
    diag = (lane == krow) | (lane == krow + HEAD_DIM)

    chunks = range(tm // C)
    cum_all = _mm_exact_lhs(tril, jnp.concatenate([lw[c * C:(c + 1) * C] for c in chunks], axis=1))
    xa, xr, yy, lhs_t, vb, gam, zz = [], [], [], [], [], [], []
    for c in chunks:
        sl = slice(c * C, (c + 1) * C)
        lw_c = lw[sl]
        cum_c = cum_all[:, c * LANES:(c + 1) * LANES]
        tot = cum_c[C - 1:C, :]
        g_inc = jnp.exp(cum_c)
        g_prev = jnp.exp(cum_c - lw_c)
        g_inv = jnp.exp(-cum_c)
        g_end = jnp.exp(tot - cum_c)
        gam.append(jnp.exp(tot))
        xa.append(_stack_heads(-kk[sl] * g_prev, head_a))
        xr.append(_stack_heads(r[sl] * g_inc, head_a))
        yb, yk = beta[sl] * g_inv, k2[sl] * g_inv
        yy.append(jnp.concatenate([yb, yb, yk, yk], axis=0))
        bh, kh = beta[sl] * g_end, k2[sl] * g_end
        lhs_t.append(jnp.concatenate([bh, bh, kh, kh], axis=0).T)
        vb.append(_stack_heads(v[sl], head_a))
        zz.append(_mm_nt(jnp.concatenate([xa[c], xr[c]], axis=0), yy[c]))

    l_ab = [jnp.where(strict, zz[c][:2 * C, :2 * C], 0.0) for c in chunks]
    l_ak = [jnp.where(strict, zz[c][:2 * C, 2 * C:], 0.0) for c in chunks]
    m_rbk = [jnp.concatenate([jnp.where(incl, zz[c][2 * C:, :2 * C], 0.0),
                              jnp.where(incl, zz[c][2 * C:, 2 * C:], 0.0)], axis=1) for c in chunks]
    lakv = [_mm(l_ak[c], vb[c]) for c in chunks]

    def live(m, p):
        return m if p == 0 else jnp.concatenate([m[p:C], m[C + p:]], axis=0)

    def put_back(full, part, p, add):
        if p == 0:
            return full + part if add else part
        n = C - p
        top = full[p:C] + part[:n] if add else part[:n]
        bot = full[C + p:] + part[n:] if add else part[n:]
        head = full[:p] if add else jnp.zeros((p, full.shape[1]), F32)
        mid = full[C:C + p] if add else jnp.zeros((p, full.shape[1]), F32)
        return jnp.concatenate([head, top, mid, bot], axis=0)

    lp = [_mm(l_ab[c], l_ab[c]) for c in chunks]
    tinv = [eye + l_ab[c] for c in chunks]
    power = 2
    while 2 * power < C:
        p = power if power % ROW_TILE_BF16 == 0 else 0
        both = [_mm(live(lp[c], p), jnp.concatenate([lp[c], tinv[c]], axis=1)) for c in chunks]
        tinv = [put_back(tinv[c], both[c][:, 2 * C:], p, True) for c in chunks]
        lp = [put_back(lp[c], both[c][:, :2 * C], p, False) for c in chunks]
        power *= 2
    p = power if power % ROW_TILE_BF16 == 0 else 0
    tinv = [put_back(tinv[c], _mm(live(lp[c], p), tinv[c]), p, True) for c in chunks]

    wu = [_mm(tinv[c], jnp.concatenate([xa[c], lakv[c]], axis=1)) for c in chunks]
    zero = jnp.zeros((2 * C, LANES), F32)
    for c in chunks:
        sl = slice(c * C, (c + 1) * C)
        rhs = jnp.concatenate([wu[c], jnp.concatenate([zero, vb[c]], axis=1)], axis=0)
        out = _mm(jnp.concatenate([m_rbk[c], lhs_t[c]], axis=0), rhs)
        rp = xr[c] + out[:2 * C, :LANES]
        yp = out[:2 * C, LANES:]
        rp_ref[sl, :] = (rp[:C] + rp[C:]).astype(rp_ref.dtype)
        yp_ref[sl, :] = (yp[:C] + yp[C:]).astype(yp_ref.dtype)
        pm, qm = out[2 * C:, :LANES], out[2 * C:, LANES:]
        p_ref[c, 0] = jnp.where(head_a, pm[:C], pm[C:]) + jnp.where(diag, gam[c], 0.0)
        q_ref[c, 0] = jnp.where(head_a, qm[:C], qm[C:])


def _rwkv_chunks(prk, params, wda, wg, tm):
    T = prk.shape[0]
    npair = 8
    zblk = 3 * LANES
    zcol = (3 * npair * LANES) // zblk

    def col(base):
        return pl.BlockSpec((tm, LANES), lambda i, j, base=base: (i, base + j))

    in_specs = [col(0), col(npair), col(2 * npair), pl.BlockSpec((tm, zblk), lambda i, j: (i, zcol)),
                pl.BlockSpec((8, LANES), lambda i, j: (0, j)),
                pl.BlockSpec((1, LANES, 2 * LANES), lambda i, j: (j, 0, 0)),
                pl.BlockSpec((2 * LANES, LANES), lambda i, j: (0, j))]
    tok = pl.BlockSpec((tm, LANES), lambda i, j: (i, j))
    mat = pl.BlockSpec((tm // CHUNK, 1, CHUNK, LANES), lambda i, j: (i, j, 0, 0))
    tok_bf16 = jax.ShapeDtypeStruct((T, npair * LANES), BF16)
    mat_shape = jax.ShapeDtypeStruct((T // CHUNK, npair, CHUNK, LANES), F32)
    return pl.pallas_call(
        _rwkv_chunk_kernel, grid=(T // tm, npair),
        in_specs=in_specs, out_specs=[tok, tok, tok, tok, mat, mat],
        out_shape=[tok_bf16, tok_bf16, tok_bf16, tok_bf16, mat_shape, mat_shape],
        compiler_params=_cparams("parallel", "parallel"), name="rwkv_chunks")(
            prk, prk, prk, prk, params, wda, wg)


def _rwkv_scan_kernel(rp_ref, yp_ref, p_ref, q_ref, y_ref, s_scr):
    nseq, npair = s_scr.shape[0], s_scr.shape[1]
    nchunk = p_ref.shape[1]
    C = CHUNK

    @pl.when(pl.program_id(0) == 0)
    def _():
        s_scr[...] = jnp.zeros(s_scr.shape, F32)

    head_a = lax.broadcasted_iota(jnp.int32, (C, LANES), 1) < HEAD_DIM

    def chunk(c, carry):
        rows = pl.ds(pl.multiple_of(c * C, C), C)
        units = [(b, j) for b in range(nseq) for j in range(npair)]
        outs, cross = [], []
        for b, j in units:
            s_u = s_scr[b, j]
            s_hi = s_u.astype(BF16)
            s_lo = (s_u - s_hi.astype(F32)).astype(BF16)
            p = _stack_heads(p_ref[b, c, j], head_a)
            p_hi = p.astype(BF16)
            p_lo = (p - p_hi.astype(F32)).astype(BF16)
            lhs = jnp.concatenate([rp_ref[b, rows, j * LANES:(j + 1) * LANES], p_lo, p_hi], axis=0)
            outs.append(jnp.dot(lhs, s_hi, preferred_element_type=F32))
            cross.append(jnp.dot(p_hi, s_lo, preferred_element_type=F32))
        for (b, j), o, x in zip(units, outs, cross):
            cols = slice(j * LANES, (j + 1) * LANES)
            y_ref[b, rows, cols] = (o[:C] + yp_ref[b, rows, cols]).astype(y_ref.dtype)
            s_scr[b, j] = o[C:3 * C] + x + o[3 * C:] + _stack_heads(q_ref[b, c, j], head_a)
        return carry

    lax.fori_loop(0, nchunk, chunk, 0)


def _rwkv_scan(rp, yp, p, q, batch, seq, tm):
    T, W = rp.shape
    npair = W // LANES
    rp, yp = rp.reshape(batch, seq, W), yp.reshape(batch, seq, W)
    p = p.reshape(batch, seq // CHUNK, npair, CHUNK, LANES)
    q = q.reshape(batch, seq // CHUNK, npair, CHUNK, LANES)
    tok = pl.BlockSpec((batch, tm, W), lambda i: (0, i, 0))
    mat = pl.BlockSpec((batch, tm // CHUNK, npair, CHUNK, LANES), lambda i: (0, i, 0, 0, 0))
    y = pl.pallas_call(
        _rwkv_scan_kernel, grid=(seq // tm,),
        in_specs=[tok, tok, mat, mat], out_specs=tok,
        out_shape=jax.ShapeDtypeStruct((batch, seq, W), BF16),
        scratch_shapes=[pltpu.VMEM((batch, npair, LANES, LANES), F32)],
        compiler_params=_cparams("arbitrary"), name="rwkv_scan")(rp, yp, p, q)
    return y.reshape(T, W)


def _merge_kernel(x_ref, ya_ref, y_ref, bonus_ref, rg_ref, lnw_ref, lnb_ref, ga_ref, gr_ref,
                  wba_ref, wbr_ref, wo_ref, o_ref, yr_scr):
    ones_bd = _head_ones(MXU_TILE)
    inv_n = 1.0 / HEAD_DIM
    for j in range(y_ref.shape[1] // MXU_TILE):
        cols = slice(j * MXU_TILE, (j + 1) * MXU_TILE)
        y = y_ref[:, cols]
        yc = y - _mm(y, ones_bd) * inv_n
        var = _mm(yc * yc, ones_bd) * inv_n
        yn = yc * lax.rsqrt(var + GN_EPS) * lnw_ref[:, cols] + lnb_ref[:, cols]
        yr_scr[:, cols] = ((yn + bonus_ref[:, cols]) * rg_ref[:, cols]).astype(BF16)

    ba = jnp.dot(ya_ref[...], wba_ref[...], preferred_element_type=F32)
    br = jnp.dot(yr_scr[...], wbr_ref[...], preferred_element_type=F32)
    merged = ga_ref[...] * ba + gr_ref[...] * br
    o_ref[...] = x_ref[...] + jnp.dot(merged.astype(BF16), wo_ref[...], preferred_element_type=F32)


def _merge(x, ya, y, bonus, rgate, lnw, lnb, gates, wba, wbr, wo, tm):
    T, D = x.shape
    row = lambda w: pl.BlockSpec((tm, w), lambda i: (i, 0))
    full = lambda a: pl.BlockSpec(a.shape, lambda i: (0, 0))
    return pl.pallas_call(
        _merge_kernel, grid=(T // tm,),
        in_specs=[row(D), row(ya.shape[1]), row(D), row(D), row(D), full(lnw), full(lnb),
                  pl.BlockSpec((tm, D), lambda i: (i, 0)), pl.BlockSpec((tm, D), lambda i: (i, 1)),
                  full(wba), full(wbr), full(wo)],
        out_specs=row(D), out_shape=jax.ShapeDtypeStruct((T, D), F32),
        scratch_shapes=[pltpu.VMEM((tm, D), BF16)],
        compiler_params=_cparams("parallel"), name="merge_out")(
            x, ya, y, bonus, rgate, lnw, lnb, gates, gates, wba, wbr, wo)


def _ffn_kernel(x_ref, xh_ref, g_ref, wu_ref, cw_ref, cb_ref, wd_ref, gf_ref, o_ref, act_scr, *, seq, pieces):
    tm = x_ref.shape[0]
    dff = wd_ref.shape[0]

    def norm(x):
        ms = jnp.mean(x * x, axis=-1, keepdims=True)
        return x * lax.rsqrt(ms + RMS_EPS) * g_ref[...]

    first = (pl.program_id(0) * tm) % seq == 0
    h = jnp.concatenate([jnp.where(first, 0.0, norm(xh_ref[...])).astype(BF16),
                         norm(x_ref[...]).astype(BF16)], axis=0)

    def conv(cols):
        u = jnp.dot(h, wu_ref[:, cols], preferred_element_type=F32)
        c = (u * cw_ref[2:3, cols] + pltpu.roll(u, 1, 0) * cw_ref[1:2, cols]
             + pltpu.roll(u, 2, 0) * cw_ref[0:1, cols])
        return c[HALO:] + cb_ref[:, cols]

    for lo, hi in pieces:
        cg = conv(slice(lo, hi))
        cv = conv(slice(dff + lo, dff + hi))
        act_scr[:, lo:hi] = (cg * _sigmoid(cg) * cv).astype(BF16)

    x2 = x_ref[...] + jnp.dot(act_scr[...], wd_ref[...], preferred_element_type=F32)
    ms = jnp.mean(x2 * x2, axis=-1, keepdims=True)
    o_ref[...] = x2 * lax.rsqrt(ms + RMS_EPS) * gf_ref[...]


def _conv_ffn(x, g, w_up, conv_w, conv_b, w_down, g_final, seq, tm):
    T, D = x.shape
    dff = w_down.shape[0]
    hb = tm // HALO
    mid = (dff // 2 + MXU_TILE - 1) // MXU_TILE * MXU_TILE
    pieces = ((0, mid), (mid, dff))
    resident = lambda a: pl.BlockSpec(a.shape, lambda i: (0, 0), pipeline_mode=pl.Buffered(1))
    in_specs = [pl.BlockSpec((tm, D), lambda i: (i, 0)),
                pl.BlockSpec((HALO, D), lambda i: (jnp.maximum(i * hb - 1, 0), 0)),
                resident(g), resident(w_up), resident(conv_w), resident(conv_b), resident(w_down),
                resident(g_final)]
    return pl.pallas_call(
        functools.partial(_ffn_kernel, seq=seq, pieces=pieces), grid=(T // tm,), in_specs=in_specs,
        out_specs=pl.BlockSpec((tm, D), lambda i: (i, 0)),
        out_shape=jax.ShapeDtypeStruct((T, D), F32),
        scratch_shapes=[pltpu.VMEM((tm, dff), BF16)],
        compiler_params=_cparams("parallel"), name="conv_ffn")(
            x, x, g, w_up, conv_w, conv_b, w_down, g_final)


def _rope_tables(seq):
    half = ROPE_DIM // 2
    lane = jnp.arange(LANES, dtype=jnp.int32) % HEAD_DIM
    expo = (lane % half).astype(F32) * (2.0 / ROPE_DIM)
    freq = jnp.where(lane < ROPE_DIM, jnp.power(ROPE_THETA, -expo), 0.0)
    lo_n = 128
    hi = (jnp.arange(seq // lo_n, dtype=jnp.int32) * lo_n).astype(F32)[:, None] * freq[None, :]
    lo = jnp.arange(lo_n, dtype=jnp.int32).astype(F32)[:, None] * freq[None, :]
    ch, sh, cl, sl = jnp.cos(hi)[:, None, :], jnp.sin(hi)[:, None, :], jnp.cos(lo)[None], jnp.sin(lo)[None]
    cos = (ch * cl - sh * sl).reshape(seq, LANES)
    sin = (sh * cl + ch * sl).reshape(seq, LANES)
    sa = jnp.where((lane >= half) & (lane < ROPE_DIM), sin, 0.0)
    sb = jnp.where(lane < half, -sin, 0.0)
    return cos, sa, sb


def kernel(x, norm_mix_g, w_in, b_gate, mu_shift, w0, w_decay_up, a0, w_a_up, w_g_up, k_k, k_a, r_k,
           ln_x_w, ln_x_b, w_branch_attn, w_branch_rwkv, w_out, norm_ffn_g, w_ffn_up, conv_w, conv_b,
           w_ffn_down, norm_final_g):
    B, S, D = x.shape
    T = B * S
    depth = norm_mix_g.shape[0]
    attn_w = 3 * 4 * HEAD_DIM
    rw = D
    n_decay, n_iclr, n_gate = w_decay_up.shape[1], w_a_up.shape[1], w_g_up.shape[1]
    lora_w = n_decay + n_iclr + n_gate
    lora_pad = 3 * LANES
    c_rwkv = 3 * attn_w
    c_lora = c_rwkv + 3 * rw
    c_gate = c_lora + lora_w

    assert depth == 1 and S % ATTN_TILE == 0 and D == 16 * HEAD_DIM and n_decay + n_iclr == LANES
    xt = x.reshape(T, D)
    cos, sa, sb = _rope_tables(S)
    for l in range(depth):
        wi = w_in[l]
        w_qkv = wi[:, :c_rwkv].astype(BF16)
        w_rkvz = wi[:, c_rwkv:c_lora + lora_pad].astype(BF16)
        w_gt = wi[:, c_gate:].astype(BF16)
        gmix = norm_mix_g[l].reshape(1, D)

        mu = jnp.concatenate([mu_shift[l], jnp.zeros((lora_pad - lora_w,), F32)]).reshape(1, -1)
        qkv, prk, gates = _in_proj(xt, gmix, w_qkv, w_rkvz, w_gt, b_gate[l].reshape(1, -1), mu, (cos, sa, sb), S,
                                   ROW_BLOCK, attn_w, D, rope_tiles=2, n_decay=n_decay, lora_col=3 * rw)

        y_attn = _attention(qkv, B, S)

        params = jnp.concatenate(
            [jnp.stack([w0[l], a0[l], k_k[l], k_a[l], r_k[l].reshape(-1)]), jnp.zeros((3, rw), F32)], axis=0)
        wd = jnp.concatenate([w_decay_up[l], jnp.zeros((LANES - n_decay, rw), F32)], axis=0).astype(BF16)
        wa = jnp.concatenate([jnp.zeros((n_decay, rw), F32), w_a_up[l],
                              jnp.zeros((LANES - n_decay - n_iclr, rw), F32)], axis=0).astype(BF16)
        wg = jnp.concatenate([w_g_up[l], jnp.zeros((2 * LANES - n_gate, rw), F32)], axis=0).astype(BF16)
        wda = jnp.concatenate([wd.reshape(LANES, -1, LANES), wa.reshape(LANES, -1, LANES)], axis=2)
        rp, yp, bonus, gate, pm, qm = _rwkv_chunks(prk, params, wda.transpose(1, 0, 2), wg, RWKV_BLOCK)
        y_scan = _rwkv_scan(rp, yp, pm, qm, B, S, SCAN_BLOCK)

        xt = _merge(xt, y_attn, y_scan, bonus, gate, ln_x_w[l].reshape(1, -1), ln_x_b[l].reshape(1, -1),
                    gates, w_branch_attn[l].astype(BF16),
                    w_branch_rwkv[l].astype(BF16), w_out[l].astype(BF16), ROW_BLOCK)

        xt = _conv_ffn(xt, norm_ffn_g[l].reshape(1, D), w_ffn_up[l].astype(BF16), conv_w[l].reshape(3, -1),
                       conv_b[l].reshape(1, -1), w_ffn_down[l].astype(BF16), norm_final_g.reshape(1, D),
                       S, ROW_BLOCK)
    return xt.reshape(B, S, D)
```

```python
import functools

import jax
import jax.numpy as jnp
from jax import lax
from jax.experimental import pallas as pl
from jax.experimental.pallas import tpu as pltpu

F32 = jnp.float32
BF16 = jnp.bfloat16

HEAD_DIM = 64
LANES = 128
MXU_TILE = 256
ATTN_GROUPS = ((128, 1), (512, 4), (2048, 16))
ATTN_SPAN = 128
ATTN_TILE = 2048
ATTN_UNROLL = 8
ROPE_THETA = 500000.0
ROPE_DIM = HEAD_DIM // 4
RMS_EPS = 1e-6
GN_EPS = 64e-5
CHUNK = 64
DECAY_SCALE = 0.6065306597126334
ROW_TILE_BF16 = 16
HALO = ROW_TILE_BF16
NEG_BIG = -1e30
VMEM_LIMIT = 56 * 1024 * 1024
ROW_BLOCK = 512
RWKV_BLOCK = 2048
SCAN_BLOCK = 512


def _cparams(*sem):
    return pltpu.CompilerParams(dimension_semantics=sem, vmem_limit_bytes=VMEM_LIMIT)


def _mm(a, b):
    return jnp.dot(a.astype(BF16), b.astype(BF16), preferred_element_type=F32)


def _mm_nt(a, b):
    return lax.dot_general(a.astype(BF16), b.astype(BF16), (((1,), (1,)), ((), ())),
                           preferred_element_type=F32)


def _split3(a):
    hi = a.astype(BF16)
    r1 = a - hi.astype(F32)
    mid = r1.astype(BF16)
    lo = (r1 - mid.astype(F32)).astype(BF16)
    return hi, mid, lo


def _mm_exact_lhs(a_bf16, b):
    hi, mid, lo = _split3(b)
    out = jnp.dot(a_bf16, lo, preferred_element_type=F32)
    out = out + jnp.dot(a_bf16, mid, preferred_element_type=F32)
    return out + jnp.dot(a_bf16, hi, preferred_element_type=F32)


def _sigmoid(z):
    return 1.0 / (1.0 + jnp.exp(-z))


def _head_ones(n):
    shift = HEAD_DIM.bit_length() - 1
    r = lax.shift_right_logical(lax.broadcasted_iota(jnp.int32, (n, n), 0), shift)
    c = lax.shift_right_logical(lax.broadcasted_iota(jnp.int32, (n, n), 1), shift)
    return jnp.where(r == c, 1.0, 0.0).astype(BF16)


def _in_proj_kernel(x_ref, xh_ref, g_ref, wq_ref, wr_ref, wg_ref, bg_ref, mu_ref, cos_ref, sa_ref, sb_ref,
                    oq_ref, or_ref, og_ref, *, seq, tq, tg, rope_tiles, n_decay, lora_col):
    tm = x_ref.shape[0]

    def norm(x):
        ms = jnp.mean(x * x, axis=-1, keepdims=True)
        return (x * lax.rsqrt(ms + RMS_EPS) * g_ref[...]).astype(BF16)

    h = norm(x_ref[...])

    first = (pl.program_id(0) * tm) % seq == 0
    o_all = jnp.dot(jnp.concatenate([norm(xh_ref[...]), h], axis=0), wr_ref[...], preferred_element_type=F32)
    row = lax.broadcasted_iota(jnp.int32, (tm, 1), 0)
    lane = lax.broadcasted_iota(jnp.int32, (tm, LANES), 1)
    starts_sequence = (row == 0) & first
    for c in range(wr_ref.shape[1] // LANES):
        cols = slice(c * LANES, (c + 1) * LANES)
        ob_all = o_all[:, cols]
        ob = ob_all[HALO:]
        prev = jnp.where(starts_sequence, 0.0, pltpu.roll(ob_all, 1, 0)[HALO:])
        s = ob + (prev - ob) * mu_ref[:, cols]
        if c * LANES == lora_col:
            s = jnp.where(lane < n_decay, jnp.tanh(s), s)
        elif c * LANES > lora_col:
            s = _sigmoid(s)
        or_ref[:, cols] = s.astype(or_ref.dtype)

    half = ROPE_DIM // 2
    cos, sa, sb = cos_ref[...], sa_ref[...], sb_ref[...]
    for t in range(wq_ref.shape[1] // tq):
        cols = slice(t * tq, (t + 1) * tq)
        o = jnp.dot(h, wq_ref[:, cols], preferred_element_type=F32)
        if t >= rope_tiles:
            oq_ref[:, cols] = o
            continue
        for c in range(tq // LANES):
            ob = o[:, c * LANES:(c + 1) * LANES]
            oq_ref[:, t * tq + c * LANES:t * tq + (c + 1) * LANES] = (
                ob * cos + pltpu.roll(ob, half, 1) * sa + pltpu.roll(ob, LANES - half, 1) * sb)

    for t in range(wg_ref.shape[1] // tg):
        cols = slice(t * tg, (t + 1) * tg)
        o = jnp.dot(h, wg_ref[:, cols], preferred_element_type=F32)
        og_ref[:, cols] = _sigmoid(o + bg_ref[:, cols]).astype(og_ref.dtype)


def _in_proj(x, g, w_qkv, w_rkvz, w_gate, b_gate, mu, rope, seq, tm, tq, tg, rope_tiles, n_decay, lora_col):
    T, D = x.shape
    cos, sa, sb = rope
    nseq = seq // tm
    hb = tm // HALO
    resident = lambda a: pl.BlockSpec(a.shape, lambda i: (0, 0), pipeline_mode=pl.Buffered(1))
    row = lambda w: pl.BlockSpec((tm, w), lambda i: (i, 0))
    t_spec = pl.BlockSpec((tm, LANES), lambda i: (i % nseq, 0))
    in_specs = [row(D), pl.BlockSpec((HALO, D), lambda i: (jnp.maximum(i * hb - 1, 0), 0)),
                resident(g), resident(w_qkv), resident(w_rkvz), resident(w_gate), resident(b_gate), resident(mu),
                t_spec, t_spec, t_spec]
    outs = [(w_qkv.shape[1], F32), (w_rkvz.shape[1], BF16), (w_gate.shape[1], BF16)]
    return pl.pallas_call(
        functools.partial(_in_proj_kernel, seq=seq, tq=tq, tg=tg, rope_tiles=rope_tiles, n_decay=n_decay,
                          lora_col=lora_col),
        grid=(T // tm,), in_specs=in_specs, out_specs=[row(w) for w, _ in outs],
        out_shape=[jax.ShapeDtypeStruct((T, w), dt) for w, dt in outs],
        compiler_params=_cparams("parallel"), name="in_proj")(
            x, x, g, w_qkv, w_rkvz, w_gate, b_gate, mu, cos, sa, sb)


def _attn_kernel(q0, q1, q2, k0, k1, k2, v0, v1, v2, kp0, kp1, kp2, vp0, vp1, vp2, o_ref, obuf, mbuf, lbuf):
    tile = pl.program_id(2)
    q_refs, k_refs, v_refs = (q0, q1, q2), (k0, k1, k2), (v0, v1, v2)
    kp_refs, vp_refs = (kp0, kp1, kp2), (vp0, vp1, vp2)
    L = ATTN_SPAN
    TL = ATTN_TILE
    U = ATTN_UNROLL

    lane = lax.broadcasted_iota(jnp.int32, (L, LANES), 1)
    head_a = lane < HEAD_DIM
    qi = lax.broadcasted_iota(jnp.int32, (L, 2 * L), 0)
    kj = lax.broadcasted_iota(jnp.int32, (L, 2 * L), 1)
    rel = L + qi - kj
    band = (rel >= 0) & (rel <= L)
    band_first = band & (kj >= jnp.where(tile > 0, 0, L))

    def run_blocks(g, d, blocks):
        qs, kcs, vcs, valids = [], [], [], []
        for start, from_prev in blocks:
            rows = pl.ds(start, L, stride=d)
            qs.append(q_refs[g][rows, :] * (HEAD_DIM ** -0.5))
            if from_prev:
                k1, v1 = kp_refs[g][rows, :], vp_refs[g][rows, :]
            else:
                back = pl.ds(start - L * d, L, stride=d)
                k1, v1 = k_refs[g][back, :], v_refs[g][back, :]
            kcs.append(jnp.concatenate([k1, k_refs[g][rows, :]], axis=0).astype(BF16))
            vcs.append(jnp.concatenate([v1, v_refs[g][rows, :]], axis=0).astype(BF16))
            valids.append(band_first if from_prev else band)
        nb = range(len(blocks))
        qh = [[jnp.where(head_a, qs[b], 0.0).astype(BF16), jnp.where(head_a, 0.0, qs[b]).astype(BF16)] for b in nb]
        s = [[lax.dot_general(qh[b][h], kcs[b], (((1,), (1,)), ((), ())), preferred_element_type=F32)
              for h in range(2)] for b in nb]
        s = [[jnp.where(valids[b], s[b][h], NEG_BIG) for h in range(2)] for b in nb]
        m = [[jnp.max(s[b][h], axis=1, keepdims=True) for h in range(2)] for b in nb]
        e = [[jnp.exp(s[b][h] - m[b][h]) for h in range(2)] for b in nb]
        l = [[jnp.sum(e[b][h], axis=1, keepdims=True) for h in range(2)] for b in nb]
        o = [[jnp.dot(e[b][h].astype(BF16), vcs[b], preferred_element_type=F32) for h in range(2)] for b in nb]
        for b, (start, _) in enumerate(blocks):
            rows = pl.ds(start, L, stride=d)
            obuf[g, rows, :] = jnp.where(head_a, o[b][0], o[b][1])
            mbuf[g, rows, :] = jnp.where(head_a, jnp.broadcast_to(m[b][0], (L, LANES)),
                                         jnp.broadcast_to(m[b][1], (L, LANES)))
            lbuf[g, rows, :] = jnp.where(head_a, jnp.broadcast_to(l[b][0], (L, LANES)),
                                         jnp.broadcast_to(l[b][1], (L, LANES)))

    nblk = TL // L
    for g, (_, d) in enumerate(ATTN_GROUPS):
        def cur_body(it, carry, g=g, d=d):
            idxs = [it * U + u for u in range(U)]
            run_blocks(g, d, [((idx % d) + (L * d) * (idx // d), False) for idx in idxs])
            return carry

        if d >= U:
            def prev_body(it, carry, g=g, d=d):
                run_blocks(g, d, [(it * U + u, True) for u in range(U)])
                return carry

            lax.fori_loop(0, d // U, prev_body, 0)
            if d < nblk:
                lax.fori_loop(d // U, nblk // U, cur_body, 0)
        else:
            run_blocks(g, d, [((idx % d) + (L * d) * (idx // d), idx // d == 0) for idx in range(U)])
            lax.fori_loop(1, nblk // U, cur_body, 0)

    MR = 256

    def merge(i, carry):
        rows = pl.ds(pl.multiple_of(i * MR, MR), MR)
        m0, m1, m2 = mbuf[0, rows, :], mbuf[1, rows, :], mbuf[2, rows, :]
        mx = jnp.maximum(jnp.maximum(m0, m1), m2)
        w0, w1, w2 = jnp.exp(m0 - mx), jnp.exp(m1 - mx), jnp.exp(m2 - mx)
        num = w0 * obuf[0, rows, :] + w1 * obuf[1, rows, :] + w2 * obuf[2, rows, :]
        den = w0 * lbuf[0, rows, :] + w1 * lbuf[1, rows, :] + w2 * lbuf[2, rows, :]
        o_ref[rows, :] = (num / den).astype(o_ref.dtype)
        return carry

    lax.fori_loop(0, TL // MR, merge, 0)


def _attention(qkv, batch, seq):
    T = qkv.shape[0]
    TL = ATTN_TILE
    tiles = seq // TL
    npair = 2
    qcols = 3 * 4 * HEAD_DIM // LANES

    def spec(section, g):
        return pl.BlockSpec((TL, LANES),
                            lambda p, b, t, section=section, g=g: (b * tiles + t, section * qcols + 2 * g + p))

    def prev_spec(section, g):
        rows = ATTN_SPAN * ATTN_GROUPS[g][1]
        per_tile = TL // rows
        return pl.BlockSpec(
            (rows, LANES),
            lambda p, b, t, section=section, g=g: (jnp.maximum((b * tiles + t) * per_tile - 1, 0),
                                                   section * qcols + 2 * g + p))

    in_specs = ([spec(s, g) for s in range(3) for g in range(3)]
                + [prev_spec(s, g) for s in (1, 2) for g in range(3)])
    out_spec = pl.BlockSpec((TL, LANES), lambda p, b, t: (b * tiles + t, p))
    return pl.pallas_call(
        _attn_kernel, grid=(npair, batch, tiles), in_specs=in_specs, out_specs=out_spec,
        out_shape=jax.ShapeDtypeStruct((T, npair * LANES), BF16),
        scratch_shapes=[pltpu.VMEM((3, TL, LANES), F32), pltpu.VMEM((3, TL, LANES), F32),
                        pltpu.VMEM((3, TL, LANES), F32)],
        compiler_params=_cparams("parallel", "parallel", "parallel"), name="dilated_attn")(*([qkv] * 15))


def _stack_heads(x, head_a):
    return jnp.concatenate([jnp.where(head_a, x, 0.0), jnp.where(head_a, 0.0, x)], axis=0)


def _rwkv_chunk_kernel(r_ref, k_ref, v_ref, z_ref, par_ref, wda_ref, wg_ref,
                       rp_ref, yp_ref, bonus_ref, gate_ref, p_ref, q_ref):
    tm = r_ref.shape[0]
    C = CHUNK
    r = r_ref[...].astype(F32)
    k = k_ref[...].astype(F32)
    v = v_ref[...].astype(F32)

    w0, a0, k_k, k_a, r_k = (par_ref[i:i + 1, :] for i in range(5))
    da = jnp.dot(z_ref[:, 0:LANES], wda_ref[0], preferred_element_type=F32)
    lw = -DECAY_SCALE * _sigmoid(w0 + da[:, :LANES])
    a = _sigmoid(a0 + da[:, LANES:])
    gate_ref[...] = jnp.dot(z_ref[:, LANES:3 * LANES], wg_ref[...],
                            preferred_element_type=F32).astype(gate_ref.dtype)

    kk = k * k_k
    k2 = k * (1.0 + (a - 1.0) * k_a)
    sums = _mm(jnp.concatenate([kk * kk, r * k2 * r_k], axis=1), _head_ones(2 * LANES))
    kk = kk * lax.rsqrt(jnp.maximum(sums[:, :LANES], 1e-24))
    bonus_ref[...] = (sums[:, LANES:] * v).astype(bonus_ref.dtype)
    beta = kk * a

    ti = lax.broadcasted_iota(jnp.int32, (C, C), 0)
    si = lax.broadcasted_iota(jnp.int32, (C, C), 1)
    tril = jnp.where(si <= ti, 1.0, 0.0).astype(BF16)
    t2 = lax.broadcasted_iota(jnp.int32, (2 * C, 2 * C), 0)
    s2 = lax.broadcasted_iota(jnp.int32, (2 * C, 2 * C), 1)
    same = jnp.where(t2 < C, 0, 1) == jnp.where(s2 < C, 0, 1)
    strict = same & (s2 < t2)
    incl = same & (s2 <= t2)
    eye = jnp.where(t2 == s2, 1.0, 0.0)
    lane = lax.broadcasted_iota(jnp.int32, (C, LANES), 1)
    head_a = lane < HEAD_DIM
    krow = lax.broadcasted_iota(jnp.int32, (C, LANES), 0)
    diag = (lane == krow) | (lane == krow + HEAD_DIM)

    chunks = range(tm // C)
    cum_all = _mm_exact_lhs(tril, jnp.concatenate([lw[c * C:(c + 1) * C] for c in chunks], axis=1))
    xa, xr, yy, lhs_t, vb, gam, zz = [], [], [], [], [], [], []
    for c in chunks:
        sl = slice(c * C, (c + 1) * C)
        lw_c = lw[sl]
        cum_c = cum_all[:, c * LANES:(c + 1) * LANES]
        tot = cum_c[C - 1:C, :]
        g_inc = jnp.exp(cum_c)
        g_prev = jnp.exp(cum_c - lw_c)
        g_inv = jnp.exp(-cum_c)
        g_end = jnp.exp(tot - cum_c)
        gam.append(jnp.exp(tot))
        xa.append(_stack_heads(-kk[sl] * g_prev, head_a))
        xr.append(_stack_heads(r[sl] * g_inc, head_a))
        yb, yk = beta[sl] * g_inv, k2[sl] * g_inv
        yy.append(jnp.concatenate([yb, yb, yk, yk], axis=0))
        bh, kh = beta[sl] * g_end, k2[sl] * g_end
        lhs_t.append(jnp.concatenate([bh, bh, kh, kh], axis=0).T)
        vb.append(_stack_heads(v[sl], head_a))
        zz.append(_mm_nt(jnp.concatenate([xa[c], xr[c]], axis=0), yy[c]))

    l_ab = [jnp.where(strict, zz[c][:2 * C, :2 * C], 0.0) for c in chunks]
    l_ak = [jnp.where(strict, zz[c][:2 * C, 2 * C:], 0.0) for c in chunks]
    m_rbk = [jnp.concatenate([jnp.where(incl, zz[c][2 * C:, :2 * C], 0.0),
                              jnp.where(incl, zz[c][2 * C:, 2 * C:], 0.0)], axis=1) for c in chunks]
    lakv = [_mm(l_ak[c], vb[c]) for c in chunks]

    def live(m, p):
        return m if p == 0 else jnp.concatenate([m[p:C], m[C + p:]], axis=0)

    def put_back(full, part, p, add):
        if p == 0:
            return full + part if add else part
        n = C - p
        top = full[p:C] + part[:n] if add else part[:n]
        bot = full[C + p:] + part[n:] if add else part[n:]
        head = full[:p] if add else jnp.zeros((p, full.shape[1]), F32)
        mid = full[C:C + p] if add else jnp.zeros((p, full.shape[1]), F32)
        return jnp.concatenate([head, top, mid, bot], axis=0)

    lp = [_mm(l_ab[c], l_ab[c]) for c in chunks]
    tinv = [eye + l_ab[c] for c in chunks]
    power = 2
    while 2 * power < C:
        p = power if power % ROW_TILE_BF16 == 0 else 0
        both = [_mm(live(lp[c], p), jnp.concatenate([lp[c], tinv[c]], axis=1)) for c in chunks]
        tinv = [put_back(tinv[c], both[c][:, 2 * C:], p, True) for c in chunks]
        lp = [put_back(lp[c], both[c][:, :2 * C], p, False) for c in chunks]
        power *= 2
    p = power if power % ROW_TILE_BF16 == 0 else 0
    tinv = [put_back(tinv[c], _mm(live(lp[c], p), tinv[c]), p, True) for c in chunks]

    wu = [_mm(tinv[c], jnp.concatenate([xa[c], lakv[c]], axis=1)) for c in chunks]
    zero = jnp.zeros((2 * C, LANES), F32)
    for c in chunks:
        sl = slice(c * C, (c + 1) * C)
        rhs = jnp.concatenate([wu[c], jnp.concatenate([zero, vb[c]], axis=1)], axis=0)
        out = _mm(jnp.concatenate([m_rbk[c], lhs_t[c]], axis=0), rhs)
        rp = xr[c] + out[:2 * C, :LANES]
        yp = out[:2 * C, LANES:]
        rp_ref[sl, :] = (rp[:C] + rp[C:]).astype(rp_ref.dtype)
        yp_ref[sl, :] = (yp[:C] + yp[C:]).astype(yp_ref.dtype)
        pm, qm = out[2 * C:, :LANES], out[2 * C:, LANES:]
        p_ref[c, 0] = jnp.where(head_a, pm[:C], pm[C:]) + jnp.where(diag, gam[c], 0.0)
        q_ref[c, 0] = jnp.where(head_a, qm[:C], qm[C:])


def _rwkv_chunks(prk, params, wda, wg, tm):
    T = prk.shape[0]
    npair = 8
    zblk = 3 * LANES
    zcol = (3 * npair * LANES) // zblk

    def col(base):
        return pl.BlockSpec((tm, LANES), lambda i, j, base=base: (i, base + j))

    in_specs = [col(0), col(npair), col(2 * npair), pl.BlockSpec((tm, zblk), lambda i, j: (i, zcol)),
                pl.BlockSpec((8, LANES), lambda i, j: (0, j)),
                pl.BlockSpec((1, LANES, 2 * LANES), lambda i, j: (j, 0, 0)),
                pl.BlockSpec((2 * LANES, LANES), lambda i, j: (0, j))]
    tok = pl.BlockSpec((tm, LANES), lambda i, j: (i, j))
    mat = pl.BlockSpec((tm // CHUNK, 1, CHUNK, LANES), lambda i, j: (i, j, 0, 0))
    tok_bf16 = jax.ShapeDtypeStruct((T, npair * LANES), BF16)
    mat_shape = jax.ShapeDtypeStruct((T // CHUNK, npair, CHUNK, LANES), F32)
    return pl.pallas_call(
        _rwkv_chunk_kernel, grid=(T // tm, npair),
        in_specs=in_specs, out_specs=[tok, tok, tok, tok, mat, mat],
        out_shape=[tok_bf16, tok_bf16, tok_bf16, tok_bf16, mat_shape, mat_shape],
        compiler_params=_cparams("parallel", "parallel"), name="rwkv_chunks")(
            prk, prk, prk, prk, params, wda, wg)


def _rwkv_scan_kernel(rp_ref, yp_ref, p_ref, q_ref, y_ref, s_scr):
    nseq, npair = s_scr.shape[0], s_scr.shape[1]
    nchunk = p_ref.shape[1]
    C = CHUNK

    @pl.when(pl.program_id(0) == 0)
    def _():
        s_scr[...] = jnp.zeros(s_scr.shape, F32)

    head_a = lax.broadcasted_iota(jnp.int32, (C, LANES), 1) < HEAD_DIM

    def chunk(c, carry):
        rows = pl.ds(pl.multiple_of(c * C, C), C)
        units = [(b, j) for b in range(nseq) for j in range(npair)]
        outs, cross = [], []
        for b, j in units:
            s_u = s_scr[b, j]
            s_hi = s_u.astype(BF16)
            s_lo = (s_u - s_hi.astype(F32)).astype(BF16)
            p = _stack_heads(p_ref[b, c, j], head_a)
            p_hi = p.astype(BF16)
            p_lo = (p - p_hi.astype(F32)).astype(BF16)
            lhs = jnp.concatenate([rp_ref[b, rows, j * LANES:(j + 1) * LANES], p_lo, p_hi], axis=0)
            outs.append(jnp.dot(lhs, s_hi, preferred_element_type=F32))
            cross.append(jnp.dot(p_hi, s_lo, preferred_element_type=F32))
        for (b, j), o, x in zip(units, outs, cross):
            cols = slice(j * LANES, (j + 1) * LANES)
            y_ref[b, rows, cols] = (o[:C] + yp_ref[b, rows, cols]).astype(y_ref.dtype)
            s_scr[b, j] = o[C:3 * C] + x + o[3 * C:] + _stack_heads(q_ref[b, c, j], head_a)
        return carry

    lax.fori_loop(0, nchunk, chunk, 0)


def _rwkv_scan(rp, yp, p, q, batch, seq, tm):
    T, W = rp.shape
    npair = W // LANES
    rp, yp = rp.reshape(batch, seq, W), yp.reshape(batch, seq, W)
    p = p.reshape(batch, seq // CHUNK, npair, CHUNK, LANES)
    q = q.reshape(batch, seq // CHUNK, npair, CHUNK, LANES)
    tok = pl.BlockSpec((batch, tm, W), lambda i: (0, i, 0))
    mat = pl.BlockSpec((batch, tm // CHUNK, npair, CHUNK, LANES), lambda i: (0, i, 0, 0, 0))
    y = pl.pallas_call(
        _rwkv_scan_kernel, grid=(seq // tm,),
        in_specs=[tok, tok, mat, mat], out_specs=tok,
        out_shape=jax.ShapeDtypeStruct((batch, seq, W), BF16),
        scratch_shapes=[pltpu.VMEM((batch, npair, LANES, LANES), F32)],
        compiler_params=_cparams("arbitrary"), name="rwkv_scan")(rp, yp, p, q)
    return y.reshape(T, W)


def _merge_kernel(x_ref, ya_ref, y_ref, bonus_ref, rg_ref, lnw_ref, lnb_ref, ga_ref, gr_ref,
                  wba_ref, wbr_ref, wo_ref, o_ref, yr_scr):
    ones_bd = _head_ones(MXU_TILE)
    inv_n = 1.0 / HEAD_DIM
    for j in range(y_ref.shape[1] // MXU_TILE):
        cols = slice(j * MXU_TILE, (j + 1) * MXU_TILE)
        y = y_ref[:, cols]
        yc = y - _mm(y, ones_bd) * inv_n
        var = _mm(yc * yc, ones_bd) * inv_n
        yn = yc * lax.rsqrt(var + GN_EPS) * lnw_ref[:, cols] + lnb_ref[:, cols]
        yr_scr[:, cols] = ((yn + bonus_ref[:, cols]) * rg_ref[:, cols]).astype(BF16)

    ba = jnp.dot(ya_ref[...], wba_ref[...], preferred_element_type=F32)
    br = jnp.dot(yr_scr[...], wbr_ref[...], preferred_element_type=F32)
    merged = ga_ref[...] * ba + gr_ref[...] * br
    o_ref[...] = x_ref[...] + jnp.dot(merged.astype(BF16), wo_ref[...], preferred_element_type=F32)


def _merge(x, ya, y, bonus, rgate, lnw, lnb, gates, wba, wbr, wo, tm):
    T, D = x.shape
    row = lambda w: pl.BlockSpec((tm, w), lambda i: (i, 0))
    full = lambda a: pl.BlockSpec(a.shape, lambda i: (0, 0))
    return pl.pallas_call(
        _merge_kernel, grid=(T // tm,),
        in_specs=[row(D), row(ya.shape[1]), row(D), row(D), row(D), full(lnw), full(lnb),
                  pl.BlockSpec((tm, D), lambda i: (i, 0)), pl.BlockSpec((tm, D), lambda i: (i, 1)),
                  full(wba), full(wbr), full(wo)],
        out_specs=row(D), out_shape=jax.ShapeDtypeStruct((T, D), F32),
        scratch_shapes=[pltpu.VMEM((tm, D), BF16)],
        compiler_params=_cparams("parallel"), name="merge_out")(
            x, ya, y, bonus, rgate, lnw, lnb, gates, gates, wba, wbr, wo)


def _ffn_kernel(x_ref, xh_ref, g_ref, wu_ref, cw_ref, cb_ref, wd_ref, gf_ref, o_ref, act_scr, *, seq, pieces):
    tm = x_ref.shape[0]
    dff = wd_ref.shape[0]

    def norm(x):
        ms = jnp.mean(x * x, axis=-1, keepdims=True)
        return x * lax.rsqrt(ms + RMS_EPS) * g_ref[...]

    first = (pl.program_id(0) * tm) % seq == 0
    h = jnp.concatenate([jnp.where(first, 0.0, norm(xh_ref[...])).astype(BF16),
                         norm(x_ref[...]).astype(BF16)], axis=0)

    def conv(cols):
        u = jnp.dot(h, wu_ref[:, cols], preferred_element_type=F32)
        c = (u * cw_ref[2:3, cols] + pltpu.roll(u, 1, 0) * cw_ref[1:2, cols]
             + pltpu.roll(u, 2, 0) * cw_ref[0:1, cols])
        return c[HALO:] + cb_ref[:, cols]

    for lo, hi in pieces:
        cg = conv(slice(lo, hi))
        cv = conv(slice(dff + lo, dff + hi))
        act_scr[:, lo:hi] = (cg * _sigmoid(cg) * cv).astype(BF16)

    x2 = x_ref[...] + jnp.dot(act_scr[...], wd_ref[...], preferred_element_type=F32)
    ms = jnp.mean(x2 * x2, axis=-1, keepdims=True)
    o_ref[...] = x2 * lax.rsqrt(ms + RMS_EPS) * gf_ref[...]


def _conv_ffn(x, g, w_up, conv_w, conv_b, w_down, g_final, seq, tm):
    T, D = x.shape
    dff = w_down.shape[0]
    hb = tm // HALO
    mid = (dff // 2 + MXU_TILE - 1) // MXU_TILE * MXU_TILE
    pieces = ((0, mid), (mid, dff))
    resident = lambda a: pl.BlockSpec(a.shape, lambda i: (0, 0), pipeline_mode=pl.Buffered(1))
    in_specs = [pl.BlockSpec((tm, D), lambda i: (i, 0)),
                pl.BlockSpec((HALO, D), lambda i: (jnp.maximum(i * hb - 1, 0), 0)),
                resident(g), resident(w_up), resident(conv_w), resident(conv_b), resident(w_down),
                resident(g_final)]
    return pl.pallas_call(
        functools.partial(_ffn_kernel, seq=seq, pieces=pieces), grid=(T // tm,), in_specs=in_specs,
        out_specs=pl.BlockSpec((tm, D), lambda i: (i, 0)),
        out_shape=jax.ShapeDtypeStruct((T, D), F32),
        scratch_shapes=[pltpu.VMEM((tm, dff), BF16)],
        compiler_params=_cparams("parallel"), name="conv_ffn")(
            x, x, g, w_up, conv_w, conv_b, w_down, g_final)


def _rope_tables(seq):
    half = ROPE_DIM // 2
    lane = jnp.arange(LANES, dtype=jnp.int32) % HEAD_DIM
    expo = (lane % half).astype(F32) * (2.0 / ROPE_DIM)
    freq = jnp.where(lane < ROPE_DIM, jnp.power(ROPE_THETA, -expo), 0.0)
    lo_n = 128
    hi = (jnp.arange(seq // lo_n, dtype=jnp.int32) * lo_n).astype(F32)[:, None] * freq[None, :]
    lo = jnp.arange(lo_n, dtype=jnp.int32).astype(F32)[:, None] * freq[None, :]
    ch, sh, cl, sl = jnp.cos(hi)[:, None, :], jnp.sin(hi)[:, None, :], jnp.cos(lo)[None], jnp.sin(lo)[None]
    cos = (ch * cl - sh * sl).reshape(seq, LANES)
    sin = (sh * cl + ch * sl).reshape(seq, LANES)
    sa = jnp.where((lane >= half) & (lane < ROPE_DIM), sin, 0.0)
    sb = jnp.where(lane < half, -sin, 0.0)
    return cos, sa, sb


def kernel(x, norm_mix_g, w_in, b_gate, mu_shift, w0, w_decay_up, a0, w_a_up, w_g_up, k_k, k_a, r_k,
           ln_x_w, ln_x_b, w_branch_attn, w_branch_rwkv, w_out, norm_ffn_g, w_ffn_up, conv_w, conv_b,
           w_ffn_down, norm_final_g):
    B, S, D = x.shape
    T = B * S
    depth = norm_mix_g.shape[0]
    attn_w = 3 * 4 * HEAD_DIM
    rw = D
    n_decay, n_iclr, n_gate = w_decay_up.shape[1], w_a_up.shape[1], w_g_up.shape[1]
    lora_w = n_decay + n_iclr + n_gate
    lora_pad = 3 * LANES
    c_rwkv = 3 * attn_w
    c_lora = c_rwkv + 3 * rw
    c_gate = c_lora + lora_w

    assert depth == 1 and S % ATTN_TILE == 0 and D == 16 * HEAD_DIM and n_decay + n_iclr == LANES
    xt = x.reshape(T, D)
    cos, sa, sb = _rope_tables(S)
    for l in range(depth):
        wi = w_in[l]
        w_qkv = wi[:, :c_rwkv].astype(BF16)
        w_rkvz = wi[:, c_rwkv:c_lora + lora_pad].astype(BF16)
        w_gt = wi[:, c_gate:].astype(BF16)
        gmix = norm_mix_g[l].reshape(1, D)

        mu = jnp.concatenate([mu_shift[l], jnp.zeros((lora_pad - lora_w,), F32)]).reshape(1, -1)
        qkv, prk, gates = _in_proj(xt, gmix, w_qkv, w_rkvz, w_gt, b_gate[l].reshape(1, -1), mu, (cos, sa, sb), S,
                                   ROW_BLOCK, attn_w, D, rope_tiles=2, n_decay=n_decay, lora_col=3 * rw)

        y_attn = _attention(qkv, B, S)

        params = jnp.concatenate(
            [jnp.stack([w0[l], a0[l], k_k[l], k_a[l], r_k[l].reshape(-1)]), jnp.zeros((3, rw), F32)], axis=0)
        wd = jnp.concatenate([w_decay_up[l], jnp.zeros((LANES - n_decay, rw), F32)], axis=0).astype(BF16)
        wa = jnp.concatenate([jnp.zeros((n_decay, rw), F32), w_a_up[l],
                              jnp.zeros((LANES - n_decay - n_iclr, rw), F32)], axis=0).astype(BF16)
        wg = jnp.concatenate([w_g_up[l], jnp.zeros((2 * LANES - n_gate, rw), F32)], axis=0).astype(BF16)
        wda = jnp.concatenate([wd.reshape(LANES, -1, LANES), wa.reshape(LANES, -1, LANES)], axis=2)
        rp, yp, bonus, gate, pm, qm = _rwkv_chunks(prk, params, wda.transpose(1, 0, 2), wg, RWKV_BLOCK)
        y_scan = _rwkv_scan(rp, yp, pm, qm, B, S, SCAN_BLOCK)

        xt = _merge(xt, y_attn, y_scan, bonus, gate, ln_x_w[l].reshape(1, -1), ln_x_b[l].reshape(1, -1),
                    gates, w_branch_attn[l].astype(BF16),
                    w_branch_rwkv[l].astype(BF16), w_out[l].astype(BF16), ROW_BLOCK)

        xt = _conv_ffn(xt, norm_ffn_g[l].reshape(1, D), w_ffn_up[l].astype(BF16), conv_w[l].reshape(3, -1),
                       conv_b[l].reshape(1, -1), w_ffn_down[l].astype(BF16), norm_final_g.reshape(1, D),
                       S, ROW_BLOCK)
    return xt.reshape(B, S, D)
```

```python
import functools

import jax
import jax.numpy as jnp
from jax import lax
from jax.experimental import pallas as pl
from jax.experimental.pallas import tpu as pltpu

F32 = jnp.float32
BF16 = jnp.bfloat16

HEAD_DIM = 64
LANES = 128
MXU_TILE = 256
ATTN_GROUPS = ((128, 1), (512, 4), (2048, 16))
ATTN_SPAN = 128
ATTN_TILE = 2048
ATTN_UNROLL = 8
ROPE_THETA = 500000.0
ROPE_DIM = HEAD_DIM // 4
RMS_EPS = 1e-6
GN_EPS = 64e-5
CHUNK = 64
DECAY_SCALE = 0.6065306597126334
ROW_TILE_BF16 = 16
HALO = ROW_TILE_BF16
NEG_BIG = -1e30
VMEM_LIMIT = 56 * 1024 * 1024
ROW_BLOCK = 512
RWKV_BLOCK = 2048
SCAN_BLOCK = 512


def _cparams(*sem):
    return pltpu.CompilerParams(dimension_semantics=sem, vmem_limit_bytes=VMEM_LIMIT)


def _mm(a, b):
    return jnp.dot(a.astype(BF16), b.astype(BF16), preferred_element_type=F32)


def _mm_nt(a, b):
    return lax.dot_general(a.astype(BF16), b.astype(BF16), (((1,), (1,)), ((), ())),
                           preferred_element_type=F32)


def _split3(a):
    hi = a.astype(BF16)
    r1 = a - hi.astype(F32)
    mid = r1.astype(BF16)
    lo = (r1 - mid.astype(F32)).astype(BF16)
    return hi, mid, lo


def _mm_exact_lhs(a_bf16, b):
    hi, mid, lo = _split3(b)
    out = jnp.dot(a_bf16, lo, preferred_element_type=F32)
    out = out + jnp.dot(a_bf16, mid, preferred_element_type=F32)
    return out + jnp.dot(a_bf16, hi, preferred_element_type=F32)


def _sigmoid(z):
    return 1.0 / (1.0 + jnp.exp(-z))


def _head_ones(n):
    shift = HEAD_DIM.bit_length() - 1
    r = lax.shift_right_logical(lax.broadcasted_iota(jnp.int32, (n, n), 0), shift)
    c = lax.shift_right_logical(lax.broadcasted_iota(jnp.int32, (n, n), 1), shift)
    return jnp.where(r == c, 1.0, 0.0).astype(BF16)


def _in_proj_kernel(x_ref, xh_ref, g_ref, wq_ref, wr_ref, wg_ref, bg_ref, mu_ref, wgl_ref, cos_ref, sa_ref, sb_ref,
                    oq_ref, or_ref, og_ref, orgate_ref, *, seq, tq, tg, rope_tiles, n_decay, lora_col):
    tm = x_ref.shape[0]

    def norm(x):
        ms = jnp.mean(x * x, axis=-1, keepdims=True)
        return (x * lax.rsqrt(ms + RMS_EPS) * g_ref[...]).astype(BF16)

    h = norm(x_ref[...])

    first = (pl.program_id(0) * tm) % seq == 0
    o_all = jnp.dot(jnp.concatenate([norm(xh_ref[...]), h], axis=0), wr_ref[...], preferred_element_type=F32)
    row = lax.broadcasted_iota(jnp.int32, (tm, 1), 0)
    lane = lax.broadcasted_iota(jnp.int32, (tm, LANES), 1)
    starts_sequence = (row == 0) & first
    gate_in = []
    for c in range(wr_ref.shape[1] // LANES):
        cols = slice(c * LANES, (c + 1) * LANES)
        ob_all = o_all[:, cols]
        ob = ob_all[HALO:]
        prev = jnp.where(starts_sequence, 0.0, pltpu.roll(ob_all, 1, 0)[HALO:])
        s = ob + (prev - ob) * mu_ref[:, cols]
        if c * LANES == lora_col:
            s = jnp.where(lane < n_decay, jnp.tanh(s), s)
        elif c * LANES > lora_col:
            s = _sigmoid(s)
            gate_in.append(s.astype(BF16))
        or_ref[:, cols] = s.astype(or_ref.dtype)
    orgate_ref[...] = jnp.dot(jnp.concatenate(gate_in, axis=1), wgl_ref[...],
                              preferred_element_type=F32).astype(orgate_ref.dtype)

    half = ROPE_DIM // 2
    cos, sa, sb = cos_ref[...], sa_ref[...], sb_ref[...]
    for t in range(wq_ref.shape[1] // tq):
        cols = slice(t * tq, (t + 1) * tq)
        o = jnp.dot(h, wq_ref[:, cols], preferred_element_type=F32)
        if t >= rope_tiles:
            oq_ref[:, cols] = o
            continue
        for c in range(tq // LANES):
            ob = o[:, c * LANES:(c + 1) * LANES]
            oq_ref[:, t * tq + c * LANES:t * tq + (c + 1) * LANES] = (
                ob * cos + pltpu.roll(ob, half, 1) * sa + pltpu.roll(ob, LANES - half, 1) * sb)

    for t in range(wg_ref.shape[1] // tg):
        cols = slice(t * tg, (t + 1) * tg)
        o = jnp.dot(h, wg_ref[:, cols], preferred_element_type=F32)
        og_ref[:, cols] = _sigmoid(o + bg_ref[:, cols]).astype(og_ref.dtype)


def _in_proj(x, g, w_qkv, w_rkvz, w_gate, b_gate, mu, w_glora, rope, seq, tm, tq, tg, rope_tiles, n_decay,
             lora_col):
    T, D = x.shape
    cos, sa, sb = rope
    nseq = seq // tm
    hb = tm // HALO
    resident = lambda a: pl.BlockSpec(a.shape, lambda i: (0, 0), pipeline_mode=pl.Buffered(1))
    row = lambda w: pl.BlockSpec((tm, w), lambda i: (i, 0))
    t_spec = pl.BlockSpec((tm, LANES), lambda i: (i % nseq, 0))
    in_specs = [row(D), pl.BlockSpec((HALO, D), lambda i: (jnp.maximum(i * hb - 1, 0), 0)),
                resident(g), resident(w_qkv), resident(w_rkvz), resident(w_gate), resident(b_gate), resident(mu),
                resident(w_glora), t_spec, t_spec, t_spec]
    outs = [(w_qkv.shape[1], F32), (w_rkvz.shape[1], BF16), (w_gate.shape[1], BF16), (w_glora.shape[1], BF16)]
    return pl.pallas_call(
        functools.partial(_in_proj_kernel, seq=seq, tq=tq, tg=tg, rope_tiles=rope_tiles, n_decay=n_decay,
                          lora_col=lora_col),
        grid=(T // tm,), in_specs=in_specs, out_specs=[row(w) for w, _ in outs],
        out_shape=[jax.ShapeDtypeStruct((T, w), dt) for w, dt in outs],
        compiler_params=_cparams("parallel"), name="in_proj")(
            x, x, g, w_qkv, w_rkvz, w_gate, b_gate, mu, w_glora, cos, sa, sb)


def _attn_kernel(q0, q1, q2, k0, k1, k2, v0, v1, v2, kp0, kp1, kp2, vp0, vp1, vp2, o_ref, obuf, mbuf, lbuf):
    tile = pl.program_id(2)
    q_refs, k_refs, v_refs = (q0, q1, q2), (k0, k1, k2), (v0, v1, v2)
    kp_refs, vp_refs = (kp0, kp1, kp2), (vp0, vp1, vp2)
    L = ATTN_SPAN
    TL = ATTN_TILE
    U = ATTN_UNROLL

    lane = lax.broadcasted_iota(jnp.int32, (L, LANES), 1)
    head_a = lane < HEAD_DIM
    qi = lax.broadcasted_iota(jnp.int32, (L, 2 * L), 0)
    kj = lax.broadcasted_iota(jnp.int32, (L, 2 * L), 1)
    rel = L + qi - kj
    band = (rel >= 0) & (rel <= L)
    band_first = band & (kj >= jnp.where(tile > 0, 0, L))

    def run_blocks(g, d, blocks):
        qs, kcs, vcs, valids = [], [], [], []
        for start, from_prev in blocks:
            rows = pl.ds(start, L, stride=d)
            qs.append(q_refs[g][rows, :] * (HEAD_DIM ** -0.5))
            if from_prev:
                k1, v1 = kp_refs[g][rows, :], vp_refs[g][rows, :]
            else:
                back = pl.ds(start - L * d, L, stride=d)
                k1, v1 = k_refs[g][back, :], v_refs[g][back, :]
            kcs.append(jnp.concatenate([k1, k_refs[g][rows, :]], axis=0).astype(BF16))
            vcs.append(jnp.concatenate([v1, v_refs[g][rows, :]], axis=0).astype(BF16))
            valids.append(band_first if from_prev else band)
        nb = range(len(blocks))
        qh = [[jnp.where(head_a, qs[b], 0.0).astype(BF16), jnp.where(head_a, 0.0, qs[b]).astype(BF16)] for b in nb]
        s = [[lax.dot_general(qh[b][h], kcs[b], (((1,), (1,)), ((), ())), preferred_element_type=F32)
              for h in range(2)] for b in nb]
        s = [[jnp.where(valids[b], s[b][h], NEG_BIG) for h in range(2)] for b in nb]
        m = [[jnp.max(s[b][h], axis=1, keepdims=True) for h in range(2)] for b in nb]
        e = [[jnp.exp(s[b][h] - m[b][h]) for h in range(2)] for b in nb]
        l = [[jnp.sum(e[b][h], axis=1, keepdims=True) for h in range(2)] for b in nb]
        o = [[jnp.dot(e[b][h].astype(BF16), vcs[b], preferred_element_type=F32) for h in range(2)] for b in nb]
        for b, (start, _) in enumerate(blocks):
            rows = pl.ds(start, L, stride=d)
            obuf[g, rows, :] = jnp.where(head_a, o[b][0], o[b][1])
            mbuf[g, rows, :] = jnp.where(head_a, jnp.broadcast_to(m[b][0], (L, LANES)),
                                         jnp.broadcast_to(m[b][1], (L, LANES)))
            lbuf[g, rows, :] = jnp.where(head_a, jnp.broadcast_to(l[b][0], (L, LANES)),
                                         jnp.broadcast_to(l[b][1], (L, LANES)))

    nblk = TL // L
    for g, (_, d) in enumerate(ATTN_GROUPS):
        def cur_body(it, carry, g=g, d=d):
            idxs = [it * U + u for u in range(U)]
            run_blocks(g, d, [((idx % d) + (L * d) * (idx // d), False) for idx in idxs])
            return carry

        if d >= U:
            def prev_body(it, carry, g=g, d=d):
                run_blocks(g, d, [(it * U + u, True) for u in range(U)])
                return carry

            lax.fori_loop(0, d // U, prev_body, 0)
            if d < nblk:
                lax.fori_loop(d // U, nblk // U, cur_body, 0)
        else:
            run_blocks(g, d, [((idx % d) + (L * d) * (idx // d), idx // d == 0) for idx in range(U)])
            lax.fori_loop(1, nblk // U, cur_body, 0)

    MR = 256

    def merge(i, carry):
        rows = pl.ds(pl.multiple_of(i * MR, MR), MR)
        m0, m1, m2 = mbuf[0, rows, :], mbuf[1, rows, :], mbuf[2, rows, :]
        mx = jnp.maximum(jnp.maximum(m0, m1), m2)
        w0, w1, w2 = jnp.exp(m0 - mx), jnp.exp(m1 - mx), jnp.exp(m2 - mx)
        num = w0 * obuf[0, rows, :] + w1 * obuf[1, rows, :] + w2 * obuf[2, rows, :]
        den = w0 * lbuf[0, rows, :] + w1 * lbuf[1, rows, :] + w2 * lbuf[2, rows, :]
        o_ref[rows, :] = (num / den).astype(o_ref.dtype)
        return carry

    lax.fori_loop(0, TL // MR, merge, 0)


def _attention(qkv, batch, seq):
    T = qkv.shape[0]
    TL = ATTN_TILE
    tiles = seq // TL
    npair = 2
    qcols = 3 * 4 * HEAD_DIM // LANES

    def spec(section, g):
        return pl.BlockSpec((TL, LANES),
                            lambda p, b, t, section=section, g=g: (b * tiles + t, section * qcols + 2 * g + p))

    def prev_spec(section, g):
        rows = ATTN_SPAN * ATTN_GROUPS[g][1]
        per_tile = TL // rows
        return pl.BlockSpec(
            (rows, LANES),
            lambda p, b, t, section=section, g=g: (jnp.maximum((b * tiles + t) * per_tile - 1, 0),
                                                   section * qcols + 2 * g + p))

    in_specs = ([spec(s, g) for s in range(3) for g in range(3)]
                + [prev_spec(s, g) for s in (1, 2) for g in range(3)])
    out_spec = pl.BlockSpec((TL, LANES), lambda p, b, t: (b * tiles + t, p))
    return pl.pallas_call(
        _attn_kernel, grid=(npair, batch, tiles), in_specs=in_specs, out_specs=out_spec,
        out_shape=jax.ShapeDtypeStruct((T, npair * LANES), BF16),
        scratch_shapes=[pltpu.VMEM((3, TL, LANES), F32), pltpu.VMEM((3, TL, LANES), F32),
                        pltpu.VMEM((3, TL, LANES), F32)],
        compiler_params=_cparams("parallel", "parallel", "parallel"), name="dilated_attn")(*([qkv] * 15))


def _stack_heads(x, head_a):
    return jnp.concatenate([jnp.where(head_a, x, 0.0), jnp.where(head_a, 0.0, x)], axis=0)


def _rwkv_chunk_kernel(r_ref, k_ref, v_ref, z_ref, par_ref, wda_ref,
                       rp_ref, yp_ref, bonus_ref, p_ref, q_ref):
    tm = r_ref.shape[0]
    C = CHUNK
    r = r_ref[...].astype(F32)
    k = k_ref[...].astype(F32)
    v = v_ref[...].astype(F32)

    w0, a0, k_k, k_a, r_k = (par_ref[i:i + 1, :] for i in range(5))
    da = jnp.dot(z_ref[...], wda_ref[0], preferred_element_type=F32)
    lw = -DECAY_SCALE * _sigmoid(w0 + da[:, :LANES])
    a = _sigmoid(a0 + da[:, LANES:])

    kk = k * k_k
    k2 = k * (1.0 + (a - 1.0) * k_a)
    sums = _mm(jnp.concatenate([kk * kk, r * k2 * r_k], axis=1), _head_ones(2 * LANES))
    kk = kk * lax.rsqrt(jnp.maximum(sums[:, :LANES], 1e-24))
    bonus_ref[...] = (sums[:, LANES:] * v).astype(bonus_ref.dtype)
    beta = kk * a

    ti = lax.broadcasted_iota(jnp.int32, (C, C), 0)
    si = lax.broadcasted_iota(jnp.int32, (C, C), 1)
    tril = jnp.where(si <= ti, 1.0, 0.0).astype(BF16)
    t2 = lax.broadcasted_iota(jnp.int32, (2 * C, 2 * C), 0)
    s2 = lax.broadcasted_iota(jnp.int32, (2 * C, 2 * C), 1)
    same = jnp.where(t2 < C, 0, 1) == jnp.where(s2 < C, 0, 1)
    strict = same & (s2 < t2)
    incl = same & (s2 <= t2)
    eye = jnp.where(t2 == s2, 1.0, 0.0)
    lane = lax.broadcasted_iota(jnp.int32, (C, LANES), 1)
    head_a = lane < HEAD_DIM
    krow = lax.broadcasted_iota(jnp.int32, (C, LANES), 0)
    diag = (lane == krow) | (lane == krow + HEAD_DIM)

    chunks = range(tm // C)
    cum_all = _mm_exact_lhs(tril, jnp.concatenate([lw[c * C:(c + 1) * C] for c in chunks], axis=1))
    xa, xr, yy, lhs_t, vb, gam, zz = [], [], [], [], [], [], []
    for c in chunks:
        sl = slice(c * C, (c + 1) * C)
        lw_c = lw[sl]
        cum_c = cum_all[:, c * LANES:(c + 1) * LANES]
        tot = cum_c[C - 1:C, :]
        g_inc = jnp.exp(cum_c)
        g_prev = jnp.exp(cum_c - lw_c)
        g_inv = jnp.exp(-cum_c)
        g_end = jnp.exp(tot - cum_c)
        gam.append(jnp.exp(tot))
        xa.append(_stack_heads(-kk[sl] * g_prev, head_a))
        xr.append(_stack_heads(r[sl] * g_inc, head_a))
        yb, yk = beta[sl] * g_inv, k2[sl] * g_inv
        yy.append(jnp.concatenate([yb, yb, yk, yk], axis=0))
        bh, kh = beta[sl] * g_end, k2[sl] * g_end
        lhs_t.append(jnp.concatenate([bh, bh, kh, kh], axis=0).T)
        vb.append(_stack_heads(v[sl], head_a))
        zz.append(_mm_nt(jnp.concatenate([xa[c], xr[c]], axis=0), yy[c]))

    l_ab = [jnp.where(strict, zz[c][:2 * C, :2 * C], 0.0) for c in chunks]
    l_ak = [jnp.where(strict, zz[c][:2 * C, 2 * C:], 0.0) for c in chunks]
    m_rbk = [jnp.concatenate([jnp.where(incl, zz[c][2 * C:, :2 * C], 0.0),
                              jnp.where(incl, zz[c][2 * C:, 2 * C:], 0.0)], axis=1) for c in chunks]
    lakv = [_mm(l_ak[c], vb[c]) for c in chunks]

    def live(m, p):
        return m if p == 0 else jnp.concatenate([m[p:C], m[C + p:]], axis=0)

    def put_back(full, part, p, add):
        if p == 0:
            return full + part if add else part
        n = C - p
        top = full[p:C] + part[:n] if add else part[:n]
        bot = full[C + p:] + part[n:] if add else part[n:]
        head = full[:p] if add else jnp.zeros((p, full.shape[1]), F32)
        mid = full[C:C + p] if add else jnp.zeros((p, full.shape[1]), F32)
        return jnp.concatenate([head, top, mid, bot], axis=0)

    lp = [_mm(l_ab[c], l_ab[c]) for c in chunks]
    tinv = [eye + l_ab[c] for c in chunks]
    power = 2
    while 2 * power < C:
        p = power if power % ROW_TILE_BF16 == 0 else 0
        both = [_mm(live(lp[c], p), jnp.concatenate([lp[c], tinv[c]], axis=1)) for c in chunks]
        tinv = [put_back(tinv[c], both[c][:, 2 * C:], p, True) for c in chunks]
        lp = [put_back(lp[c], both[c][:, :2 * C], p, False) for c in chunks]
        power *= 2
    p = power if power % ROW_TILE_BF16 == 0 else 0
    tinv = [put_back(tinv[c], _mm(live(lp[c], p), tinv[c]), p, True) for c in chunks]

    wu = [_mm(tinv[c], jnp.concatenate([xa[c], lakv[c]], axis=1)) for c in chunks]
    zero = jnp.zeros((2 * C, LANES), F32)
    for c in chunks:
        sl = slice(c * C, (c + 1) * C)
        rhs = jnp.concatenate([wu[c], jnp.concatenate([zero, vb[c]], axis=1)], axis=0)
        out = _mm(jnp.concatenate([m_rbk[c], lhs_t[c]], axis=0), rhs)
        rp = xr[c] + out[:2 * C, :LANES]
        yp = out[:2 * C, LANES:]
        rp_ref[sl, :] = (rp[:C] + rp[C:]).astype(rp_ref.dtype)
        yp_ref[sl, :] = (yp[:C] + yp[C:]).astype(yp_ref.dtype)
        pm, qm = out[2 * C:, :LANES], out[2 * C:, LANES:]
        p_ref[c, 0] = jnp.where(head_a, pm[:C], pm[C:]) + jnp.where(diag, gam[c], 0.0)
        q_ref[c, 0] = jnp.where(head_a, qm[:C], qm[C:])


def _rwkv_chunks(prk, params, wda, tm):
    T = prk.shape[0]
    npair = 8

    def col(base):
        return pl.BlockSpec((tm, LANES), lambda i, j, base=base: (i, base + j))

    in_specs = [col(0), col(npair), col(2 * npair), pl.BlockSpec((tm, LANES), lambda i, j: (i, 3 * npair)),
                pl.BlockSpec((8, LANES), lambda i, j: (0, j)),
                pl.BlockSpec((1, LANES, 2 * LANES), lambda i, j: (j, 0, 0))]
    tok = pl.BlockSpec((tm, LANES), lambda i, j: (i, j))
    mat = pl.BlockSpec((tm // CHUNK, 1, CHUNK, LANES), lambda i, j: (i, j, 0, 0))
    tok_bf16 = jax.ShapeDtypeStruct((T, npair * LANES), BF16)
    mat_shape = jax.ShapeDtypeStruct((T // CHUNK, npair, CHUNK, LANES), F32)
    return pl.pallas_call(
        _rwkv_chunk_kernel, grid=(T // tm, npair),
        in_specs=in_specs, out_specs=[tok, tok, tok, mat, mat],
        out_shape=[tok_bf16, tok_bf16, tok_bf16, mat_shape, mat_shape],
        compiler_params=_cparams("parallel", "parallel"), name="rwkv_chunks")(
            prk, prk, prk, prk, params, wda)


def _rwkv_scan_kernel(rp_ref, yp_ref, p_ref, q_ref, y_ref, s_scr):
    nseq, npair = s_scr.shape[0], s_scr.shape[1]
    nchunk = p_ref.shape[1]
    C = CHUNK

    @pl.when(pl.program_id(0) == 0)
    def _():
        s_scr[...] = jnp.zeros(s_scr.shape, F32)

    head_a = lax.broadcasted_iota(jnp.int32, (C, LANES), 1) < HEAD_DIM

    def chunk(c, carry):
        rows = pl.ds(pl.multiple_of(c * C, C), C)
        units = [(b, j) for b in range(nseq) for j in range(npair)]
        outs, cross = [], []
        for b, j in units:
            s_u = s_scr[b, j]
            s_hi = s_u.astype(BF16)
            s_lo = (s_u - s_hi.astype(F32)).astype(BF16)
            p = _stack_heads(p_ref[b, c, j], head_a)
            p_hi = p.astype(BF16)
            p_lo = (p - p_hi.astype(F32)).astype(BF16)
            lhs = jnp.concatenate([rp_ref[b, rows, j * LANES:(j + 1) * LANES], p_lo, p_hi], axis=0)
            outs.append(jnp.dot(lhs, s_hi, preferred_element_type=F32))
            cross.append(jnp.dot(p_hi, s_lo, preferred_element_type=F32))
        for (b, j), o, x in zip(units, outs, cross):
            cols = slice(j * LANES, (j + 1) * LANES)
            y_ref[b, rows, cols] = (o[:C] + yp_ref[b, rows, cols]).astype(y_ref.dtype)
            s_scr[b, j] = o[C:3 * C] + x + o[3 * C:] + _stack_heads(q_ref[b, c, j], head_a)
        return carry

    lax.fori_loop(0, nchunk, chunk, 0)


def _rwkv_scan(rp, yp, p, q, batch, seq, tm):
    T, W = rp.shape
    npair = W // LANES
    rp, yp = rp.reshape(batch, seq, W), yp.reshape(batch, seq, W)
    p = p.reshape(batch, seq // CHUNK, npair, CHUNK, LANES)
    q = q.reshape(batch, seq // CHUNK, npair, CHUNK, LANES)
    tok = pl.BlockSpec((batch, tm, W), lambda i: (0, i, 0))
    mat = pl.BlockSpec((batch, tm // CHUNK, npair, CHUNK, LANES), lambda i: (0, i, 0, 0, 0))
    y = pl.pallas_call(
        _rwkv_scan_kernel, grid=(seq // tm,),
        in_specs=[tok, tok, mat, mat], out_specs=tok,
        out_shape=jax.ShapeDtypeStruct((batch, seq, W), BF16),
        scratch_shapes=[pltpu.VMEM((batch, npair, LANES, LANES), F32)],
        compiler_params=_cparams("arbitrary"), name="rwkv_scan")(rp, yp, p, q)
    return y.reshape(T, W)


def _merge_kernel(x_ref, ya_ref, y_ref, bonus_ref, rg_ref, lnw_ref, lnb_ref, ga_ref, gr_ref,
                  wba_ref, wbr_ref, wo_ref, o_ref, yr_scr):
    ones_bd = _head_ones(MXU_TILE)
    inv_n = 1.0 / HEAD_DIM
    for j in range(y_ref.shape[1] // MXU_TILE):
        cols = slice(j * MXU_TILE, (j + 1) * MXU_TILE)
        y = y_ref[:, cols]
        yc = y - _mm(y, ones_bd) * inv_n
        var = _mm(yc * yc, ones_bd) * inv_n
        yn = yc * lax.rsqrt(var + GN_EPS) * lnw_ref[:, cols] + lnb_ref[:, cols]
        yr_scr[:, cols] = ((yn + bonus_ref[:, cols]) * rg_ref[:, cols]).astype(BF16)

    ba = jnp.dot(ya_ref[...], wba_ref[...], preferred_element_type=F32)
    br = jnp.dot(yr_scr[...], wbr_ref[...], preferred_element_type=F32)
    merged = ga_ref[...] * ba + gr_ref[...] * br
    o_ref[...] = x_ref[...] + jnp.dot(merged.astype(BF16), wo_ref[...], preferred_element_type=F32)


def _merge(x, ya, y, bonus, rgate, lnw, lnb, gates, wba, wbr, wo, tm):
    T, D = x.shape
    row = lambda w: pl.BlockSpec((tm, w), lambda i: (i, 0))
    full = lambda a: pl.BlockSpec(a.shape, lambda i: (0, 0))
    return pl.pallas_call(
        _merge_kernel, grid=(T // tm,),
        in_specs=[row(D), row(ya.shape[1]), row(D), row(D), row(D), full(lnw), full(lnb),
                  pl.BlockSpec((tm, D), lambda i: (i, 0)), pl.BlockSpec((tm, D), lambda i: (i, 1)),
                  full(wba), full(wbr), full(wo)],
        out_specs=row(D), out_shape=jax.ShapeDtypeStruct((T, D), F32),
        scratch_shapes=[pltpu.VMEM((tm, D), BF16)],
        compiler_params=_cparams("parallel"), name="merge_out")(
            x, ya, y, bonus, rgate, lnw, lnb, gates, gates, wba, wbr, wo)


def _ffn_kernel(x_ref, xh_ref, g_ref, wu_ref, cw_ref, cb_ref, wd_ref, gf_ref, o_ref, act_scr, *, seq, pieces):
    tm = x_ref.shape[0]
    dff = wd_ref.shape[0]

    def norm(x):
        ms = jnp.mean(x * x, axis=-1, keepdims=True)
        return x * lax.rsqrt(ms + RMS_EPS) * g_ref[...]

    first = (pl.program_id(0) * tm) % seq == 0
    h = jnp.concatenate([jnp.where(first, 0.0, norm(xh_ref[...])).astype(BF16),
                         norm(x_ref[...]).astype(BF16)], axis=0)

    def conv(cols):
        u = jnp.dot(h, wu_ref[:, cols], preferred_element_type=F32)
        c = (u * cw_ref[2:3, cols] + pltpu.roll(u, 1, 0) * cw_ref[1:2, cols]
             + pltpu.roll(u, 2, 0) * cw_ref[0:1, cols])
        return c[HALO:] + cb_ref[:, cols]

    for lo, hi in pieces:
        cg = conv(slice(lo, hi))
        cv = conv(slice(dff + lo, dff + hi))
        act_scr[:, lo:hi] = (cg * _sigmoid(cg) * cv).astype(BF16)

    x2 = x_ref[...] + jnp.dot(act_scr[...], wd_ref[...], preferred_element_type=F32)
    ms = jnp.mean(x2 * x2, axis=-1, keepdims=True)
    o_ref[...] = x2 * lax.rsqrt(ms + RMS_EPS) * gf_ref[...]


def _conv_ffn(x, g, w_up, conv_w, conv_b, w_down, g_final, seq, tm):
    T, D = x.shape
    dff = w_down.shape[0]
    hb = tm // HALO
    mid = (dff // 2 + MXU_TILE - 1) // MXU_TILE * MXU_TILE
    pieces = ((0, mid), (mid, dff))
    resident = lambda a: pl.BlockSpec(a.shape, lambda i: (0, 0), pipeline_mode=pl.Buffered(1))
    in_specs = [pl.BlockSpec((tm, D), lambda i: (i, 0)),
                pl.BlockSpec((HALO, D), lambda i: (jnp.maximum(i * hb - 1, 0), 0)),
                resident(g), resident(w_up), resident(conv_w), resident(conv_b), resident(w_down),
                resident(g_final)]
    return pl.pallas_call(
        functools.partial(_ffn_kernel, seq=seq, pieces=pieces), grid=(T // tm,), in_specs=in_specs,
        out_specs=pl.BlockSpec((tm, D), lambda i: (i, 0)),
        out_shape=jax.ShapeDtypeStruct((T, D), F32),
        scratch_shapes=[pltpu.VMEM((tm, dff), BF16)],
        compiler_params=_cparams("parallel"), name="conv_ffn")(
            x, x, g, w_up, conv_w, conv_b, w_down, g_final)


def _rope_tables(seq):
    half = ROPE_DIM // 2
    lane = jnp.arange(LANES, dtype=jnp.int32) % HEAD_DIM
    expo = (lane % half).astype(F32) * (2.0 / ROPE_DIM)
    freq = jnp.where(lane < ROPE_DIM, jnp.power(ROPE_THETA, -expo), 0.0)
    lo_n = 128
    hi = (jnp.arange(seq // lo_n, dtype=jnp.int32) * lo_n).astype(F32)[:, None] * freq[None, :]
    lo = jnp.arange(lo_n, dtype=jnp.int32).astype(F32)[:, None] * freq[None, :]
    ch, sh, cl, sl = jnp.cos(hi)[:, None, :], jnp.sin(hi)[:, None, :], jnp.cos(lo)[None], jnp.sin(lo)[None]
    cos = (ch * cl - sh * sl).reshape(seq, LANES)
    sin = (sh * cl + ch * sl).reshape(seq, LANES)
    sa = jnp.where((lane >= half) & (lane < ROPE_DIM), sin, 0.0)
    sb = jnp.where(lane < half, -sin, 0.0)
    return cos, sa, sb


def kernel(x, norm_mix_g, w_in, b_gate, mu_shift, w0, w_decay_up, a0, w_a_up, w_g_up, k_k, k_a, r_k,
           ln_x_w, ln_x_b, w_branch_attn, w_branch_rwkv, w_out, norm_ffn_g, w_ffn_up, conv_w, conv_b,
           w_ffn_down, norm_final_g):
    B, S, D = x.shape
    T = B * S
    depth = norm_mix_g.shape[0]
    attn_w = 3 * 4 * HEAD_DIM
    rw = D
    n_decay, n_iclr, n_gate = w_decay_up.shape[1], w_a_up.shape[1], w_g_up.shape[1]
    lora_w = n_decay + n_iclr + n_gate
    lora_pad = 3 * LANES
    c_rwkv = 3 * attn_w
    c_lora = c_rwkv + 3 * rw
    c_gate = c_lora + lora_w

    assert depth == 1 and S % ATTN_TILE == 0 and D == 16 * HEAD_DIM and n_decay + n_iclr == LANES
    xt = x.reshape(T, D)
    cos, sa, sb = _rope_tables(S)
    for l in range(depth):
        wi = w_in[l]
        w_qkv = wi[:, :c_rwkv].astype(BF16)
        w_rkvz = wi[:, c_rwkv:c_lora + lora_pad].astype(BF16)
        w_gt = wi[:, c_gate:].astype(BF16)
        gmix = norm_mix_g[l].reshape(1, D)

        mu = jnp.concatenate([mu_shift[l], jnp.zeros((lora_pad - lora_w,), F32)]).reshape(1, -1)
        wg = jnp.concatenate([w_g_up[l], jnp.zeros((2 * LANES - n_gate, rw), F32)], axis=0).astype(BF16)
        qkv, prk, gates, gate = _in_proj(xt, gmix, w_qkv, w_rkvz, w_gt, b_gate[l].reshape(1, -1), mu, wg,
                                         (cos, sa, sb), S, ROW_BLOCK, attn_w, D, rope_tiles=2, n_decay=n_decay,
                                         lora_col=3 * rw)

        y_attn = _attention(qkv, B, S)

        params = jnp.concatenate(
            [jnp.stack([w0[l], a0[l], k_k[l], k_a[l], r_k[l].reshape(-1)]), jnp.zeros((3, rw), F32)], axis=0)
        wd = jnp.concatenate([w_decay_up[l], jnp.zeros((LANES - n_decay, rw), F32)], axis=0).astype(BF16)
        wa = jnp.concatenate([jnp.zeros((n_decay, rw), F32), w_a_up[l],
                              jnp.zeros((LANES - n_decay - n_iclr, rw), F32)], axis=0).astype(BF16)
        wda = jnp.concatenate([wd.reshape(LANES, -1, LANES), wa.reshape(LANES, -1, LANES)], axis=2)
        rp, yp, bonus, pm, qm = _rwkv_chunks(prk, params, wda.transpose(1, 0, 2), RWKV_BLOCK)
        y_scan = _rwkv_scan(rp, yp, pm, qm, B, S, SCAN_BLOCK)

        xt = _merge(xt, y_attn, y_scan, bonus, gate, ln_x_w[l].reshape(1, -1), ln_x_b[l].reshape(1, -1),
                    gates, w_branch_attn[l].astype(BF16),
                    w_branch_rwkv[l].astype(BF16), w_out[l].astype(BF16), ROW_BLOCK)

        xt = _conv_ffn(xt, norm_ffn_g[l].reshape(1, D), w_ffn_up[l].astype(BF16), conv_w[l].reshape(3, -1),
                       conv_b[l].reshape(1, -1), w_ffn_down[l].astype(BF16), norm_final_g.reshape(1, D),
                       S, ROW_BLOCK)
    return xt.reshape(B, S, D)
```
